```python
import jax, jax.numpy as jnp
from jax import lax
import numpy as np

D_MODEL = 1024
BATCH = 4
SEQ = 4096
DEPTH = 1

CHUNK = 128
A_GROUPS = 8
A_WIDTH = 1024
A_GROUP_DIM = A_WIDTH // A_GROUPS
N_HEADS = 8
N_KV_HEADS = 2
HEAD_DIM = 128
IDX_HEADS = 8
IDX_DIM = 64
TOPK_MAX = 256
Q_BLOCK = 128
ROPE_THETA = 500000.0
ROT_DIM = HEAD_DIM // 4
IDX_ROT_DIM = IDX_DIM // 4
D_FF = -(-8 * D_MODEL // (3 * 256)) * 256
EPS = 1e-6
NEG_INF = -1e30

IN_SIZES = (A_WIDTH, A_WIDTH, N_HEADS * HEAD_DIM, N_KV_HEADS * HEAD_DIM,
            N_KV_HEADS * HEAD_DIM, IDX_HEADS * IDX_DIM, IDX_DIM, IDX_HEADS,
            D_MODEL, D_MODEL)
IN_COLS = sum(IN_SIZES)

kernel_name = "hybrid_gated_gmlp_dsa_block"


def rms_norm(x, g):
    xf = x.astype(jnp.float32)
    y = xf * lax.rsqrt(jnp.mean(xf * xf, axis=-1, keepdims=True) + EPS)
    return (y * g.astype(jnp.float32)).astype(x.dtype)


def layer_norm(x, g, b):
    xf = x.astype(jnp.float32)
    mu = jnp.mean(xf, axis=-1, keepdims=True)
    var = jnp.mean(jnp.square(xf - mu), axis=-1, keepdims=True)
    y = (xf - mu) * lax.rsqrt(var + EPS)
    return (y * g.astype(jnp.float32) + b.astype(jnp.float32)).astype(x.dtype)


def partial_rope(x, pos, rot_dim):
    inv_freq = ROPE_THETA ** (-jnp.arange(0, rot_dim, 2, dtype=jnp.float32) / rot_dim)
    ang = pos.astype(jnp.float32)[..., None] * inv_freq
    cos = jnp.cos(ang)[:, :, None, :]
    sin = jnp.sin(ang)[:, :, None, :]
    xr = x[..., :rot_dim].astype(jnp.float32)
    x1, x2 = xr[..., : rot_dim // 2], xr[..., rot_dim // 2:]
    rot = jnp.concatenate([x1 * cos - x2 * sin, x2 * cos + x1 * sin], axis=-1)
    return jnp.concatenate([rot.astype(x.dtype), x[..., rot_dim:]], axis=-1)


def chunked_gmlp(u, v, w_s, b_s, ln_g, ln_b):
    bsz, s, _ = u.shape
    n = s // CHUNK
    v = layer_norm(v, ln_g, ln_b).reshape(bsz, n, CHUNK, A_GROUPS, A_GROUP_DIM)
    mask = jnp.tril(jnp.ones((CHUNK, CHUNK), dtype=w_s.dtype))
    mixed = jnp.einsum('gts,bnsgc->bntgc', w_s * mask, v) + b_s.T[None, None, :, :, None]
    return u * mixed.reshape(bsz, s, A_WIDTH)


def dsa_attention(q, k, v, qi, ki, wi):
    bsz, s = q.shape[0], q.shape[1]
    top_k = min(TOPK_MAX, s // 4)
    n_blocks = s // Q_BLOCK
    key_pos = jnp.arange(s)
    ki32 = ki.astype(jnp.float32)
    gather = jax.vmap(lambda kk, ii: kk[ii])

    def block(i):
        start = i * Q_BLOCK
        qb = lax.dynamic_slice_in_dim(q, start, Q_BLOCK, axis=1)
        qib = lax.dynamic_slice_in_dim(qi, start, Q_BLOCK, axis=1).astype(jnp.float32)
        wib = lax.dynamic_slice_in_dim(wi, start, Q_BLOCK, axis=1).astype(jnp.float32)
        qpos = start + jnp.arange(Q_BLOCK)
        causal = key_pos[None, :] <= qpos[:, None]
        logits = jnp.einsum('bthd,bsd->bths', qib, ki32) * (IDX_DIM ** -0.5)
        score = jnp.einsum('bth,bths->bts', wib * (IDX_HEADS ** -0.5), jax.nn.relu(logits))
        score = jnp.where(causal[None], score, -jnp.inf)
        sel_score, sel_idx = lax.top_k(score, top_k)
        valid = jnp.isfinite(sel_score)
        k_sel = gather(k, sel_idx).astype(jnp.float32)
        v_sel = gather(v, sel_idx).astype(jnp.float32)
        qg = qb.reshape(bsz, Q_BLOCK, N_KV_HEADS, N_HEADS // N_KV_HEADS, HEAD_DIM)
        att = jnp.einsum('btgrd,btkgd->btgrk', qg.astype(jnp.float32), k_sel) * (HEAD_DIM ** -0.5)
        att = jnp.where(valid[:, :, None, None, :], att, NEG_INF)
        p = jax.nn.softmax(att, axis=-1)
        o = jnp.einsum('btgrk,btkgd->btgrd', p, v_sel).astype(q.dtype)
        return o.reshape(bsz, Q_BLOCK, N_HEADS * HEAD_DIM)

    out = lax.map(block, jnp.arange(n_blocks))
    return out.transpose(1, 0, 2, 3).reshape(bsz, s, N_HEADS * HEAD_DIM)


def hybrid_layer(x, c, positions, w_ada, b_ada, norm1_g, w_in, gmlp_ln_g, gmlp_ln_b,
                 gmlp_w_s, gmlp_b_s, idx_k_ln_g, idx_k_ln_b, w_proj_a, w_proj_b, w_out,
                 norm2_g, w_ffn_in, w_ffn_out):
    bsz, s, _ = x.shape
    mod = jax.nn.silu(c) @ w_ada + b_ada
    shift1, scale1, gate1, shift2, scale2, gate2 = [m[:, None, :] for m in jnp.split(mod, 6, axis=-1)]

    h = rms_norm(x, norm1_g) * (1.0 + scale1) + shift1
    z = h @ w_in
    split_points = []
    acc = 0
    for sz in IN_SIZES[:-1]:
        acc += sz
        split_points.append(acc)
    a_u, a_v, q, k, v, qi, ki, wi, g_a, g_b = jnp.split(z, split_points, axis=-1)

    y_a = chunked_gmlp(jax.nn.gelu(a_u), jax.nn.gelu(a_v), gmlp_w_s, gmlp_b_s, gmlp_ln_g, gmlp_ln_b)

    q = partial_rope(q.reshape(bsz, s, N_HEADS, HEAD_DIM), positions, ROT_DIM)
    k = partial_rope(k.reshape(bsz, s, N_KV_HEADS, HEAD_DIM), positions, ROT_DIM)
    v = v.reshape(bsz, s, N_KV_HEADS, HEAD_DIM)
    qi = partial_rope(qi.reshape(bsz, s, IDX_HEADS, IDX_DIM), positions, IDX_ROT_DIM)
    ki = layer_norm(ki, idx_k_ln_g, idx_k_ln_b)
    ki = partial_rope(ki[:, :, None, :], positions, IDX_ROT_DIM)[:, :, 0, :]
    y_b = dsa_attention(q, k, v, qi, ki, wi)

    merged = jax.nn.sigmoid(g_a) * (y_a @ w_proj_a) + jax.nn.sigmoid(g_b) * (y_b @ w_proj_b)
    x = x + gate1 * (merged @ w_out)

    h2 = rms_norm(x, norm2_g) * (1.0 + scale2) + shift2
    f_g, f_u = jnp.split(h2 @ w_ffn_in, 2, axis=-1)
    x = x + gate2 * ((jax.nn.silu(f_g) * f_u) @ w_ffn_out)
    return x


def setup_inputs(seed: int = 0) -> dict:
    key = jax.random.key(seed)
    ks = jax.random.split(key, 24)
    f32 = jnp.float32
    nrm = lambda k_, shape, scale: jax.random.normal(k_, shape, f32) * scale
    d = D_MODEL
    x = jax.random.normal(ks[0], (BATCH, SEQ, d), f32)
    c = jax.random.normal(ks[1], (BATCH, d), f32)
    offset = jax.random.randint(ks[2], (BATCH, 1), 0, 1024, dtype=jnp.int32)
    positions = offset + jnp.arange(SEQ, dtype=jnp.int32)[None, :]
    return {
        "x": x,
        "c": c,
        "positions": positions,
        "w_ada": nrm(ks[3], (DEPTH, d, 6 * d), 0.5 * d ** -0.5),
        "b_ada": nrm(ks[4], (DEPTH, 6 * d), 0.02),
        "norm1_g": 1.0 + nrm(ks[5], (DEPTH, d), 0.02),
        "w_in": nrm(ks[6], (DEPTH, d, IN_COLS), d ** -0.5),
        "gmlp_ln_g": 1.0 + nrm(ks[7], (DEPTH, A_WIDTH), 0.02),
        "gmlp_ln_b": nrm(ks[8], (DEPTH, A_WIDTH), 0.02),
        "gmlp_w_s": nrm(ks[9], (DEPTH, A_GROUPS, CHUNK, CHUNK), 0.5 * CHUNK ** -0.5),
        "gmlp_b_s": 1.0 + nrm(ks[10], (DEPTH, A_GROUPS, CHUNK), 0.02),
        "idx_k_ln_g": 1.0 + nrm(ks[11], (DEPTH, IDX_DIM), 0.02),
        "idx_k_ln_b": nrm(ks[12], (DEPTH, IDX_DIM), 0.02),
        "w_proj_a": nrm(ks[13], (DEPTH, A_WIDTH, d), A_WIDTH ** -0.5),
        "w_proj_b": nrm(ks[14], (DEPTH, N_HEADS * HEAD_DIM, d), (N_HEADS * HEAD_DIM) ** -0.5),
        "w_out": nrm(ks[15], (DEPTH, d, d), d ** -0.5),
        "norm2_g": 1.0 + nrm(ks[16], (DEPTH, d), 0.02),
        "w_ffn_in": nrm(ks[17], (DEPTH, d, 2 * D_FF), d ** -0.5),
        "w_ffn_out": nrm(ks[18], (DEPTH, D_FF, d), D_FF ** -0.5),
        "final_norm_g": 1.0 + nrm(ks[19], (d,), 0.02),
    }


def reference(x, c, positions, w_ada, b_ada, norm1_g, w_in, gmlp_ln_g, gmlp_ln_b,
              gmlp_w_s, gmlp_b_s, idx_k_ln_g, idx_k_ln_b, w_proj_a, w_proj_b, w_out,
              norm2_g, w_ffn_in, w_ffn_out, final_norm_g):
    for l in range(DEPTH):
        x = hybrid_layer(x, c, positions, w_ada[l], b_ada[l], norm1_g[l], w_in[l],
                         gmlp_ln_g[l], gmlp_ln_b[l], gmlp_w_s[l], gmlp_b_s[l],
                         idx_k_ln_g[l], idx_k_ln_b[l], w_proj_a[l], w_proj_b[l], w_out[l],
                         norm2_g[l], w_ffn_in[l], w_ffn_out[l])
    return rms_norm(x, final_norm_g)
```

```python
import functools

import numpy as np
import jax
import jax.numpy as jnp
from jax import lax
from jax.experimental import pallas as pl
from jax.experimental.pallas import tpu as pltpu

D_MODEL = 1024
BATCH = 4
SEQ = 4096
CHUNK = 128
A_GROUPS = 8
A_WIDTH = 1024
N_HEADS = 8
N_KV_HEADS = 2
HEAD_DIM = 128
IDX_HEADS = 8
IDX_DIM = 64
TOPK = 256
ROPE_THETA = 500000.0
ROT_DIM = HEAD_DIM // 4
IDX_ROT_DIM = IDX_DIM // 4
D_FF = 2816
EPS = 1e-6
NEG_INF = -1e30

TOKENS = BATCH * SEQ
Q_TILE = 128
KEY_TILE = 256
TM_IN = 256
TM_OUT = 256
HEADS_PER_KV = N_HEADS // N_KV_HEADS
VMEM_LIMIT = 56 * 1024 * 1024

_OFF_U = 0
_OFF_V = _OFF_U + A_WIDTH
_OFF_Q = _OFF_V + A_WIDTH
_OFF_K = _OFF_Q + N_HEADS * HEAD_DIM
_OFF_VV = _OFF_K + N_KV_HEADS * HEAD_DIM
_OFF_QI = _OFF_VV + N_KV_HEADS * HEAD_DIM
_OFF_KI = _OFF_QI + IDX_HEADS * IDX_DIM
_OFF_WI = _OFF_KI + IDX_DIM
_OFF_GA = _OFF_WI + IDX_HEADS
_OFF_GB = _OFF_GA + D_MODEL
_IN_COLS = _OFF_GB + D_MODEL

_T_Q = 0
_T_K = _T_Q + N_HEADS * HEAD_DIM
_T_V = _T_K + N_KV_HEADS * HEAD_DIM
_T_QI = _T_V + N_KV_HEADS * HEAD_DIM
_T_KI = _T_QI + IDX_HEADS * IDX_DIM
_T_WI = _T_KI + IDX_DIM
_T_ROWS = _T_WI + IDX_HEADS

F32 = jnp.float32
BF16 = jnp.bfloat16


def _gelu_tanh(x):
    return 0.5 * x * (1.0 + jnp.tanh(np.sqrt(2.0 / np.pi).astype(np.float32) * (x + 0.044715 * (x * x * x))))


def _sigmoid(x):
    return 1.0 / (1.0 + jnp.exp(-x))


def _rms_norm(x, g):
    return x * lax.rsqrt(jnp.mean(x * x, axis=-1, keepdims=True) + EPS) * g


def _ada_kernel(c_ref, w_ref, b_ref, o_ref):
    c = c_ref[...]
    a = c * _sigmoid(c)
    o_ref[...] = jnp.dot(a, w_ref[...], preferred_element_type=F32,
                         precision=lax.Precision.HIGHEST) + b_ref[...]


def _ada_call(c, w_ada, b_ada):
    n_out = 6 * D_MODEL
    tn = 1024
    return pl.pallas_call(
        _ada_kernel,
        grid=(n_out // tn,),
        in_specs=[pl.BlockSpec((BATCH, D_MODEL), lambda j: (0, 0)),
                  pl.BlockSpec((D_MODEL, tn), lambda j: (0, j)),
                  pl.BlockSpec((1, tn), lambda j: (0, j))],
        out_specs=pl.BlockSpec((BATCH, tn), lambda j: (0, j)),
        out_shape=jax.ShapeDtypeStruct((BATCH, n_out), F32),
        compiler_params=pltpu.CompilerParams(dimension_semantics=("arbitrary",),
                                             vmem_limit_bytes=VMEM_LIMIT),
        name="ada_mod",
    )(c, w_ada, b_ada.reshape(1, n_out))


def _rope_rows(blk, cos, sin, half):
    x1 = blk[0:half]
    x2 = blk[half:2 * half]
    return x1 * cos - x2 * sin, x2 * cos + x1 * sin


def _inproj_kernel(x_ref, mod_ref, pos_ref, n1g_ref, wn_ref, wt_ref, lng_ref, lnb_ref,
                   kig_ref, kib_ref, invf_ref,
                   u_ref, vln_ref, sga_ref, sgb_ref, qT_ref, k_ref, vT_ref, qiT_ref, ki_ref, wiT_ref):
    tm = x_ref.shape[0]
    x = x_ref[...]
    mod = mod_ref[0]
    shift1 = mod[:, 0:D_MODEL]
    scale1 = mod[:, D_MODEL:2 * D_MODEL]
    h = _rms_norm(x, n1g_ref[...]) * (1.0 + scale1) + shift1
    hb = h.astype(BF16)

    zu = jnp.dot(hb, wn_ref[:, 0:A_WIDTH], preferred_element_type=F32)
    u_ref[...] = _gelu_tanh(zu).astype(BF16)
    zv = _gelu_tanh(jnp.dot(hb, wn_ref[:, A_WIDTH:2 * A_WIDTH], preferred_element_type=F32))
    mu = jnp.mean(zv, axis=-1, keepdims=True)
    zc = zv - mu
    var = jnp.mean(zc * zc, axis=-1, keepdims=True)
    vln_ref[...] = (zc * lax.rsqrt(var + EPS) * lng_ref[...] + lnb_ref[...]).astype(BF16)
    zga = jnp.dot(hb, wn_ref[:, 2 * A_WIDTH:2 * A_WIDTH + D_MODEL], preferred_element_type=F32)
    sga_ref[...] = _sigmoid(zga).astype(BF16)
    zgb = jnp.dot(hb, wn_ref[:, 2 * A_WIDTH + D_MODEL:], preferred_element_type=F32)
    sgb_ref[...] = _sigmoid(zgb).astype(BF16)

    nt = (((1,), (1,)), ((), ()))
    pos = pos_ref[0].astype(F32)
    ang = invf_ref[...] * pos
    cos = jnp.cos(ang)
    sin = jnp.sin(ang)
    hq = ROT_DIM // 2
    hi = IDX_ROT_DIM // 2
    cos_q, sin_q = cos[0:hq], sin[0:hq]
    cos_i, sin_i = cos[hq:hq + hi], sin[hq:hq + hi]
    n_sub = tm // Q_TILE

    zq = lax.dot_general(wt_ref[_T_Q:_T_K, :], hb, nt, preferred_element_type=F32)
    q_scale = HEAD_DIM ** -0.5
    for hd in range(N_HEADS):
        blk = zq[hd * HEAD_DIM:(hd + 1) * HEAD_DIM]
        r1, r2 = _rope_rows(blk, cos_q, sin_q, hq)
        full = (jnp.concatenate([r1, r2, blk[ROT_DIM:]], axis=0) * q_scale).astype(BF16)
        for s in range(n_sub):
            qT_ref[s, :, hd * Q_TILE:(hd + 1) * Q_TILE] = full[:, s * Q_TILE:(s + 1) * Q_TILE]

    zk = lax.dot_general(wt_ref[_T_K:_T_V, :], hb, nt, preferred_element_type=F32)
    k_rows = []
    for g in range(N_KV_HEADS):
        blk = zk[g * HEAD_DIM:(g + 1) * HEAD_DIM]
        r1, r2 = _rope_rows(blk, cos_q, sin_q, hq)
        k_rows += [r1, r2, blk[ROT_DIM:]]
    k_ref[...] = jnp.concatenate(k_rows, axis=0).T.astype(BF16)

    zvv = lax.dot_general(wt_ref[_T_V:_T_QI, :], hb, nt, preferred_element_type=F32)
    vT_ref[...] = zvv.astype(BF16)

    zqi = lax.dot_general(wt_ref[_T_QI:_T_KI, :], hb, nt, preferred_element_type=F32)
    for hd in range(IDX_HEADS):
        blk = zqi[hd * IDX_DIM:(hd + 1) * IDX_DIM]
        r1, r2 = _rope_rows(blk, cos_i, sin_i, hi)
        full = jnp.concatenate([r1, r2, blk[IDX_ROT_DIM:]], axis=0).astype(BF16)
        for s in range(n_sub):
            qiT_ref[s, :, hd * Q_TILE:(hd + 1) * Q_TILE] = full[:, s * Q_TILE:(s + 1) * Q_TILE]

    zrest = lax.dot_general(wt_ref[_T_KI:_T_ROWS, :], hb, nt, preferred_element_type=F32)
    zki = zrest[0:IDX_DIM]
    kmu = jnp.mean(zki, axis=0, keepdims=True)
    kc = zki - kmu
    kvar = jnp.mean(kc * kc, axis=0, keepdims=True)
    kin = kc * lax.rsqrt(kvar + EPS) * kig_ref[...] + kib_ref[...]
    r1, r2 = _rope_rows(kin, cos_i, sin_i, hi)
    ki_full = jnp.concatenate([r1, r2, kin[IDX_ROT_DIM:], jnp.zeros((128 - IDX_DIM, tm), F32)], axis=0)
    ki_ref[...] = ki_full.T.astype(BF16)
    wiT_ref[...] = zrest[IDX_DIM:IDX_DIM + IDX_HEADS] * ((IDX_HEADS ** -0.5) * (IDX_DIM ** -0.5))


def _inproj_call(x2, mod3, pos3, n1g, wn, wt, lng, lnb, kig, kib, invf):
    tm = TM_IN
    n_tiles = TOKENS // tm
    per_b = SEQ // tm
    n_sub = tm // Q_TILE
    const2 = lambda t: (0, 0)
    row = lambda t: (t, 0)
    in_specs = [
        pl.BlockSpec((tm, D_MODEL), row),
        pl.BlockSpec((1, 1, 6 * D_MODEL), lambda t: (t // per_b, 0, 0)),
        pl.BlockSpec((1, 1, tm), lambda t: (t // per_b, 0, t % per_b)),
        pl.BlockSpec((1, D_MODEL), const2),
        pl.BlockSpec(wn.shape, const2),
        pl.BlockSpec(wt.shape, const2),
        pl.BlockSpec((1, A_WIDTH), const2),
        pl.BlockSpec((1, A_WIDTH), const2),
        pl.BlockSpec((IDX_DIM, 1), const2),
        pl.BlockSpec((IDX_DIM, 1), const2),
        pl.BlockSpec(invf.shape, const2),
    ]
    out_shape = [
        jax.ShapeDtypeStruct((TOKENS, A_WIDTH), BF16),
        jax.ShapeDtypeStruct((TOKENS, A_WIDTH), BF16),
        jax.ShapeDtypeStruct((TOKENS, D_MODEL), BF16),
        jax.ShapeDtypeStruct((TOKENS, D_MODEL), BF16),
        jax.ShapeDtypeStruct((TOKENS // Q_TILE, HEAD_DIM, N_HEADS * Q_TILE), BF16),
        jax.ShapeDtypeStruct((TOKENS, N_KV_HEADS * HEAD_DIM), BF16),
        jax.ShapeDtypeStruct((N_KV_HEADS * HEAD_DIM, TOKENS), BF16),
        jax.ShapeDtypeStruct((TOKENS // Q_TILE, IDX_DIM, IDX_HEADS * Q_TILE), BF16),
        jax.ShapeDtypeStruct((TOKENS, 128), BF16),
        jax.ShapeDtypeStruct((IDX_HEADS, TOKENS), F32),
    ]
    out_specs = [
        pl.BlockSpec((tm, A_WIDTH), row),
        pl.BlockSpec((tm, A_WIDTH), row),
        pl.BlockSpec((tm, D_MODEL), row),
        pl.BlockSpec((tm, D_MODEL), row),
        pl.BlockSpec((n_sub, HEAD_DIM, N_HEADS * Q_TILE), lambda t: (t, 0, 0)),
        pl.BlockSpec((tm, N_KV_HEADS * HEAD_DIM), row),
        pl.BlockSpec((N_KV_HEADS * HEAD_DIM, tm), lambda t: (0, t)),
        pl.BlockSpec((n_sub, IDX_DIM, IDX_HEADS * Q_TILE), lambda t: (t, 0, 0)),
        pl.BlockSpec((tm, 128), row),
        pl.BlockSpec((IDX_HEADS, tm), lambda t: (0, t)),
    ]
    return pl.pallas_call(
        _inproj_kernel,
        grid=(n_tiles,),
        in_specs=in_specs,
        out_specs=out_specs,
        out_shape=out_shape,
        compiler_params=pltpu.CompilerParams(dimension_semantics=("arbitrary",),
                                             vmem_limit_bytes=VMEM_LIMIT),
        name="in_proj",
    )(x2, mod3, pos3, n1g, wn, wt, lng, lnb, kig, kib, invf)


def _col_reduce(x, op):
    return op(x.reshape(x.shape[0] // 8, 8, x.shape[1]), axis=0)


def _attn_kernel(qT_ref, qiT_ref, wiT_ref, k_ref, ki_ref, vT_ref, y_ref, sc_ref, att_ref, acc_ref):
    i = pl.program_id(1)
    nkt = (i + 2) // 2
    kf = float(TOPK)

    qiT = qiT_ref[0]
    w = wiT_ref[...]
    wrow = jnp.concatenate([w[hd:hd + 1, :] for hd in range(IDX_HEADS)], axis=1)

    def idx_body(j, carry):
        kt = ki_ref[pl.ds(pl.multiple_of(j * KEY_TILE, KEY_TILE), KEY_TILE), 0:IDX_DIM]
        lg = jnp.dot(kt, qiT, preferred_element_type=F32)
        r = jnp.maximum(lg, 0.0) * wrow
        s = r[:, 0:Q_TILE]
        for hd in range(1, IDX_HEADS):
            s = s + r[:, hd * Q_TILE:(hd + 1) * Q_TILE]
        sc_ref[pl.ds(pl.multiple_of(j * KEY_TILE, KEY_TILE), KEY_TILE), :] = s
        return carry

    lax.fori_loop(0, nkt, idx_body, 0)

    last = pl.multiple_of((nkt - 1) * KEY_TILE, KEY_TILE)
    s_last = sc_ref[pl.ds(last, KEY_TILE), :]
    kidx = last + lax.broadcasted_iota(jnp.int32, (KEY_TILE, Q_TILE), 0)
    qidx = i * Q_TILE + lax.broadcasted_iota(jnp.int32, (KEY_TILE, Q_TILE), 1)
    sc_ref[pl.ds(last, KEY_TILE), :] = jnp.where(kidx <= qidx, s_last, -jnp.inf)

    def tiles(fn, init):
        def body(j, carry):
            s = sc_ref[pl.ds(pl.multiple_of(j * KEY_TILE, KEY_TILE), KEY_TILE), :]
            return fn(s, carry)
        return lax.fori_loop(0, nkt, body, init)

    def count_ge(t):
        c8 = tiles(lambda s, c: c + _col_reduce(jnp.where(s >= t, 1.0, 0.0), jnp.sum),
                   jnp.zeros((8, Q_TILE), F32))
        return jnp.sum(c8, axis=0, keepdims=True)

    def bisect(lo, hi, n):
        def step(_, c):
            lo, hi = c
            mid = 0.5 * lo + 0.5 * hi
            ok = count_ge(mid) >= kf
            return jnp.where(ok, mid, lo), jnp.where(ok, hi, mid)
        return lax.fori_loop(0, n, step, (lo, hi))

    @pl.when(i < TOPK // Q_TILE)
    def _():
        def body(j, carry):
            sl = pl.ds(pl.multiple_of(j * KEY_TILE, KEY_TILE), KEY_TILE)
            sc_ref[sl, :] = jnp.where(sc_ref[sl, :] > -jnp.inf, 0.0, NEG_INF)
            return carry
        lax.fori_loop(0, nkt, body, 0)

    @pl.when(i >= TOPK // Q_TILE)
    def _():
        def mm(s, c):
            mx8, mn8 = c
            mx8 = jnp.maximum(mx8, _col_reduce(s, jnp.max))
            mn8 = jnp.minimum(mn8, _col_reduce(jnp.where(s > -jnp.inf, s, jnp.inf), jnp.min))
            return mx8, mn8
        mx8, mn8 = tiles(mm, (jnp.full((8, Q_TILE), -jnp.inf, F32), jnp.full((8, Q_TILE), jnp.inf, F32)))
        mx = jnp.max(mx8, axis=0, keepdims=True)
        mn = jnp.min(mn8, axis=0, keepdims=True)
        lo = jnp.where(count_ge(mx) >= kf, mx, mn)
        hi = mx
        lo, hi = bisect(lo, hi, 16)

        def check(lo, hi):
            def f(s, c):
                a8, b8 = c
                a8 = jnp.maximum(a8, _col_reduce(jnp.where(s < hi, s, -jnp.inf), jnp.max))
                b8 = jnp.minimum(b8, _col_reduce(jnp.where(s >= lo, s, jnp.inf), jnp.min))
                return a8, b8
            a8, b8 = tiles(f, (jnp.full((8, Q_TILE), -jnp.inf, F32), jnp.full((8, Q_TILE), jnp.inf, F32)))
            below_hi = jnp.max(a8, axis=0, keepdims=True)
            above_lo = jnp.min(b8, axis=0, keepdims=True)
            conv = (lo == hi) | (below_hi == above_lo)
            v = jnp.where(lo == hi, lo, above_lo)
            return v, jnp.min(jnp.where(conv, 1.0, 0.0))

        def w_cond(c):
            return c[3] < 0.5

        def w_body(c):
            lo, hi, _, _ = c
            lo, hi = bisect(lo, hi, 4)
            v, done = check(lo, hi)
            return lo, hi, v, done

        _, _, vk, _ = lax.while_loop(w_cond, w_body, (lo, hi, lo, jnp.float32(0.0)))

        def cnt2(s, c):
            g8, e8 = c
            g8 = g8 + _col_reduce(jnp.where(s > vk, 1.0, 0.0), jnp.sum)
            e8 = e8 + _col_reduce(jnp.where(s == vk, 1.0, 0.0), jnp.sum)
            return g8, e8
        g8, e8 = tiles(cnt2, (jnp.zeros((8, Q_TILE), F32), jnp.zeros((8, Q_TILE), F32)))
        n_gt = jnp.sum(g8, axis=0, keepdims=True)
        n_eq = jnp.sum(e8, axis=0, keepdims=True)
        need = kf - n_gt
        has_tie = jnp.max(n_eq - need) > 0.5

        @pl.when(jnp.logical_not(has_tie))
        def _():
            def body(j, carry):
                sl = pl.ds(pl.multiple_of(j * KEY_TILE, KEY_TILE), KEY_TILE)
                sc_ref[sl, :] = jnp.where(sc_ref[sl, :] >= vk, 0.0, NEG_INF)
                return carry
            lax.fori_loop(0, nkt, body, 0)

        @pl.when(has_tie)
        def _():
            r_i = lax.broadcasted_iota(jnp.int32, (KEY_TILE, KEY_TILE), 0)
            c_i = lax.broadcasted_iota(jnp.int32, (KEY_TILE, KEY_TILE), 1)
            tri = jnp.where(r_i >= c_i, 1.0, 0.0).astype(BF16)

            def body(j, seen):
                sl = pl.ds(pl.multiple_of(j * KEY_TILE, KEY_TILE), KEY_TILE)
                s = sc_ref[sl, :]
                eq = s == vk
                rank = jnp.dot(tri, jnp.where(eq, 1.0, 0.0).astype(BF16), preferred_element_type=F32) + seen
                sel = (s > vk) | (eq & (rank <= need))
                sc_ref[sl, :] = jnp.where(sel, 0.0, NEG_INF)
                return rank[KEY_TILE - 1:KEY_TILE, :]
            lax.fori_loop(0, nkt, body, jnp.zeros((1, Q_TILE), F32))

    gw = HEADS_PER_KV * Q_TILE
    for g in range(N_KV_HEADS):
        qT = qT_ref[0, :, g * gw:(g + 1) * gw]

        def p1(j, m8):
            sl = pl.ds(pl.multiple_of(j * KEY_TILE, KEY_TILE), KEY_TILE)
            kt = k_ref[sl, g * HEAD_DIM:(g + 1) * HEAD_DIM]
            a = jnp.dot(kt, qT, preferred_element_type=F32)
            bias = sc_ref[sl, :]
            a = a + jnp.concatenate([bias] * HEADS_PER_KV, axis=1)
            att_ref[sl, :] = a
            return jnp.maximum(m8, _col_reduce(a, jnp.max))
        m8 = lax.fori_loop(0, nkt, p1, jnp.full((8, gw), -jnp.inf, F32))
        m = jnp.max(m8, axis=0, keepdims=True)

        acc_ref[...] = jnp.zeros_like(acc_ref)

        def p2(j, l8):
            sl = pl.ds(pl.multiple_of(j * KEY_TILE, KEY_TILE), KEY_TILE)
            p = jnp.exp(att_ref[sl, :] - m)
            vt = vT_ref[g * HEAD_DIM:(g + 1) * HEAD_DIM, sl]
            acc_ref[...] += jnp.dot(vt, p.astype(BF16), preferred_element_type=F32)
            return l8 + _col_reduce(p, jnp.sum)
        l8 = lax.fori_loop(0, nkt, p2, jnp.zeros((8, gw), F32))
        l = jnp.sum(l8, axis=0, keepdims=True)
        oT = acc_ref[...] / l
        for r in range(HEADS_PER_KV):
            hd = g * HEADS_PER_KV + r
            y_ref[:, hd * HEAD_DIM:(hd + 1) * HEAD_DIM] = oT[:, r * Q_TILE:(r + 1) * Q_TILE].T.astype(BF16)


def _attn_call(qT, qiT, wiT, k, ki, vT):
    nq = SEQ // Q_TILE
    return pl.pallas_call(
        _attn_kernel,
        grid=(BATCH, nq),
        in_specs=[
            pl.BlockSpec((1, HEAD_DIM, N_HEADS * Q_TILE), lambda b, i: (b * nq + i, 0, 0)),
            pl.BlockSpec((1, IDX_DIM, IDX_HEADS * Q_TILE), lambda b, i: (b * nq + i, 0, 0)),
            pl.BlockSpec((IDX_HEADS, Q_TILE), lambda b, i: (0, b * nq + i)),
            pl.BlockSpec((SEQ, N_KV_HEADS * HEAD_DIM), lambda b, i: (b, 0)),
            pl.BlockSpec((SEQ, 128), lambda b, i: (b, 0)),
            pl.BlockSpec((N_KV_HEADS * HEAD_DIM, SEQ), lambda b, i: (0, b)),
        ],
        out_specs=pl.BlockSpec((Q_TILE, N_HEADS * HEAD_DIM), lambda b, i: (b * nq + i, 0)),
        out_shape=jax.ShapeDtypeStruct((TOKENS, N_HEADS * HEAD_DIM), BF16),
        scratch_shapes=[pltpu.VMEM((SEQ, Q_TILE), F32),
                        pltpu.VMEM((SEQ, HEADS_PER_KV * Q_TILE), F32),
                        pltpu.VMEM((HEAD_DIM, HEADS_PER_KV * Q_TILE), F32)],
        compiler_params=pltpu.CompilerParams(dimension_semantics=("arbitrary", "arbitrary"),
                                             vmem_limit_bytes=VMEM_LIMIT),
        name="dsa_attention",
    )(qT, qiT, wiT, k, ki, vT)


def _post_kernel(x_ref, mod_ref, u_ref, vln_ref, sga_ref, sgb_ref, yb_ref, ws_ref, bsT_ref,
                 wpa_ref, wpb_ref, wout_ref, n2g_ref, wfi_ref, wfo_ref, fg_ref, o_ref, ya_ref):
    tm = x_ref.shape[0]
    mod = mod_ref[0]
    gate1 = mod[:, 2 * D_MODEL:3 * D_MODEL]
    shift2 = mod[:, 3 * D_MODEL:4 * D_MODEL]
    scale2 = mod[:, 4 * D_MODEL:5 * D_MODEL]
    gate2 = mod[:, 5 * D_MODEL:6 * D_MODEL]

    r_i = lax.broadcasted_iota(jnp.int32, (CHUNK, CHUNK), 0)
    c_i = lax.broadcasted_iota(jnp.int32, (CHUNK, CHUNK), 1)
    bsT = bsT_ref[...]
    gd = A_WIDTH // A_GROUPS
    for g in range(A_GROUPS):
        wm = jnp.where(r_i >= c_i, ws_ref[g], 0.0).astype(BF16)
        bias = bsT[:, g:g + 1]
        for c in range(tm // CHUNK):
            rows = slice(c * CHUNK, (c + 1) * CHUNK)
            cols = slice(g * gd, (g + 1) * gd)
            mixed = jnp.dot(wm, vln_ref[rows, cols], preferred_element_type=F32) + bias
            ya_ref[rows, cols] = (u_ref[rows, cols].astype(F32) * mixed).astype(BF16)

    pa = jnp.dot(ya_ref[...], wpa_ref[...], preferred_element_type=F32)
    pb = jnp.dot(yb_ref[...], wpb_ref[...], preferred_element_type=F32)
    merged = sga_ref[...].astype(F32) * pa + sgb_ref[...].astype(F32) * pb
    x1 = x_ref[...] + gate1 * jnp.dot(merged.astype(BF16), wout_ref[...], preferred_element_type=F32)

    h2 = (_rms_norm(x1, n2g_ref[...]) * (1.0 + scale2) + shift2).astype(BF16)
    fg = jnp.dot(h2, wfi_ref[:, 0:D_FF], preferred_element_type=F32)
    fu = jnp.dot(h2, wfi_ref[:, D_FF:], preferred_element_type=F32)
    act = (fg * _sigmoid(fg) * fu).astype(BF16)
    x2 = x1 + gate2 * jnp.dot(act, wfo_ref[...], preferred_element_type=F32)
    o_ref[...] = _rms_norm(x2, fg_ref[...])


def _post_call(x2, mod3, u, vln, sga, sgb, yb, ws, bsT, wpa, wpb, wout, n2g, wfi, wfo, fg):
    tm = TM_OUT
    per_b = SEQ // tm
    row = lambda t: (t, 0)
    const2 = lambda t: (0, 0)
    tok = lambda: pl.BlockSpec((tm, D_MODEL), row)
    in_specs = [
        tok(),
        pl.BlockSpec((1, 1, 6 * D_MODEL), lambda t: (t // per_b, 0, 0)),
        tok(), tok(), tok(), tok(), tok(),
        pl.BlockSpec(ws.shape, lambda t: (0, 0, 0)),
        pl.BlockSpec(bsT.shape, const2),
        pl.BlockSpec(wpa.shape, const2),
        pl.BlockSpec(wpb.shape, const2),
        pl.BlockSpec(wout.shape, const2),
        pl.BlockSpec((1, D_MODEL), const2),
        pl.BlockSpec(wfi.shape, const2),
        pl.BlockSpec(wfo.shape, const2),
        pl.BlockSpec((1, D_MODEL), const2),
    ]
    return pl.pallas_call(
        _post_kernel,
        grid=(TOKENS // tm,),
        in_specs=in_specs,
        out_specs=pl.BlockSpec((tm, D_MODEL), row),
        out_shape=jax.ShapeDtypeStruct((TOKENS, D_MODEL), F32),
        scratch_shapes=[pltpu.VMEM((tm, A_WIDTH), BF16)],
        compiler_params=pltpu.CompilerParams(dimension_semantics=("arbitrary",),
                                             vmem_limit_bytes=VMEM_LIMIT),
        name="merge_ffn",
    )(x2, mod3, u, vln, sga, sgb, yb, ws, bsT, wpa, wpb, wout, n2g, wfi, wfo, fg)


def _inv_freq_table():
    fq = ROPE_THETA ** (-np.arange(0, ROT_DIM, 2, dtype=np.float32) / ROT_DIM)
    fi = ROPE_THETA ** (-np.arange(0, IDX_ROT_DIM, 2, dtype=np.float32) / IDX_ROT_DIM)
    return np.concatenate([fq, fi]).astype(np.float32).reshape(-1, 1)


def kernel(x, c, positions, w_ada, b_ada, norm1_g, w_in, gmlp_ln_g, gmlp_ln_b, gmlp_w_s, gmlp_b_s,
           idx_k_ln_g, idx_k_ln_b, w_proj_a, w_proj_b, w_out, norm2_g, w_ffn_in, w_ffn_out, final_norm_g):
    assert x.shape == (BATCH, SEQ, D_MODEL) and w_in.shape == (1, D_MODEL, _IN_COLS)
    x2 = x.reshape(TOKENS, D_MODEL)
    pos3 = positions.reshape(BATCH, 1, SEQ)
    invf = jnp.asarray(_inv_freq_table())
    xcur = x2
    for l in range(w_ada.shape[0]):
        w = w_in[l]
        wn = jnp.concatenate([w[:, _OFF_U:_OFF_Q], w[:, _OFF_GA:]], axis=1).astype(BF16)
        wt = w[:, _OFF_Q:_OFF_GA].T.astype(BF16)
        mod = _ada_call(c, w_ada[l], b_ada[l])
        mod3 = mod.reshape(BATCH, 1, 6 * D_MODEL)
        u, vln, sga, sgb, qT, k, vT, qiT, ki, wiT = _inproj_call(
            xcur, mod3, pos3, norm1_g[l].reshape(1, -1), wn, wt,
            gmlp_ln_g[l].reshape(1, -1), gmlp_ln_b[l].reshape(1, -1),
            idx_k_ln_g[l].reshape(-1, 1), idx_k_ln_b[l].reshape(-1, 1), invf)
        yb = _attn_call(qT, qiT, wiT, k, ki, vT)
        last = l == w_ada.shape[0] - 1
        fg = final_norm_g.reshape(1, -1) if last else None
        assert last, "single-layer block"
        xcur = _post_call(xcur, mod3, u, vln, sga, sgb, yb, gmlp_w_s[l], gmlp_b_s[l].T,
                          w_proj_a[l].astype(BF16), w_proj_b[l].astype(BF16), w_out[l].astype(BF16),
                          norm2_g[l].reshape(1, -1), w_ffn_in[l].astype(BF16), w_ffn_out[l].astype(BF16), fg)
    return xcur.reshape(BATCH, SEQ, D_MODEL)
```

```python
import functools

import numpy as np
import jax
import jax.numpy as jnp
from jax import lax
from jax.experimental import pallas as pl
from jax.experimental.pallas import tpu as pltpu

D_MODEL = 1024
BATCH = 4
SEQ = 4096
CHUNK = 128
A_GROUPS = 8
A_WIDTH = 1024
N_HEADS = 8
N_KV_HEADS = 2
HEAD_DIM = 128
IDX_HEADS = 8
IDX_DIM = 64
TOPK = 256
ROPE_THETA = 500000.0
ROT_DIM = HEAD_DIM // 4
IDX_ROT_DIM = IDX_DIM // 4
D_FF = 2816
EPS = 1e-6
NEG_INF = -1e30

TOKENS = BATCH * SEQ
Q_TILE = 128
KEY_TILE = 512
ACC_ROWS = 32
TM_IN = 256
TM_OUT = 256
HEADS_PER_KV = N_HEADS // N_KV_HEADS
VMEM_LIMIT = 56 * 1024 * 1024

_OFF_U = 0
_OFF_V = _OFF_U + A_WIDTH
_OFF_Q = _OFF_V + A_WIDTH
_OFF_K = _OFF_Q + N_HEADS * HEAD_DIM
_OFF_VV = _OFF_K + N_KV_HEADS * HEAD_DIM
_OFF_QI = _OFF_VV + N_KV_HEADS * HEAD_DIM
_OFF_KI = _OFF_QI + IDX_HEADS * IDX_DIM
_OFF_WI = _OFF_KI + IDX_DIM
_OFF_GA = _OFF_WI + IDX_HEADS
_OFF_GB = _OFF_GA + D_MODEL
_IN_COLS = _OFF_GB + D_MODEL

_T_Q = 0
_T_K = _T_Q + N_HEADS * HEAD_DIM
_T_V = _T_K + N_KV_HEADS * HEAD_DIM
_T_QI = _T_V + N_KV_HEADS * HEAD_DIM
_T_KI = _T_QI + IDX_HEADS * IDX_DIM
_T_WI = _T_KI + IDX_DIM
_T_ROWS = _T_WI + IDX_HEADS

F32 = jnp.float32
BF16 = jnp.bfloat16


def _gelu_tanh(x):
    return 0.5 * x * (1.0 + jnp.tanh(np.sqrt(2.0 / np.pi).astype(np.float32) * (x + 0.044715 * (x * x * x))))


def _sigmoid(x):
    return 1.0 / (1.0 + jnp.exp(-x))


def _rms_norm(x, g):
    return x * lax.rsqrt(jnp.mean(x * x, axis=-1, keepdims=True) + EPS) * g


def _ada_kernel(c_ref, w_ref, b_ref, o_ref):
    c = c_ref[...]
    a = c * _sigmoid(c)
    o_ref[...] = jnp.dot(a, w_ref[...], preferred_element_type=F32,
                         precision=lax.Precision.HIGHEST) + b_ref[...]


def _ada_call(c, w_ada, b_ada):
    n_out = 6 * D_MODEL
    tn = 1024
    return pl.pallas_call(
        _ada_kernel,
        grid=(n_out // tn,),
        in_specs=[pl.BlockSpec((BATCH, D_MODEL), lambda j: (0, 0)),
                  pl.BlockSpec((D_MODEL, tn), lambda j: (0, j)),
                  pl.BlockSpec((1, tn), lambda j: (0, j))],
        out_specs=pl.BlockSpec((BATCH, tn), lambda j: (0, j)),
        out_shape=jax.ShapeDtypeStruct((BATCH, n_out), F32),
        compiler_params=pltpu.CompilerParams(dimension_semantics=("arbitrary",),
                                             vmem_limit_bytes=VMEM_LIMIT),
        name="ada_mod",
    )(c, w_ada, b_ada.reshape(1, n_out))


def _rope_rows(blk, cos, sin, half):
    x1 = blk[0:half]
    x2 = blk[half:2 * half]
    return x1 * cos - x2 * sin, x2 * cos + x1 * sin


def _inproj_kernel(x_ref, mod_ref, pos_ref, n1g_ref, wn_ref, wt_ref, lng_ref, lnb_ref,
                   kig_ref, kib_ref, invf_ref,
                   u_ref, vln_ref, sga_ref, sgb_ref, qT_ref, k_ref, vT_ref, qiT_ref, ki_ref, wiT_ref):
    tm = x_ref.shape[0]
    x = x_ref[...]
    mod = mod_ref[0]
    shift1 = mod[:, 0:D_MODEL]
    scale1 = mod[:, D_MODEL:2 * D_MODEL]
    h = _rms_norm(x, n1g_ref[...]) * (1.0 + scale1) + shift1
    hb = h.astype(BF16)

    zu = jnp.dot(hb, wn_ref[:, 0:A_WIDTH], preferred_element_type=F32)
    u_ref[...] = _gelu_tanh(zu).astype(BF16)
    zv = _gelu_tanh(jnp.dot(hb, wn_ref[:, A_WIDTH:2 * A_WIDTH], preferred_element_type=F32))
    mu = jnp.mean(zv, axis=-1, keepdims=True)
    zc = zv - mu
    var = jnp.mean(zc * zc, axis=-1, keepdims=True)
    vln_ref[...] = (zc * lax.rsqrt(var + EPS) * lng_ref[...] + lnb_ref[...]).astype(BF16)
    zga = jnp.dot(hb, wn_ref[:, 2 * A_WIDTH:2 * A_WIDTH + D_MODEL], preferred_element_type=F32)
    sga_ref[...] = _sigmoid(zga).astype(BF16)
    zgb = jnp.dot(hb, wn_ref[:, 2 * A_WIDTH + D_MODEL:], preferred_element_type=F32)
    sgb_ref[...] = _sigmoid(zgb).astype(BF16)

    nt = (((1,), (1,)), ((), ()))
    pos = pos_ref[0].astype(F32)
    ang = invf_ref[...] * pos
    cos = jnp.cos(ang)
    sin = jnp.sin(ang)
    hq = ROT_DIM // 2
    hi = IDX_ROT_DIM // 2
    cos_q, sin_q = cos[0:hq], sin[0:hq]
    cos_i, sin_i = cos[hq:hq + hi], sin[hq:hq + hi]
    n_sub = tm // Q_TILE

    zq = lax.dot_general(wt_ref[_T_Q:_T_K, :], hb, nt, preferred_element_type=F32)
    q_scale = HEAD_DIM ** -0.5
    for hd in range(N_HEADS):
        blk = zq[hd * HEAD_DIM:(hd + 1) * HEAD_DIM]
        r1, r2 = _rope_rows(blk, cos_q, sin_q, hq)
        full = (jnp.concatenate([r1, r2, blk[ROT_DIM:]], axis=0) * q_scale).astype(BF16)
        for s in range(n_sub):
            qT_ref[s, :, hd * Q_TILE:(hd + 1) * Q_TILE] = full[:, s * Q_TILE:(s + 1) * Q_TILE]

    zk = lax.dot_general(wt_ref[_T_K:_T_V, :], hb, nt, preferred_element_type=F32)
    k_rows = []
    for g in range(N_KV_HEADS):
        blk = zk[g * HEAD_DIM:(g + 1) * HEAD_DIM]
        r1, r2 = _rope_rows(blk, cos_q, sin_q, hq)
        k_rows += [r1, r2, blk[ROT_DIM:]]
    k_ref[...] = jnp.concatenate(k_rows, axis=0).T.astype(BF16)

    zvv = lax.dot_general(wt_ref[_T_V:_T_QI, :], hb, nt, preferred_element_type=F32)
    vT_ref[...] = zvv.astype(BF16)

    zqi = lax.dot_general(wt_ref[_T_QI:_T_KI, :], hb, nt, preferred_element_type=F32)
    for hd in range(IDX_HEADS):
        blk = zqi[hd * IDX_DIM:(hd + 1) * IDX_DIM]
        r1, r2 = _rope_rows(blk, cos_i, sin_i, hi)
        full = jnp.concatenate([r1, r2, blk[IDX_ROT_DIM:]], axis=0).astype(BF16)
        for s in range(n_sub):
            qiT_ref[s, :, hd * Q_TILE:(hd + 1) * Q_TILE] = full[:, s * Q_TILE:(s + 1) * Q_TILE]

    zrest = lax.dot_general(wt_ref[_T_KI:_T_ROWS, :], hb, nt, preferred_element_type=F32)
    zki = zrest[0:IDX_DIM]
    kmu = jnp.mean(zki, axis=0, keepdims=True)
    kc = zki - kmu
    kvar = jnp.mean(kc * kc, axis=0, keepdims=True)
    kin = kc * lax.rsqrt(kvar + EPS) * kig_ref[...] + kib_ref[...]
    r1, r2 = _rope_rows(kin, cos_i, sin_i, hi)
    ki_full = jnp.concatenate([r1, r2, kin[IDX_ROT_DIM:], jnp.zeros((128 - IDX_DIM, tm), F32)], axis=0)
    ki_ref[...] = ki_full.T.astype(BF16)
    wiT_ref[...] = zrest[IDX_DIM:IDX_DIM + IDX_HEADS] * ((IDX_HEADS ** -0.5) * (IDX_DIM ** -0.5))


def _inproj_call(x2, mod3, pos3, n1g, wn, wt, lng, lnb, kig, kib, invf):
    tm = TM_IN
    n_tiles = TOKENS // tm
    per_b = SEQ // tm
    n_sub = tm // Q_TILE
    const2 = lambda t: (0, 0)
    row = lambda t: (t, 0)
    in_specs = [
        pl.BlockSpec((tm, D_MODEL), row),
        pl.BlockSpec((1, 1, 6 * D_MODEL), lambda t: (t // per_b, 0, 0)),
        pl.BlockSpec((1, 1, tm), lambda t: (t // per_b, 0, t % per_b)),
        pl.BlockSpec((1, D_MODEL), const2),
        pl.BlockSpec(wn.shape, const2),
        pl.BlockSpec(wt.shape, const2),
        pl.BlockSpec((1, A_WIDTH), const2),
        pl.BlockSpec((1, A_WIDTH), const2),
        pl.BlockSpec((IDX_DIM, 1), const2),
        pl.BlockSpec((IDX_DIM, 1), const2),
        pl.BlockSpec(invf.shape, const2),
    ]
    out_shape = [
        jax.ShapeDtypeStruct((TOKENS, A_WIDTH), BF16),
        jax.ShapeDtypeStruct((TOKENS, A_WIDTH), BF16),
        jax.ShapeDtypeStruct((TOKENS, D_MODEL), BF16),
        jax.ShapeDtypeStruct((TOKENS, D_MODEL), BF16),
        jax.ShapeDtypeStruct((TOKENS // Q_TILE, HEAD_DIM, N_HEADS * Q_TILE), BF16),
        jax.ShapeDtypeStruct((TOKENS, N_KV_HEADS * HEAD_DIM), BF16),
        jax.ShapeDtypeStruct((N_KV_HEADS * HEAD_DIM, TOKENS), BF16),
        jax.ShapeDtypeStruct((TOKENS // Q_TILE, IDX_DIM, IDX_HEADS * Q_TILE), BF16),
        jax.ShapeDtypeStruct((TOKENS, 128), BF16),
        jax.ShapeDtypeStruct((IDX_HEADS, TOKENS), F32),
    ]
    out_specs = [
        pl.BlockSpec((tm, A_WIDTH), row),
        pl.BlockSpec((tm, A_WIDTH), row),
        pl.BlockSpec((tm, D_MODEL), row),
        pl.BlockSpec((tm, D_MODEL), row),
        pl.BlockSpec((n_sub, HEAD_DIM, N_HEADS * Q_TILE), lambda t: (t, 0, 0)),
        pl.BlockSpec((tm, N_KV_HEADS * HEAD_DIM), row),
        pl.BlockSpec((N_KV_HEADS * HEAD_DIM, tm), lambda t: (0, t)),
        pl.BlockSpec((n_sub, IDX_DIM, IDX_HEADS * Q_TILE), lambda t: (t, 0, 0)),
        pl.BlockSpec((tm, 128), row),
        pl.BlockSpec((IDX_HEADS, tm), lambda t: (0, t)),
    ]
    return pl.pallas_call(
        _inproj_kernel,
        grid=(n_tiles,),
        in_specs=in_specs,
        out_specs=out_specs,
        out_shape=out_shape,
        compiler_params=pltpu.CompilerParams(dimension_semantics=("arbitrary",),
                                             vmem_limit_bytes=VMEM_LIMIT),
        name="in_proj",
    )(x2, mod3, pos3, n1g, wn, wt, lng, lnb, kig, kib, invf)


def _col_reduce(x, op):
    return op(x.reshape(x.shape[0] // ACC_ROWS, ACC_ROWS, x.shape[1]), axis=0)


def _attn_kernel(qT_ref, qiT_ref, wiT_ref, k_ref, ki_ref, vT_ref, y_ref, sc_ref, att_ref, acc_ref):
    i = pl.program_id(1)
    nkt = (i * Q_TILE + Q_TILE + KEY_TILE - 1) // KEY_TILE
    kf = float(TOPK)

    qiT = qiT_ref[0]
    w = wiT_ref[...]
    wrow = jnp.concatenate([w[hd:hd + 1, :] for hd in range(IDX_HEADS)], axis=1)

    def idx_body(j, carry):
        kt = ki_ref[pl.ds(pl.multiple_of(j * KEY_TILE, KEY_TILE), KEY_TILE), 0:IDX_DIM]
        lg = jnp.dot(kt, qiT, preferred_element_type=F32)
        r = jnp.maximum(lg, 0.0) * wrow
        s = r[:, 0:Q_TILE]
        for hd in range(1, IDX_HEADS):
            s = s + r[:, hd * Q_TILE:(hd + 1) * Q_TILE]
        sc_ref[pl.ds(pl.multiple_of(j * KEY_TILE, KEY_TILE), KEY_TILE), :] = s
        return carry

    lax.fori_loop(0, nkt, idx_body, 0)

    last = pl.multiple_of((nkt - 1) * KEY_TILE, KEY_TILE)
    s_last = sc_ref[pl.ds(last, KEY_TILE), :]
    kidx = last + lax.broadcasted_iota(jnp.int32, (KEY_TILE, Q_TILE), 0)
    qidx = i * Q_TILE + lax.broadcasted_iota(jnp.int32, (KEY_TILE, Q_TILE), 1)
    sc_ref[pl.ds(last, KEY_TILE), :] = jnp.where(kidx <= qidx, s_last, -jnp.inf)

    def tiles(fn, init):
        def body(j, carry):
            s = sc_ref[pl.ds(pl.multiple_of(j * KEY_TILE, KEY_TILE), KEY_TILE), :]
            return fn(s, carry)
        return lax.fori_loop(0, nkt, body, init)

    def count_ge(t):
        c8 = tiles(lambda s, c: c + _col_reduce(jnp.where(s >= t, 1.0, 0.0), jnp.sum),
                   jnp.zeros((ACC_ROWS, Q_TILE), F32))
        return jnp.sum(c8, axis=0, keepdims=True)

    def bisect(lo, hi, n):
        def step(_, c):
            lo, hi = c
            mid = 0.5 * lo + 0.5 * hi
            ok = count_ge(mid) >= kf
            return jnp.where(ok, mid, lo), jnp.where(ok, hi, mid)
        return lax.fori_loop(0, n, step, (lo, hi))

    @pl.when(i < TOPK // Q_TILE)
    def _():
        def body(j, carry):
            sl = pl.ds(pl.multiple_of(j * KEY_TILE, KEY_TILE), KEY_TILE)
            sc_ref[sl, :] = jnp.where(sc_ref[sl, :] > -jnp.inf, 0.0, NEG_INF)
            return carry
        lax.fori_loop(0, nkt, body, 0)

    @pl.when(i >= TOPK // Q_TILE)
    def _():
        def mm(s, c):
            mx8, mn8 = c
            mx8 = jnp.maximum(mx8, _col_reduce(s, jnp.max))
            mn8 = jnp.minimum(mn8, _col_reduce(jnp.where(s > -jnp.inf, s, jnp.inf), jnp.min))
            return mx8, mn8
        mx8, mn8 = tiles(mm, (jnp.full((ACC_ROWS, Q_TILE), -jnp.inf, F32), jnp.full((ACC_ROWS, Q_TILE), jnp.inf, F32)))
        mx = jnp.max(mx8, axis=0, keepdims=True)
        mn = jnp.min(mn8, axis=0, keepdims=True)
        lo = jnp.where(count_ge(mx) >= kf, mx, mn)
        hi = mx
        lo, hi = bisect(lo, hi, 16)

        def check(lo, hi):
            def f(s, c):
                a8, b8 = c
                a8 = jnp.maximum(a8, _col_reduce(jnp.where(s < hi, s, -jnp.inf), jnp.max))
                b8 = jnp.minimum(b8, _col_reduce(jnp.where(s >= lo, s, jnp.inf), jnp.min))
                return a8, b8
            a8, b8 = tiles(f, (jnp.full((ACC_ROWS, Q_TILE), -jnp.inf, F32), jnp.full((ACC_ROWS, Q_TILE), jnp.inf, F32)))
            below_hi = jnp.max(a8, axis=0, keepdims=True)
            above_lo = jnp.min(b8, axis=0, keepdims=True)
            conv = (lo == hi) | (below_hi == above_lo)
            v = jnp.where(lo == hi, lo, above_lo)
            return v, jnp.min(jnp.where(conv, 1.0, 0.0))

        def w_cond(c):
            return c[3] < 0.5

        def w_body(c):
            lo, hi, _, _ = c
            lo, hi = bisect(lo, hi, 4)
            v, done = check(lo, hi)
            return lo, hi, v, done

        _, _, vk, _ = lax.while_loop(w_cond, w_body, (lo, hi, lo, jnp.float32(0.0)))

        def cnt2(s, c):
            g8, e8 = c
            g8 = g8 + _col_reduce(jnp.where(s > vk, 1.0, 0.0), jnp.sum)
            e8 = e8 + _col_reduce(jnp.where(s == vk, 1.0, 0.0), jnp.sum)
            return g8, e8
        g8, e8 = tiles(cnt2, (jnp.zeros((ACC_ROWS, Q_TILE), F32), jnp.zeros((ACC_ROWS, Q_TILE), F32)))
        n_gt = jnp.sum(g8, axis=0, keepdims=True)
        n_eq = jnp.sum(e8, axis=0, keepdims=True)
        need = kf - n_gt
        has_tie = jnp.max(n_eq - need) > 0.5

        @pl.when(jnp.logical_not(has_tie))
        def _():
            def body(j, carry):
                sl = pl.ds(pl.multiple_of(j * KEY_TILE, KEY_TILE), KEY_TILE)
                sc_ref[sl, :] = jnp.where(sc_ref[sl, :] >= vk, 0.0, NEG_INF)
                return carry
            lax.fori_loop(0, nkt, body, 0)

        @pl.when(has_tie)
        def _():
            r_i = lax.broadcasted_iota(jnp.int32, (KEY_TILE, KEY_TILE), 0)
            c_i = lax.broadcasted_iota(jnp.int32, (KEY_TILE, KEY_TILE), 1)
            tri = jnp.where(r_i >= c_i, 1.0, 0.0).astype(BF16)

            def body(j, seen):
                sl = pl.ds(pl.multiple_of(j * KEY_TILE, KEY_TILE), KEY_TILE)
                s = sc_ref[sl, :]
                eq = s == vk
                rank = jnp.dot(tri, jnp.where(eq, 1.0, 0.0).astype(BF16), preferred_element_type=F32) + seen
                sel = (s > vk) | (eq & (rank <= need))
                sc_ref[sl, :] = jnp.where(sel, 0.0, NEG_INF)
                return rank[KEY_TILE - 1:KEY_TILE, :]
            lax.fori_loop(0, nkt, body, jnp.zeros((1, Q_TILE), F32))

    gw = HEADS_PER_KV * Q_TILE
    for g in range(N_KV_HEADS):
        qT = qT_ref[0, :, g * gw:(g + 1) * gw]

        def p1(j, m8):
            sl = pl.ds(pl.multiple_of(j * KEY_TILE, KEY_TILE), KEY_TILE)
            kt = k_ref[sl, g * HEAD_DIM:(g + 1) * HEAD_DIM]
            a = jnp.dot(kt, qT, preferred_element_type=F32)
            bias = sc_ref[sl, :]
            parts = []
            for r in range(HEADS_PER_KV):
                ar = a[:, r * Q_TILE:(r + 1) * Q_TILE] + bias
                att_ref[sl, r * Q_TILE:(r + 1) * Q_TILE] = ar
                parts.append(_col_reduce(ar, jnp.max))
            return jnp.maximum(m8, jnp.concatenate(parts, axis=1))
        m8 = lax.fori_loop(0, nkt, p1, jnp.full((ACC_ROWS, gw), -jnp.inf, F32))
        m = jnp.max(m8, axis=0, keepdims=True)

        acc_ref[...] = jnp.zeros_like(acc_ref)

        def p2(j, l8):
            sl = pl.ds(pl.multiple_of(j * KEY_TILE, KEY_TILE), KEY_TILE)
            p = jnp.exp(att_ref[sl, :] - m)
            vt = vT_ref[g * HEAD_DIM:(g + 1) * HEAD_DIM, sl]
            acc_ref[...] += jnp.dot(vt, p.astype(BF16), preferred_element_type=F32)
            return l8 + _col_reduce(p, jnp.sum)
        l8 = lax.fori_loop(0, nkt, p2, jnp.zeros((ACC_ROWS, gw), F32))
        l = jnp.sum(l8, axis=0, keepdims=True)
        oT = acc_ref[...] / l
        for r in range(HEADS_PER_KV):
            hd = g * HEADS_PER_KV + r
            y_ref[:, hd * HEAD_DIM:(hd + 1) * HEAD_DIM] = oT[:, r * Q_TILE:(r + 1) * Q_TILE].T.astype(BF16)


def _attn_call(qT, qiT, wiT, k, ki, vT):
    nq = SEQ // Q_TILE
    return pl.pallas_call(
        _attn_kernel,
        grid=(BATCH, nq),
        in_specs=[
            pl.BlockSpec((1, HEAD_DIM, N_HEADS * Q_TILE), lambda b, i: (b * nq + i, 0, 0)),
            pl.BlockSpec((1, IDX_DIM, IDX_HEADS * Q_TILE), lambda b, i: (b * nq + i, 0, 0)),
            pl.BlockSpec((IDX_HEADS, Q_TILE), lambda b, i: (0, b * nq + i)),
            pl.BlockSpec((SEQ, N_KV_HEADS * HEAD_DIM), lambda b, i: (b, 0)),
            pl.BlockSpec((SEQ, 128), lambda b, i: (b, 0)),
            pl.BlockSpec((N_KV_HEADS * HEAD_DIM, SEQ), lambda b, i: (0, b)),
        ],
        out_specs=pl.BlockSpec((Q_TILE, N_HEADS * HEAD_DIM), lambda b, i: (b * nq + i, 0)),
        out_shape=jax.ShapeDtypeStruct((TOKENS, N_HEADS * HEAD_DIM), BF16),
        scratch_shapes=[pltpu.VMEM((SEQ, Q_TILE), F32),
                        pltpu.VMEM((SEQ, HEADS_PER_KV * Q_TILE), F32),
                        pltpu.VMEM((HEAD_DIM, HEADS_PER_KV * Q_TILE), F32)],
        compiler_params=pltpu.CompilerParams(dimension_semantics=("arbitrary", "arbitrary"),
                                             vmem_limit_bytes=VMEM_LIMIT),
        name="dsa_attention",
    )(qT, qiT, wiT, k, ki, vT)


def _post_kernel(x_ref, mod_ref, u_ref, vln_ref, sga_ref, sgb_ref, yb_ref, ws_ref, bsT_ref,
                 wpa_ref, wpb_ref, wout_ref, n2g_ref, wfi_ref, wfo_ref, fg_ref, o_ref, ya_ref):
    tm = x_ref.shape[0]
    mod = mod_ref[0]
    gate1 = mod[:, 2 * D_MODEL:3 * D_MODEL]
    shift2 = mod[:, 3 * D_MODEL:4 * D_MODEL]
    scale2 = mod[:, 4 * D_MODEL:5 * D_MODEL]
    gate2 = mod[:, 5 * D_MODEL:6 * D_MODEL]

    r_i = lax.broadcasted_iota(jnp.int32, (CHUNK, CHUNK), 0)
    c_i = lax.broadcasted_iota(jnp.int32, (CHUNK, CHUNK), 1)
    bsT = bsT_ref[...]
    gd = A_WIDTH // A_GROUPS
    for g in range(A_GROUPS):
        wm = jnp.where(r_i >= c_i, ws_ref[g], 0.0).astype(BF16)
        bias = bsT[:, g:g + 1]
        for c in range(tm // CHUNK):
            rows = slice(c * CHUNK, (c + 1) * CHUNK)
            cols = slice(g * gd, (g + 1) * gd)
            mixed = jnp.dot(wm, vln_ref[rows, cols], preferred_element_type=F32) + bias
            ya_ref[rows, cols] = (u_ref[rows, cols].astype(F32) * mixed).astype(BF16)

    pa = jnp.dot(ya_ref[...], wpa_ref[...], preferred_element_type=F32)
    pb = jnp.dot(yb_ref[...], wpb_ref[...], preferred_element_type=F32)
    merged = sga_ref[...].astype(F32) * pa + sgb_ref[...].astype(F32) * pb
    x1 = x_ref[...] + gate1 * jnp.dot(merged.astype(BF16), wout_ref[...], preferred_element_type=F32)

    h2 = (_rms_norm(x1, n2g_ref[...]) * (1.0 + scale2) + shift2).astype(BF16)
    fg = jnp.dot(h2, wfi_ref[:, 0:D_FF], preferred_element_type=F32)
    fu = jnp.dot(h2, wfi_ref[:, D_FF:], preferred_element_type=F32)
    act = (fg * _sigmoid(fg) * fu).astype(BF16)
    x2 = x1 + gate2 * jnp.dot(act, wfo_ref[...], preferred_element_type=F32)
    o_ref[...] = _rms_norm(x2, fg_ref[...])


def _post_call(x2, mod3, u, vln, sga, sgb, yb, ws, bsT, wpa, wpb, wout, n2g, wfi, wfo, fg):
    tm = TM_OUT
    per_b = SEQ // tm
    row = lambda t: (t, 0)
    const2 = lambda t: (0, 0)
    tok = lambda: pl.BlockSpec((tm, D_MODEL), row)
    in_specs = [
        tok(),
        pl.BlockSpec((1, 1, 6 * D_MODEL), lambda t: (t // per_b, 0, 0)),
        tok(), tok(), tok(), tok(), tok(),
        pl.BlockSpec(ws.shape, lambda t: (0, 0, 0)),
        pl.BlockSpec(bsT.shape, const2),
        pl.BlockSpec(wpa.shape, const2),
        pl.BlockSpec(wpb.shape, const2),
        pl.BlockSpec(wout.shape, const2),
        pl.BlockSpec((1, D_MODEL), const2),
        pl.BlockSpec(wfi.shape, const2),
        pl.BlockSpec(wfo.shape, const2),
        pl.BlockSpec((1, D_MODEL), const2),
    ]
    return pl.pallas_call(
        _post_kernel,
        grid=(TOKENS // tm,),
        in_specs=in_specs,
        out_specs=pl.BlockSpec((tm, D_MODEL), row),
        out_shape=jax.ShapeDtypeStruct((TOKENS, D_MODEL), F32),
        scratch_shapes=[pltpu.VMEM((tm, A_WIDTH), BF16)],
        compiler_params=pltpu.CompilerParams(dimension_semantics=("arbitrary",),
                                             vmem_limit_bytes=VMEM_LIMIT),
        name="merge_ffn",
    )(x2, mod3, u, vln, sga, sgb, yb, ws, bsT, wpa, wpb, wout, n2g, wfi, wfo, fg)


def _inv_freq_table():
    fq = ROPE_THETA ** (-np.arange(0, ROT_DIM, 2, dtype=np.float32) / ROT_DIM)
    fi = ROPE_THETA ** (-np.arange(0, IDX_ROT_DIM, 2, dtype=np.float32) / IDX_ROT_DIM)
    return np.concatenate([fq, fi]).astype(np.float32).reshape(-1, 1)


def kernel(x, c, positions, w_ada, b_ada, norm1_g, w_in, gmlp_ln_g, gmlp_ln_b, gmlp_w_s, gmlp_b_s,
           idx_k_ln_g, idx_k_ln_b, w_proj_a, w_proj_b, w_out, norm2_g, w_ffn_in, w_ffn_out, final_norm_g):
    assert x.shape == (BATCH, SEQ, D_MODEL) and w_in.shape == (1, D_MODEL, _IN_COLS)
    x2 = x.reshape(TOKENS, D_MODEL)
    pos3 = positions.reshape(BATCH, 1, SEQ)
    invf = jnp.asarray(_inv_freq_table())
    xcur = x2
    for l in range(w_ada.shape[0]):
        w = w_in[l]
        wn = jnp.concatenate([w[:, _OFF_U:_OFF_Q], w[:, _OFF_GA:]], axis=1).astype(BF16)
        wt = w[:, _OFF_Q:_OFF_GA].T.astype(BF16)
        mod = _ada_call(c, w_ada[l], b_ada[l])
        mod3 = mod.reshape(BATCH, 1, 6 * D_MODEL)
        u, vln, sga, sgb, qT, k, vT, qiT, ki, wiT = _inproj_call(
            xcur, mod3, pos3, norm1_g[l].reshape(1, -1), wn, wt,
            gmlp_ln_g[l].reshape(1, -1), gmlp_ln_b[l].reshape(1, -1),
            idx_k_ln_g[l].reshape(-1, 1), idx_k_ln_b[l].reshape(-1, 1), invf)
        yb = _attn_call(qT, qiT, wiT, k, ki, vT)
        last = l == w_ada.shape[0] - 1
        fg = final_norm_g.reshape(1, -1) if last else None
        assert last, "single-layer block"
        xcur = _post_call(xcur, mod3, u, vln, sga, sgb, yb, gmlp_w_s[l], gmlp_b_s[l].T,
                          w_proj_a[l].astype(BF16), w_proj_b[l].astype(BF16), w_out[l].astype(BF16),
                          norm2_g[l].reshape(1, -1), w_ffn_in[l].astype(BF16), w_ffn_out[l].astype(BF16), fg)
    return xcur.reshape(BATCH, SEQ, D_MODEL)
```

```python
import functools

import numpy as np
import jax
import jax.numpy as jnp
from jax import lax
from jax.experimental import pallas as pl
from jax.experimental.pallas import tpu as pltpu

D_MODEL = 1024
BATCH = 4
SEQ = 4096
CHUNK = 128
A_GROUPS = 8
A_WIDTH = 1024
N_HEADS = 8
N_KV_HEADS = 2
HEAD_DIM = 128
IDX_HEADS = 8
IDX_DIM = 64
TOPK = 256
ROPE_THETA = 500000.0
ROT_DIM = HEAD_DIM // 4
IDX_ROT_DIM = IDX_DIM // 4
D_FF = 2816
EPS = 1e-6
NEG_INF = -1e30

TOKENS = BATCH * SEQ
Q_TILE = 128
KEY_TILE = 512
ACC_ROWS = 32
TM_IN = 256
TM_OUT = 256
HEADS_PER_KV = N_HEADS // N_KV_HEADS
VMEM_LIMIT = 56 * 1024 * 1024

_OFF_U = 0
_OFF_V = _OFF_U + A_WIDTH
_OFF_Q = _OFF_V + A_WIDTH
_OFF_K = _OFF_Q + N_HEADS * HEAD_DIM
_OFF_VV = _OFF_K + N_KV_HEADS * HEAD_DIM
_OFF_QI = _OFF_VV + N_KV_HEADS * HEAD_DIM
_OFF_KI = _OFF_QI + IDX_HEADS * IDX_DIM
_OFF_WI = _OFF_KI + IDX_DIM
_OFF_GA = _OFF_WI + IDX_HEADS
_OFF_GB = _OFF_GA + D_MODEL
_IN_COLS = _OFF_GB + D_MODEL

_T_Q = 0
_T_K = _T_Q + N_HEADS * HEAD_DIM
_T_V = _T_K + N_KV_HEADS * HEAD_DIM
_T_QI = _T_V + N_KV_HEADS * HEAD_DIM
_T_KI = _T_QI + IDX_HEADS * IDX_DIM
_T_WI = _T_KI + IDX_DIM
_T_ROWS = _T_WI + IDX_HEADS

F32 = jnp.float32
BF16 = jnp.bfloat16


def _gelu_tanh(x):
    return 0.5 * x * (1.0 + jnp.tanh(np.sqrt(2.0 / np.pi).astype(np.float32) * (x + 0.044715 * (x * x * x))))


def _sigmoid(x):
    return 1.0 / (1.0 + jnp.exp(-x))


def _rms_norm(x, g):
    return x * lax.rsqrt(jnp.mean(x * x, axis=-1, keepdims=True) + EPS) * g


def _ada_kernel(c_ref, w_ref, b_ref, o_ref):
    c = c_ref[...]
    a = c * _sigmoid(c)
    o_ref[...] = jnp.dot(a, w_ref[...], preferred_element_type=F32,
                         precision=lax.Precision.HIGHEST) + b_ref[...]


def _ada_call(c, w_ada, b_ada):
    n_out = 6 * D_MODEL
    tn = 1024
    return pl.pallas_call(
        _ada_kernel,
        grid=(n_out // tn,),
        in_specs=[pl.BlockSpec((BATCH, D_MODEL), lambda j: (0, 0)),
                  pl.BlockSpec((D_MODEL, tn), lambda j: (0, j)),
                  pl.BlockSpec((1, tn), lambda j: (0, j))],
        out_specs=pl.BlockSpec((BATCH, tn), lambda j: (0, j)),
        out_shape=jax.ShapeDtypeStruct((BATCH, n_out), F32),
        compiler_params=pltpu.CompilerParams(dimension_semantics=("arbitrary",),
                                             vmem_limit_bytes=VMEM_LIMIT),
        name="ada_mod",
    )(c, w_ada, b_ada.reshape(1, n_out))


def _rope_rows(blk, cos, sin, half):
    x1 = blk[0:half]
    x2 = blk[half:2 * half]
    return x1 * cos - x2 * sin, x2 * cos + x1 * sin


def _inproj_kernel(x_ref, mod_ref, pos_ref, n1g_ref, wn_ref, wt_ref, lng_ref, lnb_ref,
                   kig_ref, kib_ref, invf_ref,
                   u_ref, vln_ref, sga_ref, sgb_ref, qT_ref, k_ref, vT_ref, qiT_ref, ki_ref, wiT_ref):
    tm = x_ref.shape[0]
    x = x_ref[...]
    mod = mod_ref[0]
    shift1 = mod[:, 0:D_MODEL]
    scale1 = mod[:, D_MODEL:2 * D_MODEL]
    h = _rms_norm(x, n1g_ref[...]) * (1.0 + scale1) + shift1
    hb = h.astype(BF16)

    zu = jnp.dot(hb, wn_ref[:, 0:A_WIDTH], preferred_element_type=F32)
    u_ref[...] = _gelu_tanh(zu).astype(BF16)
    zv = _gelu_tanh(jnp.dot(hb, wn_ref[:, A_WIDTH:2 * A_WIDTH], preferred_element_type=F32))
    mu = jnp.mean(zv, axis=-1, keepdims=True)
    zc = zv - mu
    var = jnp.mean(zc * zc, axis=-1, keepdims=True)
    vln_ref[...] = (zc * lax.rsqrt(var + EPS) * lng_ref[...] + lnb_ref[...]).astype(BF16)
    zga = jnp.dot(hb, wn_ref[:, 2 * A_WIDTH:2 * A_WIDTH + D_MODEL], preferred_element_type=F32)
    sga_ref[...] = _sigmoid(zga).astype(BF16)
    zgb = jnp.dot(hb, wn_ref[:, 2 * A_WIDTH + D_MODEL:], preferred_element_type=F32)
    sgb_ref[...] = _sigmoid(zgb).astype(BF16)

    nt = (((1,), (1,)), ((), ()))
    pos = pos_ref[0].astype(F32)
    ang = invf_ref[...] * pos
    cos = jnp.cos(ang)
    sin = jnp.sin(ang)
    hq = ROT_DIM // 2
    hi = IDX_ROT_DIM // 2
    cos_q, sin_q = cos[0:hq], sin[0:hq]
    cos_i, sin_i = cos[hq:hq + hi], sin[hq:hq + hi]
    n_sub = tm // Q_TILE

    zq = lax.dot_general(wt_ref[_T_Q:_T_K, :], hb, nt, preferred_element_type=F32)
    q_scale = HEAD_DIM ** -0.5 * float(np.log2(np.e))
    for hd in range(N_HEADS):
        blk = zq[hd * HEAD_DIM:(hd + 1) * HEAD_DIM]
        r1, r2 = _rope_rows(blk, cos_q, sin_q, hq)
        full = (jnp.concatenate([r1, r2, blk[ROT_DIM:]], axis=0) * q_scale).astype(BF16)
        for s in range(n_sub):
            qT_ref[s, :, hd * Q_TILE:(hd + 1) * Q_TILE] = full[:, s * Q_TILE:(s + 1) * Q_TILE]

    zk = lax.dot_general(wt_ref[_T_K:_T_V, :], hb, nt, preferred_element_type=F32)
    k_rows = []
    for g in range(N_KV_HEADS):
        blk = zk[g * HEAD_DIM:(g + 1) * HEAD_DIM]
        r1, r2 = _rope_rows(blk, cos_q, sin_q, hq)
        k_rows += [r1, r2, blk[ROT_DIM:]]
    k_ref[...] = jnp.concatenate(k_rows, axis=0).T.astype(BF16)

    zvv = lax.dot_general(wt_ref[_T_V:_T_QI, :], hb, nt, preferred_element_type=F32)
    vT_ref[...] = zvv.astype(BF16)

    zqi = lax.dot_general(wt_ref[_T_QI:_T_KI, :], hb, nt, preferred_element_type=F32)
    for hd in range(IDX_HEADS):
        blk = zqi[hd * IDX_DIM:(hd + 1) * IDX_DIM]
        r1, r2 = _rope_rows(blk, cos_i, sin_i, hi)
        full = jnp.concatenate([r1, r2, blk[IDX_ROT_DIM:]], axis=0).astype(BF16)
        for s in range(n_sub):
            qiT_ref[s, :, hd * Q_TILE:(hd + 1) * Q_TILE] = full[:, s * Q_TILE:(s + 1) * Q_TILE]

    zrest = lax.dot_general(wt_ref[_T_KI:_T_ROWS, :], hb, nt, preferred_element_type=F32)
    zki = zrest[0:IDX_DIM]
    kmu = jnp.mean(zki, axis=0, keepdims=True)
    kc = zki - kmu
    kvar = jnp.mean(kc * kc, axis=0, keepdims=True)
    kin = kc * lax.rsqrt(kvar + EPS) * kig_ref[...] + kib_ref[...]
    r1, r2 = _rope_rows(kin, cos_i, sin_i, hi)
    ki_full = jnp.concatenate([r1, r2, kin[IDX_ROT_DIM:], jnp.zeros((128 - IDX_DIM, tm), F32)], axis=0)
    ki_ref[...] = ki_full.T.astype(BF16)
    wiT_ref[...] = zrest[IDX_DIM:IDX_DIM + IDX_HEADS] * ((IDX_HEADS ** -0.5) * (IDX_DIM ** -0.5))


def _inproj_call(x2, mod3, pos3, n1g, wn, wt, lng, lnb, kig, kib, invf):
    tm = TM_IN
    n_tiles = TOKENS // tm
    per_b = SEQ // tm
    n_sub = tm // Q_TILE
    const2 = lambda t: (0, 0)
    row = lambda t: (t, 0)
    in_specs = [
        pl.BlockSpec((tm, D_MODEL), row),
        pl.BlockSpec((1, 1, 6 * D_MODEL), lambda t: (t // per_b, 0, 0)),
        pl.BlockSpec((1, 1, tm), lambda t: (t // per_b, 0, t % per_b)),
        pl.BlockSpec((1, D_MODEL), const2),
        pl.BlockSpec(wn.shape, const2),
        pl.BlockSpec(wt.shape, const2),
        pl.BlockSpec((1, A_WIDTH), const2),
        pl.BlockSpec((1, A_WIDTH), const2),
        pl.BlockSpec((IDX_DIM, 1), const2),
        pl.BlockSpec((IDX_DIM, 1), const2),
        pl.BlockSpec(invf.shape, const2),
    ]
    out_shape = [
        jax.ShapeDtypeStruct((TOKENS, A_WIDTH), BF16),
        jax.ShapeDtypeStruct((TOKENS, A_WIDTH), BF16),
        jax.ShapeDtypeStruct((TOKENS, D_MODEL), BF16),
        jax.ShapeDtypeStruct((TOKENS, D_MODEL), BF16),
        jax.ShapeDtypeStruct((TOKENS // Q_TILE, HEAD_DIM, N_HEADS * Q_TILE), BF16),
        jax.ShapeDtypeStruct((TOKENS, N_KV_HEADS * HEAD_DIM), BF16),
        jax.ShapeDtypeStruct((N_KV_HEADS * HEAD_DIM, TOKENS), BF16),
        jax.ShapeDtypeStruct((TOKENS // Q_TILE, IDX_DIM, IDX_HEADS * Q_TILE), BF16),
        jax.ShapeDtypeStruct((TOKENS, 128), BF16),
        jax.ShapeDtypeStruct((IDX_HEADS, TOKENS), F32),
    ]
    out_specs = [
        pl.BlockSpec((tm, A_WIDTH), row),
        pl.BlockSpec((tm, A_WIDTH), row),
        pl.BlockSpec((tm, D_MODEL), row),
        pl.BlockSpec((tm, D_MODEL), row),
        pl.BlockSpec((n_sub, HEAD_DIM, N_HEADS * Q_TILE), lambda t: (t, 0, 0)),
        pl.BlockSpec((tm, N_KV_HEADS * HEAD_DIM), row),
        pl.BlockSpec((N_KV_HEADS * HEAD_DIM, tm), lambda t: (0, t)),
        pl.BlockSpec((n_sub, IDX_DIM, IDX_HEADS * Q_TILE), lambda t: (t, 0, 0)),
        pl.BlockSpec((tm, 128), row),
        pl.BlockSpec((IDX_HEADS, tm), lambda t: (0, t)),
    ]
    return pl.pallas_call(
        _inproj_kernel,
        grid=(n_tiles,),
        in_specs=in_specs,
        out_specs=out_specs,
        out_shape=out_shape,
        compiler_params=pltpu.CompilerParams(dimension_semantics=("arbitrary",),
                                             vmem_limit_bytes=VMEM_LIMIT),
        name="in_proj",
    )(x2, mod3, pos3, n1g, wn, wt, lng, lnb, kig, kib, invf)


def _col_reduce(x, op):
    return op(x.reshape(x.shape[0] // ACC_ROWS, ACC_ROWS, x.shape[1]), axis=0)


def _attn_kernel(qT_ref, qiT_ref, wiT_ref, k_ref, ki_ref, vT_ref, y_ref,
                 sc_ref, att_ref, acc_ref, l_ref, vsel_ref):
    i = pl.program_id(1)
    nkt = (i * Q_TILE + Q_TILE + KEY_TILE - 1) // KEY_TILE
    kf = float(TOPK)
    gw = HEADS_PER_KV * Q_TILE
    hw = N_HEADS * Q_TILE

    qiT = qiT_ref[0]
    qT = qT_ref[0]
    w = wiT_ref[...]
    wrow = jnp.concatenate([w[hd:hd + 1, :] for hd in range(IDX_HEADS)], axis=1)

    row_i = lax.broadcasted_iota(jnp.int32, (KEY_TILE, Q_TILE), 0)
    qidx = i * Q_TILE + lax.broadcasted_iota(jnp.int32, (KEY_TILE, Q_TILE), 1)

    def sweep_body(j, carry):
        mx_a, mn_a = carry
        sl = pl.ds(pl.multiple_of(j * KEY_TILE, KEY_TILE), KEY_TILE)
        lg = jnp.dot(ki_ref[sl, 0:IDX_DIM], qiT, preferred_element_type=F32)
        r = jnp.maximum(lg, 0.0) * wrow
        s = r[:, 0:Q_TILE]
        for hd in range(1, IDX_HEADS):
            s = s + r[:, hd * Q_TILE:(hd + 1) * Q_TILE]
        causal = j * KEY_TILE + row_i <= qidx
        sc_ref[sl, :] = jnp.where(causal, s, -jnp.inf)
        mx_a = jnp.maximum(mx_a, _col_reduce(jnp.where(causal, s, -jnp.inf), jnp.max))
        mn_a = jnp.minimum(mn_a, _col_reduce(jnp.where(causal, s, jnp.inf), jnp.min))
        for g in range(N_KV_HEADS):
            att_ref[sl, g * gw:(g + 1) * gw] = jnp.dot(
                k_ref[sl, g * HEAD_DIM:(g + 1) * HEAD_DIM], qT[:, g * gw:(g + 1) * gw],
                preferred_element_type=F32)
        return mx_a, mn_a

    mx_a, mn_a = lax.fori_loop(0, nkt, sweep_body, (jnp.full((ACC_ROWS, Q_TILE), -jnp.inf, F32),
                                                    jnp.full((ACC_ROWS, Q_TILE), jnp.inf, F32)))

    def tiles(fn, init):
        def body(j, carry):
            s = sc_ref[pl.ds(pl.multiple_of(j * KEY_TILE, KEY_TILE), KEY_TILE), :]
            return fn(s, carry)
        return lax.fori_loop(0, nkt, body, init)

    def count_ge(t):
        c8 = tiles(lambda s, c: c + _col_reduce(jnp.where(s >= t, 1.0, 0.0), jnp.sum),
                   jnp.zeros((ACC_ROWS, Q_TILE), F32))
        return jnp.sum(c8, axis=0, keepdims=True)

    def bisect(state, n):
        def step(_, c):
            lo, hi, n_lo, n_hi = c
            mid = 0.5 * lo + 0.5 * hi
            n_mid = count_ge(mid)
            ok = n_mid >= kf
            return (jnp.where(ok, mid, lo), jnp.where(ok, hi, mid),
                    jnp.where(ok, n_mid, n_lo), jnp.where(ok, n_hi, n_mid))
        return lax.fori_loop(0, n, step, state)

    @pl.when(i < TOPK // Q_TILE)
    def _():
        vsel_ref[...] = jnp.full((1, Q_TILE), jnp.finfo(jnp.float32).min, F32)

    @pl.when(i >= TOPK // Q_TILE)
    def _():
        mx = jnp.max(mx_a, axis=0, keepdims=True)
        mn = jnp.min(mn_a, axis=0, keepdims=True)
        n_valid = (qidx[0:1, :] + 1).astype(F32)
        hi0 = mx + jnp.maximum(jnp.abs(mx), 1e-30) * 1e-6
        state = bisect((mn, hi0, n_valid, jnp.zeros((1, Q_TILE), F32)), 16)

        def check(lo, hi):
            def f(s, c):
                a8, b8 = c
                a8 = jnp.maximum(a8, _col_reduce(jnp.where(s < hi, s, -jnp.inf), jnp.max))
                b8 = jnp.minimum(b8, _col_reduce(jnp.where(s >= lo, s, jnp.inf), jnp.min))
                return a8, b8
            a8, b8 = tiles(f, (jnp.full((ACC_ROWS, Q_TILE), -jnp.inf, F32), jnp.full((ACC_ROWS, Q_TILE), jnp.inf, F32)))
            below_hi = jnp.max(a8, axis=0, keepdims=True)
            above_lo = jnp.min(b8, axis=0, keepdims=True)
            return above_lo, jnp.min(jnp.where(below_hi == above_lo, 1.0, 0.0))

        def w_cond(c):
            return c[2] < 0.5

        def w_body(c):
            st = bisect(c[0], 4)
            v, done = check(st[0], st[1])
            return st, v, done

        (_, _, n_ge, n_gt), vk, _ = lax.while_loop(w_cond, w_body, (state, mn, jnp.float32(0.0)))
        need = kf - n_gt
        has_tie = jnp.max(n_ge) > kf + 0.5

        @pl.when(jnp.logical_not(has_tie))
        def _():
            vsel_ref[...] = vk

        @pl.when(has_tie)
        def _():
            vsel_ref[...] = jnp.full((1, Q_TILE), -0.5, F32)
            r_i = lax.broadcasted_iota(jnp.int32, (KEY_TILE, KEY_TILE), 0)
            c_i = lax.broadcasted_iota(jnp.int32, (KEY_TILE, KEY_TILE), 1)
            tri = jnp.where(r_i >= c_i, 1.0, 0.0).astype(BF16)

            def body(j, seen):
                sl = pl.ds(pl.multiple_of(j * KEY_TILE, KEY_TILE), KEY_TILE)
                s = sc_ref[sl, :]
                eq = s == vk
                rank = jnp.dot(tri, jnp.where(eq, 1.0, 0.0).astype(BF16), preferred_element_type=F32) + seen
                sel = (s > vk) | (eq & (rank <= need))
                sc_ref[sl, :] = jnp.where(sel, 0.0, NEG_INF)
                return rank[KEY_TILE - 1:KEY_TILE, :]
            lax.fori_loop(0, nkt, body, jnp.zeros((1, Q_TILE), F32))

    vsel = vsel_ref[...]

    def key_mask(sl):
        return jnp.where(sc_ref[sl, :] >= vsel, 0.0, NEG_INF)

    def softmax_pv(m):
        acc_ref[...] = jnp.zeros_like(acc_ref)

        def body(j, l8):
            sl = pl.ds(pl.multiple_of(j * KEY_TILE, KEY_TILE), KEY_TILE)
            mask = key_mask(sl)
            parts = []
            for g in range(N_KV_HEADS):
                ps = []
                for r in range(HEADS_PER_KV):
                    c0 = (g * HEADS_PER_KV + r) * Q_TILE
                    e = att_ref[sl, c0:c0 + Q_TILE] + mask
                    if m is not None:
                        e = e - m[:, c0:c0 + Q_TILE]
                    ps.append(jnp.exp2(e))
                p = jnp.concatenate(ps, axis=1)
                acc_ref[:, g * gw:(g + 1) * gw] += jnp.dot(
                    vT_ref[g * HEAD_DIM:(g + 1) * HEAD_DIM, sl], p.astype(BF16), preferred_element_type=F32)
                parts.append(jnp.sum(p.reshape(KEY_TILE // 8, 8, gw), axis=0))
            return l8 + jnp.concatenate(parts, axis=1)
        l8 = lax.fori_loop(0, nkt, body, jnp.zeros((8, hw), F32))
        l_ref[...] = jnp.sum(l8, axis=0, keepdims=True)

    softmax_pv(None)
    l_fast = l_ref[...]
    in_range = (jnp.min(l_fast) > 1e-20) & (jnp.max(l_fast) < 1e30)

    @pl.when(jnp.logical_not(in_range))
    def _():
        def body(j, m8):
            sl = pl.ds(pl.multiple_of(j * KEY_TILE, KEY_TILE), KEY_TILE)
            mask = jnp.concatenate([key_mask(sl)] * N_HEADS, axis=1)
            return jnp.maximum(m8, jnp.max((att_ref[sl, :] + mask).reshape(KEY_TILE // 8, 8, hw), axis=0))
        m8 = lax.fori_loop(0, nkt, body, jnp.full((8, hw), -jnp.inf, F32))
        softmax_pv(jnp.max(m8, axis=0, keepdims=True))

    oT = acc_ref[...] / l_ref[...]
    for hd in range(N_HEADS):
        y_ref[:, hd * HEAD_DIM:(hd + 1) * HEAD_DIM] = oT[:, hd * Q_TILE:(hd + 1) * Q_TILE].T.astype(BF16)


def _attn_call(qT, qiT, wiT, k, ki, vT):
    nq = SEQ // Q_TILE
    return pl.pallas_call(
        _attn_kernel,
        grid=(BATCH, nq),
        in_specs=[
            pl.BlockSpec((1, HEAD_DIM, N_HEADS * Q_TILE), lambda b, i: (b * nq + i, 0, 0)),
            pl.BlockSpec((1, IDX_DIM, IDX_HEADS * Q_TILE), lambda b, i: (b * nq + i, 0, 0)),
            pl.BlockSpec((IDX_HEADS, Q_TILE), lambda b, i: (0, b * nq + i)),
            pl.BlockSpec((SEQ, N_KV_HEADS * HEAD_DIM), lambda b, i: (b, 0)),
            pl.BlockSpec((SEQ, 128), lambda b, i: (b, 0)),
            pl.BlockSpec((N_KV_HEADS * HEAD_DIM, SEQ), lambda b, i: (0, b)),
        ],
        out_specs=pl.BlockSpec((Q_TILE, N_HEADS * HEAD_DIM), lambda b, i: (b * nq + i, 0)),
        out_shape=jax.ShapeDtypeStruct((TOKENS, N_HEADS * HEAD_DIM), BF16),
        scratch_shapes=[pltpu.VMEM((SEQ, Q_TILE), F32),
                        pltpu.VMEM((SEQ, N_HEADS * Q_TILE), F32),
                        pltpu.VMEM((HEAD_DIM, N_HEADS * Q_TILE), F32),
                        pltpu.VMEM((1, N_HEADS * Q_TILE), F32),
                        pltpu.VMEM((1, Q_TILE), F32)],
        compiler_params=pltpu.CompilerParams(dimension_semantics=("arbitrary", "arbitrary"),
                                             vmem_limit_bytes=VMEM_LIMIT),
        name="dsa_attention",
    )(qT, qiT, wiT, k, ki, vT)


def _post_kernel(x_ref, mod_ref, u_ref, vln_ref, sga_ref, sgb_ref, yb_ref, ws_ref, bsT_ref,
                 wpa_ref, wpb_ref, wout_ref, n2g_ref, wfi_ref, wfo_ref, fg_ref, o_ref, ya_ref):
    tm = x_ref.shape[0]
    mod = mod_ref[0]
    gate1 = mod[:, 2 * D_MODEL:3 * D_MODEL]
    shift2 = mod[:, 3 * D_MODEL:4 * D_MODEL]
    scale2 = mod[:, 4 * D_MODEL:5 * D_MODEL]
    gate2 = mod[:, 5 * D_MODEL:6 * D_MODEL]

    r_i = lax.broadcasted_iota(jnp.int32, (CHUNK, CHUNK), 0)
    c_i = lax.broadcasted_iota(jnp.int32, (CHUNK, CHUNK), 1)
    bsT = bsT_ref[...]
    gd = A_WIDTH // A_GROUPS
    for g in range(A_GROUPS):
        wm = jnp.where(r_i >= c_i, ws_ref[g], 0.0).astype(BF16)
        bias = bsT[:, g:g + 1]
        for c in range(tm // CHUNK):
            rows = slice(c * CHUNK, (c + 1) * CHUNK)
            cols = slice(g * gd, (g + 1) * gd)
            mixed = jnp.dot(wm, vln_ref[rows, cols], preferred_element_type=F32) + bias
            ya_ref[rows, cols] = (u_ref[rows, cols].astype(F32) * mixed).astype(BF16)

    pa = jnp.dot(ya_ref[...], wpa_ref[...], preferred_element_type=F32)
    pb = jnp.dot(yb_ref[...], wpb_ref[...], preferred_element_type=F32)
    merged = sga_ref[...].astype(F32) * pa + sgb_ref[...].astype(F32) * pb
    x1 = x_ref[...] + gate1 * jnp.dot(merged.astype(BF16), wout_ref[...], preferred_element_type=F32)

    h2 = (_rms_norm(x1, n2g_ref[...]) * (1.0 + scale2) + shift2).astype(BF16)
    fg = jnp.dot(h2, wfi_ref[:, 0:D_FF], preferred_element_type=F32)
    fu = jnp.dot(h2, wfi_ref[:, D_FF:], preferred_element_type=F32)
    act = (fg * _sigmoid(fg) * fu).astype(BF16)
    x2 = x1 + gate2 * jnp.dot(act, wfo_ref[...], preferred_element_type=F32)
    o_ref[...] = _rms_norm(x2, fg_ref[...])


def _post_call(x2, mod3, u, vln, sga, sgb, yb, ws, bsT, wpa, wpb, wout, n2g, wfi, wfo, fg):
    tm = TM_OUT
    per_b = SEQ // tm
    row = lambda t: (t, 0)
    const2 = lambda t: (0, 0)
    tok = lambda: pl.BlockSpec((tm, D_MODEL), row)
    in_specs = [
        tok(),
        pl.BlockSpec((1, 1, 6 * D_MODEL), lambda t: (t // per_b, 0, 0)),
        tok(), tok(), tok(), tok(), tok(),
        pl.BlockSpec(ws.shape, lambda t: (0, 0, 0)),
        pl.BlockSpec(bsT.shape, const2),
        pl.BlockSpec(wpa.shape, const2),
        pl.BlockSpec(wpb.shape, const2),
        pl.BlockSpec(wout.shape, const2),
        pl.BlockSpec((1, D_MODEL), const2),
        pl.BlockSpec(wfi.shape, const2),
        pl.BlockSpec(wfo.shape, const2),
        pl.BlockSpec((1, D_MODEL), const2),
    ]
    return pl.pallas_call(
        _post_kernel,
        grid=(TOKENS // tm,),
        in_specs=in_specs,
        out_specs=pl.BlockSpec((tm, D_MODEL), row),
        out_shape=jax.ShapeDtypeStruct((TOKENS, D_MODEL), F32),
        scratch_shapes=[pltpu.VMEM((tm, A_WIDTH), BF16)],
        compiler_params=pltpu.CompilerParams(dimension_semantics=("arbitrary",),
                                             vmem_limit_bytes=VMEM_LIMIT),
        name="merge_ffn",
    )(x2, mod3, u, vln, sga, sgb, yb, ws, bsT, wpa, wpb, wout, n2g, wfi, wfo, fg)


def _inv_freq_table():
    fq = ROPE_THETA ** (-np.arange(0, ROT_DIM, 2, dtype=np.float32) / ROT_DIM)
    fi = ROPE_THETA ** (-np.arange(0, IDX_ROT_DIM, 2, dtype=np.float32) / IDX_ROT_DIM)
    return np.concatenate([fq, fi]).astype(np.float32).reshape(-1, 1)


def kernel(x, c, positions, w_ada, b_ada, norm1_g, w_in, gmlp_ln_g, gmlp_ln_b, gmlp_w_s, gmlp_b_s,
           idx_k_ln_g, idx_k_ln_b, w_proj_a, w_proj_b, w_out, norm2_g, w_ffn_in, w_ffn_out, final_norm_g):
    assert x.shape == (BATCH, SEQ, D_MODEL) and w_in.shape == (1, D_MODEL, _IN_COLS)
    x2 = x.reshape(TOKENS, D_MODEL)
    pos3 = positions.reshape(BATCH, 1, SEQ)
    invf = jnp.asarray(_inv_freq_table())
    xcur = x2
    for l in range(w_ada.shape[0]):
        w = w_in[l]
        wn = jnp.concatenate([w[:, _OFF_U:_OFF_Q], w[:, _OFF_GA:]], axis=1).astype(BF16)
        wt = w[:, _OFF_Q:_OFF_GA].T.astype(BF16)
        mod = _ada_call(c, w_ada[l], b_ada[l])
        mod3 = mod.reshape(BATCH, 1, 6 * D_MODEL)
        u, vln, sga, sgb, qT, k, vT, qiT, ki, wiT = _inproj_call(
            xcur, mod3, pos3, norm1_g[l].reshape(1, -1), wn, wt,
            gmlp_ln_g[l].reshape(1, -1), gmlp_ln_b[l].reshape(1, -1),
            idx_k_ln_g[l].reshape(-1, 1), idx_k_ln_b[l].reshape(-1, 1), invf)
        yb = _attn_call(qT, qiT, wiT, k, ki, vT)
        last = l == w_ada.shape[0] - 1
        fg = final_norm_g.reshape(1, -1) if last else None
        assert last, "single-layer block"
        xcur = _post_call(xcur, mod3, u, vln, sga, sgb, yb, gmlp_w_s[l], gmlp_b_s[l].T,
                          w_proj_a[l].astype(BF16), w_proj_b[l].astype(BF16), w_out[l].astype(BF16),
                          norm2_g[l].reshape(1, -1), w_ffn_in[l].astype(BF16), w_ffn_out[l].astype(BF16), fg)
    return xcur.reshape(BATCH, SEQ, D_MODEL)
```

```python
import functools

import numpy as np
import jax
import jax.numpy as jnp
from jax import lax
from jax.experimental import pallas as pl
from jax.experimental.pallas import tpu as pltpu

D_MODEL = 1024
BATCH = 4
SEQ = 4096
CHUNK = 128
A_GROUPS = 8
A_WIDTH = 1024
N_HEADS = 8
N_KV_HEADS = 2
HEAD_DIM = 128
IDX_HEADS = 8
IDX_DIM = 64
TOPK = 256
ROPE_THETA = 500000.0
ROT_DIM = HEAD_DIM // 4
IDX_ROT_DIM = IDX_DIM // 4
D_FF = 2816
EPS = 1e-6
NEG_INF = -1e30

TOKENS = BATCH * SEQ
Q_TILE = 128
KEY_TILE = 512
ACC_ROWS = 32
TM_IN = 256
TM_OUT = 256
HEADS_PER_KV = N_HEADS // N_KV_HEADS
VMEM_LIMIT = 56 * 1024 * 1024

_OFF_U = 0
_OFF_V = _OFF_U + A_WIDTH
_OFF_Q = _OFF_V + A_WIDTH
_OFF_K = _OFF_Q + N_HEADS * HEAD_DIM
_OFF_VV = _OFF_K + N_KV_HEADS * HEAD_DIM
_OFF_QI = _OFF_VV + N_KV_HEADS * HEAD_DIM
_OFF_KI = _OFF_QI + IDX_HEADS * IDX_DIM
_OFF_WI = _OFF_KI + IDX_DIM
_OFF_GA = _OFF_WI + IDX_HEADS
_OFF_GB = _OFF_GA + D_MODEL
_IN_COLS = _OFF_GB + D_MODEL

_T_Q = 0
_T_K = _T_Q + N_HEADS * HEAD_DIM
_T_V = _T_K + N_KV_HEADS * HEAD_DIM
_T_QI = _T_V + N_KV_HEADS * HEAD_DIM
_T_KI = _T_QI + IDX_HEADS * IDX_DIM
_T_WI = _T_KI + IDX_DIM
_T_ROWS = _T_WI + IDX_HEADS

F32 = jnp.float32
BF16 = jnp.bfloat16


def _gelu_tanh(x):
    return 0.5 * x * (1.0 + jnp.tanh(np.sqrt(2.0 / np.pi).astype(np.float32) * (x + 0.044715 * (x * x * x))))


def _sigmoid(x):
    return 1.0 / (1.0 + jnp.exp(-x))


def _rms_norm(x, g):
    return x * lax.rsqrt(jnp.mean(x * x, axis=-1, keepdims=True) + EPS) * g


def _ada_kernel(c_ref, w_ref, b_ref, o_ref):
    c = c_ref[...]
    a = c * _sigmoid(c)
    o_ref[...] = jnp.dot(a, w_ref[...], preferred_element_type=F32,
                         precision=lax.Precision.HIGHEST) + b_ref[...]


def _ada_call(c, w_ada, b_ada):
    n_out = 6 * D_MODEL
    tn = 1024
    return pl.pallas_call(
        _ada_kernel,
        grid=(n_out // tn,),
        in_specs=[pl.BlockSpec((BATCH, D_MODEL), lambda j: (0, 0)),
                  pl.BlockSpec((D_MODEL, tn), lambda j: (0, j)),
                  pl.BlockSpec((1, tn), lambda j: (0, j))],
        out_specs=pl.BlockSpec((BATCH, tn), lambda j: (0, j)),
        out_shape=jax.ShapeDtypeStruct((BATCH, n_out), F32),
        compiler_params=pltpu.CompilerParams(dimension_semantics=("arbitrary",),
                                             vmem_limit_bytes=VMEM_LIMIT),
        name="ada_mod",
    )(c, w_ada, b_ada.reshape(1, n_out))


def _rope_rows(blk, cos, sin, half):
    x1 = blk[0:half]
    x2 = blk[half:2 * half]
    return x1 * cos - x2 * sin, x2 * cos + x1 * sin


def _inproj_kernel(x_ref, mod_ref, pos_ref, n1g_ref, wn_ref, wt_ref, lng_ref, lnb_ref,
                   kig_ref, kib_ref, invf_ref,
                   u_ref, vln_ref, sga_ref, sgb_ref, qT_ref, k_ref, vT_ref, qiT_ref, ki_ref, wiT_ref):
    tm = x_ref.shape[0]
    x = x_ref[...]
    mod = mod_ref[0]
    shift1 = mod[:, 0:D_MODEL]
    scale1 = mod[:, D_MODEL:2 * D_MODEL]
    h = _rms_norm(x, n1g_ref[...]) * (1.0 + scale1) + shift1
    hb = h.astype(BF16)

    zu = jnp.dot(hb, wn_ref[:, 0:A_WIDTH], preferred_element_type=F32)
    u_ref[...] = _gelu_tanh(zu).astype(BF16)
    zv = _gelu_tanh(jnp.dot(hb, wn_ref[:, A_WIDTH:2 * A_WIDTH], preferred_element_type=F32))
    mu = jnp.mean(zv, axis=-1, keepdims=True)
    zc = zv - mu
    var = jnp.mean(zc * zc, axis=-1, keepdims=True)
    vln_ref[...] = (zc * lax.rsqrt(var + EPS) * lng_ref[...] + lnb_ref[...]).astype(BF16)
    zga = jnp.dot(hb, wn_ref[:, 2 * A_WIDTH:2 * A_WIDTH + D_MODEL], preferred_element_type=F32)
    sga_ref[...] = _sigmoid(zga).astype(BF16)
    zgb = jnp.dot(hb, wn_ref[:, 2 * A_WIDTH + D_MODEL:], preferred_element_type=F32)
    sgb_ref[...] = _sigmoid(zgb).astype(BF16)

    nt = (((1,), (1,)), ((), ()))
    pos = pos_ref[0].astype(F32)
    ang = invf_ref[...] * pos
    cos = jnp.cos(ang)
    sin = jnp.sin(ang)
    hq = ROT_DIM // 2
    hi = IDX_ROT_DIM // 2
    cos_q, sin_q = cos[0:hq], sin[0:hq]
    cos_i, sin_i = cos[hq:hq + hi], sin[hq:hq + hi]
    n_sub = tm // Q_TILE

    zq = lax.dot_general(wt_ref[_T_Q:_T_K, :], hb, nt, preferred_element_type=F32)
    q_scale = HEAD_DIM ** -0.5 * float(np.log2(np.e))
    for hd in range(N_HEADS):
        blk = zq[hd * HEAD_DIM:(hd + 1) * HEAD_DIM]
        r1, r2 = _rope_rows(blk, cos_q, sin_q, hq)
        full = (jnp.concatenate([r1, r2, blk[ROT_DIM:]], axis=0) * q_scale).astype(BF16)
        for s in range(n_sub):
            qT_ref[s, :, hd * Q_TILE:(hd + 1) * Q_TILE] = full[:, s * Q_TILE:(s + 1) * Q_TILE]

    zk = lax.dot_general(wt_ref[_T_K:_T_V, :], hb, nt, preferred_element_type=F32)
    k_rows = []
    for g in range(N_KV_HEADS):
        blk = zk[g * HEAD_DIM:(g + 1) * HEAD_DIM]
        r1, r2 = _rope_rows(blk, cos_q, sin_q, hq)
        k_rows += [r1, r2, blk[ROT_DIM:]]
    k_ref[...] = jnp.concatenate(k_rows, axis=0).T.astype(BF16)

    zvv = lax.dot_general(wt_ref[_T_V:_T_QI, :], hb, nt, preferred_element_type=F32)
    vT_ref[...] = zvv.astype(BF16)

    zqi = lax.dot_general(wt_ref[_T_QI:_T_KI, :], hb, nt, preferred_element_type=F32)
    for hd in range(IDX_HEADS):
        blk = zqi[hd * IDX_DIM:(hd + 1) * IDX_DIM]
        r1, r2 = _rope_rows(blk, cos_i, sin_i, hi)
        full = jnp.concatenate([r1, r2, blk[IDX_ROT_DIM:]], axis=0).astype(BF16)
        for s in range(n_sub):
            qiT_ref[s, :, hd * Q_TILE:(hd + 1) * Q_TILE] = full[:, s * Q_TILE:(s + 1) * Q_TILE]

    zrest = lax.dot_general(wt_ref[_T_KI:_T_ROWS, :], hb, nt, preferred_element_type=F32)
    zki = zrest[0:IDX_DIM]
    kmu = jnp.mean(zki, axis=0, keepdims=True)
    kc = zki - kmu
    kvar = jnp.mean(kc * kc, axis=0, keepdims=True)
    kin = kc * lax.rsqrt(kvar + EPS) * kig_ref[...] + kib_ref[...]
    r1, r2 = _rope_rows(kin, cos_i, sin_i, hi)
    ki_full = jnp.concatenate([r1, r2, kin[IDX_ROT_DIM:], jnp.zeros((128 - IDX_DIM, tm), F32)], axis=0)
    ki_ref[...] = ki_full.T.astype(BF16)
    wiT_ref[...] = zrest[IDX_DIM:IDX_DIM + IDX_HEADS] * ((IDX_HEADS ** -0.5) * (IDX_DIM ** -0.5))


def _inproj_call(x2, mod3, pos3, n1g, wn, wt, lng, lnb, kig, kib, invf):
    tm = TM_IN
    n_tiles = TOKENS // tm
    per_b = SEQ // tm
    n_sub = tm // Q_TILE
    const2 = lambda t: (0, 0)
    row = lambda t: (t, 0)
    in_specs = [
        pl.BlockSpec((tm, D_MODEL), row),
        pl.BlockSpec((1, 1, 6 * D_MODEL), lambda t: (t // per_b, 0, 0)),
        pl.BlockSpec((1, 1, tm), lambda t: (t // per_b, 0, t % per_b)),
        pl.BlockSpec((1, D_MODEL), const2),
        pl.BlockSpec(wn.shape, const2),
        pl.BlockSpec(wt.shape, const2),
        pl.BlockSpec((1, A_WIDTH), const2),
        pl.BlockSpec((1, A_WIDTH), const2),
        pl.BlockSpec((IDX_DIM, 1), const2),
        pl.BlockSpec((IDX_DIM, 1), const2),
        pl.BlockSpec(invf.shape, const2),
    ]
    out_shape = [
        jax.ShapeDtypeStruct((TOKENS, A_WIDTH), BF16),
        jax.ShapeDtypeStruct((TOKENS, A_WIDTH), BF16),
        jax.ShapeDtypeStruct((TOKENS, D_MODEL), BF16),
        jax.ShapeDtypeStruct((TOKENS, D_MODEL), BF16),
        jax.ShapeDtypeStruct((TOKENS // Q_TILE, HEAD_DIM, N_HEADS * Q_TILE), BF16),
        jax.ShapeDtypeStruct((TOKENS, N_KV_HEADS * HEAD_DIM), BF16),
        jax.ShapeDtypeStruct((N_KV_HEADS * HEAD_DIM, TOKENS), BF16),
        jax.ShapeDtypeStruct((TOKENS // Q_TILE, IDX_DIM, IDX_HEADS * Q_TILE), BF16),
        jax.ShapeDtypeStruct((TOKENS, 128), BF16),
        jax.ShapeDtypeStruct((IDX_HEADS, TOKENS), F32),
    ]
    out_specs = [
        pl.BlockSpec((tm, A_WIDTH), row),
        pl.BlockSpec((tm, A_WIDTH), row),
        pl.BlockSpec((tm, D_MODEL), row),
        pl.BlockSpec((tm, D_MODEL), row),
        pl.BlockSpec((n_sub, HEAD_DIM, N_HEADS * Q_TILE), lambda t: (t, 0, 0)),
        pl.BlockSpec((tm, N_KV_HEADS * HEAD_DIM), row),
        pl.BlockSpec((N_KV_HEADS * HEAD_DIM, tm), lambda t: (0, t)),
        pl.BlockSpec((n_sub, IDX_DIM, IDX_HEADS * Q_TILE), lambda t: (t, 0, 0)),
        pl.BlockSpec((tm, 128), row),
        pl.BlockSpec((IDX_HEADS, tm), lambda t: (0, t)),
    ]
    return pl.pallas_call(
        _inproj_kernel,
        grid=(n_tiles,),
        in_specs=in_specs,
        out_specs=out_specs,
        out_shape=out_shape,
        compiler_params=pltpu.CompilerParams(dimension_semantics=("arbitrary",),
                                             vmem_limit_bytes=VMEM_LIMIT),
        name="in_proj",
    )(x2, mod3, pos3, n1g, wn, wt, lng, lnb, kig, kib, invf)


def _col_reduce(x, op):
    return op(x.reshape(x.shape[0] // ACC_ROWS, ACC_ROWS, x.shape[1]), axis=0)


def _key_tiles(tile):
    return (tile * Q_TILE + Q_TILE + KEY_TILE - 1) // KEY_TILE


def _attn_kernel(qT_ref, qiT_ref, wiT_ref, k_ref, ki_ref, vT_ref, y_ref,
                 sc_ref, acc_ref, l_ref, vsel_ref):
    t = pl.program_id(1)
    nq = SEQ // Q_TILE
    i = jnp.minimum(t, nq - 1)
    ia = jnp.maximum(t - 1, 0)
    sc_i = sc_ref.at[i % 2]
    sc_a = sc_ref.at[ia % 2]
    nkt = jnp.where(t < nq, _key_tiles(i), 0)
    nka = jnp.where(t >= 1, _key_tiles(ia), 0)
    kf = float(TOPK)
    gw = HEADS_PER_KV * Q_TILE
    hw = N_HEADS * Q_TILE

    @pl.when(t == 0)
    def _():
        vsel_ref[...] = jnp.zeros_like(vsel_ref)

    qiT = qiT_ref[0]
    qT = qT_ref[0]
    w = wiT_ref[...]
    wrow = jnp.concatenate([w[hd:hd + 1, :] for hd in range(IDX_HEADS)], axis=1)
    vsel_a = vsel_ref[ia % 2]

    row_i = lax.broadcasted_iota(jnp.int32, (KEY_TILE, Q_TILE), 0)
    qidx = i * Q_TILE + lax.broadcasted_iota(jnp.int32, (KEY_TILE, Q_TILE), 1)

    def score_tile(j, mx_a, mn_a):
        sl = pl.ds(pl.multiple_of(j * KEY_TILE, KEY_TILE), KEY_TILE)
        lg = jnp.dot(ki_ref[sl, 0:IDX_DIM], qiT, preferred_element_type=F32)
        r = jnp.maximum(lg, 0.0) * wrow
        s = r[:, 0:Q_TILE]
        for hd in range(1, IDX_HEADS):
            s = s + r[:, hd * Q_TILE:(hd + 1) * Q_TILE]
        causal = j * KEY_TILE + row_i <= qidx
        sc_i[sl, :] = jnp.where(causal, s, -jnp.inf)
        mx_a = jnp.maximum(mx_a, _col_reduce(jnp.where(causal, s, -jnp.inf), jnp.max))
        mn_a = jnp.minimum(mn_a, _col_reduce(jnp.where(causal, s, jnp.inf), jnp.min))
        return mx_a, mn_a

    def key_mask(sl):
        return jnp.where(sc_a[sl, :] >= vsel_a, 0.0, NEG_INF)

    def qk_tile(j):
        sl = pl.ds(pl.multiple_of(j * KEY_TILE, KEY_TILE), KEY_TILE)
        return [jnp.dot(k_ref[sl, g * HEAD_DIM:(g + 1) * HEAD_DIM], qT[:, g * gw:(g + 1) * gw],
                        preferred_element_type=F32) for g in range(N_KV_HEADS)]

    def attend_tile(j, qk, l8, m):
        sl = pl.ds(pl.multiple_of(j * KEY_TILE, KEY_TILE), KEY_TILE)
        mask = key_mask(sl)
        parts = []
        for g in range(N_KV_HEADS):
            a = qk[g]
            ps = []
            for r in range(HEADS_PER_KV):
                e = a[:, r * Q_TILE:(r + 1) * Q_TILE] + mask
                if m is not None:
                    c0 = (g * HEADS_PER_KV + r) * Q_TILE
                    e = e - m[:, c0:c0 + Q_TILE]
                ps.append(jnp.exp2(e))
            p = jnp.concatenate(ps, axis=1)
            acc_ref[:, g * gw:(g + 1) * gw] += jnp.dot(
                vT_ref[g * HEAD_DIM:(g + 1) * HEAD_DIM, sl], p.astype(BF16), preferred_element_type=F32)
            parts.append(jnp.sum(p.reshape(KEY_TILE // 8, 8, gw), axis=0))
        return l8 + jnp.concatenate(parts, axis=1)

    acc_ref[...] = jnp.zeros_like(acc_ref)
    n_both = jnp.minimum(nkt, nka)

    def both_body(j, c):
        mx_a, mn_a, l8 = c
        mx_a, mn_a = score_tile(j, mx_a, mn_a)
        return mx_a, mn_a, attend_tile(j, qk_tile(j), l8, None)

    def score_body(j, c):
        mx_a, mn_a, l8 = c
        mx_a, mn_a = score_tile(j, mx_a, mn_a)
        return mx_a, mn_a, l8

    def attend_body(j, c):
        mx_a, mn_a, l8 = c
        return mx_a, mn_a, attend_tile(j, qk_tile(j), l8, None)

    carry = (jnp.full((ACC_ROWS, Q_TILE), -jnp.inf, F32), jnp.full((ACC_ROWS, Q_TILE), jnp.inf, F32),
             jnp.zeros((8, hw), F32))
    carry = lax.fori_loop(0, n_both, both_body, carry)
    carry = lax.fori_loop(n_both, nkt, score_body, carry)
    mx_a, mn_a, l8 = lax.fori_loop(n_both, nka, attend_body, carry)
    l_ref[...] = jnp.sum(l8, axis=0, keepdims=True)

    def tiles(fn, init):
        def body(j, carry):
            s = sc_i[pl.ds(pl.multiple_of(j * KEY_TILE, KEY_TILE), KEY_TILE), :]
            return fn(s, carry)
        return lax.fori_loop(0, nkt, body, init)

    def count_ge(t):
        c8 = tiles(lambda s, c: c + _col_reduce(jnp.where(s >= t, 1.0, 0.0), jnp.sum),
                   jnp.zeros((ACC_ROWS, Q_TILE), F32))
        return jnp.sum(c8, axis=0, keepdims=True)

    def bisect(state, n):
        def step(_, c):
            lo, hi, n_lo, n_hi = c
            mid = 0.5 * lo + 0.5 * hi
            n_mid = count_ge(mid)
            ok = n_mid >= kf
            return (jnp.where(ok, mid, lo), jnp.where(ok, hi, mid),
                    jnp.where(ok, n_mid, n_lo), jnp.where(ok, n_hi, n_mid))
        return lax.fori_loop(0, n, step, state)

    @pl.when(t < TOPK // Q_TILE)
    def _():
        vsel_ref[i % 2] = jnp.full((1, Q_TILE), jnp.finfo(jnp.float32).min, F32)

    @pl.when((t >= TOPK // Q_TILE) & (t < nq))
    def _():
        mx = jnp.max(mx_a, axis=0, keepdims=True)
        mn = jnp.min(mn_a, axis=0, keepdims=True)
        n_valid = (qidx[0:1, :] + 1).astype(F32)
        hi0 = mx + jnp.maximum(jnp.abs(mx), 1e-30) * 1e-6
        state = bisect((mn, hi0, n_valid, jnp.zeros((1, Q_TILE), F32)), 16)

        def check(lo, hi):
            def f(s, c):
                a8, b8 = c
                a8 = jnp.maximum(a8, _col_reduce(jnp.where(s < hi, s, -jnp.inf), jnp.max))
                b8 = jnp.minimum(b8, _col_reduce(jnp.where(s >= lo, s, jnp.inf), jnp.min))
                return a8, b8
            a8, b8 = tiles(f, (jnp.full((ACC_ROWS, Q_TILE), -jnp.inf, F32), jnp.full((ACC_ROWS, Q_TILE), jnp.inf, F32)))
            below_hi = jnp.max(a8, axis=0, keepdims=True)
            above_lo = jnp.min(b8, axis=0, keepdims=True)
            return above_lo, jnp.min(jnp.where(below_hi == above_lo, 1.0, 0.0))

        def w_cond(c):
            return c[2] < 0.5

        def w_body(c):
            st = bisect(c[0], 4)
            v, done = check(st[0], st[1])
            return st, v, done

        (_, _, n_ge, n_gt), vk, _ = lax.while_loop(w_cond, w_body, (state, mn, jnp.float32(0.0)))
        need = kf - n_gt
        has_tie = jnp.max(n_ge) > kf + 0.5

        @pl.when(jnp.logical_not(has_tie))
        def _():
            vsel_ref[i % 2] = vk

        @pl.when(has_tie)
        def _():
            vsel_ref[i % 2] = jnp.full((1, Q_TILE), -0.5, F32)
            r_i = lax.broadcasted_iota(jnp.int32, (KEY_TILE, KEY_TILE), 0)
            c_i = lax.broadcasted_iota(jnp.int32, (KEY_TILE, KEY_TILE), 1)
            tri = jnp.where(r_i >= c_i, 1.0, 0.0).astype(BF16)

            def body(j, seen):
                sl = pl.ds(pl.multiple_of(j * KEY_TILE, KEY_TILE), KEY_TILE)
                s = sc_i[sl, :]
                eq = s == vk
                rank = jnp.dot(tri, jnp.where(eq, 1.0, 0.0).astype(BF16), preferred_element_type=F32) + seen
                sel = (s > vk) | (eq & (rank <= need))
                sc_i[sl, :] = jnp.where(sel, 0.0, NEG_INF)
                return rank[KEY_TILE - 1:KEY_TILE, :]
            lax.fori_loop(0, nkt, body, jnp.zeros((1, Q_TILE), F32))

    @pl.when(t >= 1)
    def _():
        l_fast = l_ref[...]
        in_range = (jnp.min(l_fast) > 1e-20) & (jnp.max(l_fast) < 1e30)

        @pl.when(jnp.logical_not(in_range))
        def _():
            def max_body(j, m8):
                sl = pl.ds(pl.multiple_of(j * KEY_TILE, KEY_TILE), KEY_TILE)
                mask = key_mask(sl)
                parts = []
                for g in range(N_KV_HEADS):
                    a = jnp.dot(k_ref[sl, g * HEAD_DIM:(g + 1) * HEAD_DIM], qT[:, g * gw:(g + 1) * gw],
                                preferred_element_type=F32)
                    a = a + jnp.concatenate([mask] * HEADS_PER_KV, axis=1)
                    parts.append(jnp.max(a.reshape(KEY_TILE // 8, 8, gw), axis=0))
                return jnp.maximum(m8, jnp.concatenate(parts, axis=1))
            m8 = lax.fori_loop(0, nka, max_body, jnp.full((8, hw), -jnp.inf, F32))
            m = jnp.max(m8, axis=0, keepdims=True)
            acc_ref[...] = jnp.zeros_like(acc_ref)
            l8 = lax.fori_loop(0, nka, lambda j, c: attend_tile(j, qk_tile(j), c, m), jnp.zeros((8, hw), F32))
            l_ref[...] = jnp.sum(l8, axis=0, keepdims=True)

        oT = acc_ref[...] / l_ref[...]
        for hd in range(N_HEADS):
            y_ref[:, hd * HEAD_DIM:(hd + 1) * HEAD_DIM] = oT[:, hd * Q_TILE:(hd + 1) * Q_TILE].T.astype(BF16)


def _attn_call(qT, qiT, wiT, k, ki, vT):
    nq = SEQ // Q_TILE
    scored = lambda b, t: b * nq + jnp.minimum(t, nq - 1)
    attended = lambda b, t: b * nq + jnp.maximum(t - 1, 0)
    return pl.pallas_call(
        _attn_kernel,
        grid=(BATCH, nq + 1),
        in_specs=[
            pl.BlockSpec((1, HEAD_DIM, N_HEADS * Q_TILE), lambda b, t: (attended(b, t), 0, 0)),
            pl.BlockSpec((1, IDX_DIM, IDX_HEADS * Q_TILE), lambda b, t: (scored(b, t), 0, 0)),
            pl.BlockSpec((IDX_HEADS, Q_TILE), lambda b, t: (0, scored(b, t))),
            pl.BlockSpec((SEQ, N_KV_HEADS * HEAD_DIM), lambda b, t: (b, 0)),
            pl.BlockSpec((SEQ, 128), lambda b, t: (b, 0)),
            pl.BlockSpec((N_KV_HEADS * HEAD_DIM, SEQ), lambda b, t: (0, b)),
        ],
        out_specs=pl.BlockSpec((Q_TILE, N_HEADS * HEAD_DIM), lambda b, t: (attended(b, t), 0)),
        out_shape=jax.ShapeDtypeStruct((TOKENS, N_HEADS * HEAD_DIM), BF16),
        scratch_shapes=[pltpu.VMEM((2, SEQ, Q_TILE), F32),
                        pltpu.VMEM((HEAD_DIM, N_HEADS * Q_TILE), F32),
                        pltpu.VMEM((1, N_HEADS * Q_TILE), F32),
                        pltpu.VMEM((2, 1, Q_TILE), F32)],
        compiler_params=pltpu.CompilerParams(dimension_semantics=("arbitrary", "arbitrary"),
                                             vmem_limit_bytes=VMEM_LIMIT),
        name="dsa_attention",
    )(qT, qiT, wiT, k, ki, vT)


def _post_kernel(x_ref, mod_ref, u_ref, vln_ref, sga_ref, sgb_ref, yb_ref, ws_ref, bsT_ref,
                 wpa_ref, wpb_ref, wout_ref, n2g_ref, wfi_ref, wfo_ref, fg_ref, o_ref, ya_ref):
    tm = x_ref.shape[0]
    mod = mod_ref[0]
    gate1 = mod[:, 2 * D_MODEL:3 * D_MODEL]
    shift2 = mod[:, 3 * D_MODEL:4 * D_MODEL]
    scale2 = mod[:, 4 * D_MODEL:5 * D_MODEL]
    gate2 = mod[:, 5 * D_MODEL:6 * D_MODEL]

    r_i = lax.broadcasted_iota(jnp.int32, (CHUNK, CHUNK), 0)
    c_i = lax.broadcasted_iota(jnp.int32, (CHUNK, CHUNK), 1)
    bsT = bsT_ref[...]
    gd = A_WIDTH // A_GROUPS
    for g in range(A_GROUPS):
        wm = jnp.where(r_i >= c_i, ws_ref[g], 0.0).astype(BF16)
        bias = bsT[:, g:g + 1]
        for c in range(tm // CHUNK):
            rows = slice(c * CHUNK, (c + 1) * CHUNK)
            cols = slice(g * gd, (g + 1) * gd)
            mixed = jnp.dot(wm, vln_ref[rows, cols], preferred_element_type=F32) + bias
            ya_ref[rows, cols] = (u_ref[rows, cols].astype(F32) * mixed).astype(BF16)

    pa = jnp.dot(ya_ref[...], wpa_ref[...], preferred_element_type=F32)
    pb = jnp.dot(yb_ref[...], wpb_ref[...], preferred_element_type=F32)
    merged = sga_ref[...].astype(F32) * pa + sgb_ref[...].astype(F32) * pb
    x1 = x_ref[...] + gate1 * jnp.dot(merged.astype(BF16), wout_ref[...], preferred_element_type=F32)

    h2 = (_rms_norm(x1, n2g_ref[...]) * (1.0 + scale2) + shift2).astype(BF16)
    fg = jnp.dot(h2, wfi_ref[:, 0:D_FF], preferred_element_type=F32)
    fu = jnp.dot(h2, wfi_ref[:, D_FF:], preferred_element_type=F32)
    act = (fg * _sigmoid(fg) * fu).astype(BF16)
    x2 = x1 + gate2 * jnp.dot(act, wfo_ref[...], preferred_element_type=F32)
    o_ref[...] = _rms_norm(x2, fg_ref[...])


def _post_call(x2, mod3, u, vln, sga, sgb, yb, ws, bsT, wpa, wpb, wout, n2g, wfi, wfo, fg):
    tm = TM_OUT
    per_b = SEQ // tm
    row = lambda t: (t, 0)
    const2 = lambda t: (0, 0)
    tok = lambda: pl.BlockSpec((tm, D_MODEL), row)
    in_specs = [
        tok(),
        pl.BlockSpec((1, 1, 6 * D_MODEL), lambda t: (t // per_b, 0, 0)),
        tok(), tok(), tok(), tok(), tok(),
        pl.BlockSpec(ws.shape, lambda t: (0, 0, 0)),
        pl.BlockSpec(bsT.shape, const2),
        pl.BlockSpec(wpa.shape, const2),
        pl.BlockSpec(wpb.shape, const2),
        pl.BlockSpec(wout.shape, const2),
        pl.BlockSpec((1, D_MODEL), const2),
        pl.BlockSpec(wfi.shape, const2),
        pl.BlockSpec(wfo.shape, const2),
        pl.BlockSpec((1, D_MODEL), const2),
    ]
    return pl.pallas_call(
        _post_kernel,
        grid=(TOKENS // tm,),
        in_specs=in_specs,
        out_specs=pl.BlockSpec((tm, D_MODEL), row),
        out_shape=jax.ShapeDtypeStruct((TOKENS, D_MODEL), F32),
        scratch_shapes=[pltpu.VMEM((tm, A_WIDTH), BF16)],
        compiler_params=pltpu.CompilerParams(dimension_semantics=("arbitrary",),
                                             vmem_limit_bytes=VMEM_LIMIT),
        name="merge_ffn",
    )(x2, mod3, u, vln, sga, sgb, yb, ws, bsT, wpa, wpb, wout, n2g, wfi, wfo, fg)


def _inv_freq_table():
    fq = ROPE_THETA ** (-np.arange(0, ROT_DIM, 2, dtype=np.float32) / ROT_DIM)
    fi = ROPE_THETA ** (-np.arange(0, IDX_ROT_DIM, 2, dtype=np.float32) / IDX_ROT_DIM)
    return np.concatenate([fq, fi]).astype(np.float32).reshape(-1, 1)


def kernel(x, c, positions, w_ada, b_ada, norm1_g, w_in, gmlp_ln_g, gmlp_ln_b, gmlp_w_s, gmlp_b_s,
           idx_k_ln_g, idx_k_ln_b, w_proj_a, w_proj_b, w_out, norm2_g, w_ffn_in, w_ffn_out, final_norm_g):
    assert x.shape == (BATCH, SEQ, D_MODEL) and w_in.shape == (1, D_MODEL, _IN_COLS)
    x2 = x.reshape(TOKENS, D_MODEL)
    pos3 = positions.reshape(BATCH, 1, SEQ)
    invf = jnp.asarray(_inv_freq_table())
    xcur = x2
    for l in range(w_ada.shape[0]):
        w = w_in[l]
        wn = jnp.concatenate([w[:, _OFF_U:_OFF_Q], w[:, _OFF_GA:]], axis=1).astype(BF16)
        wt = w[:, _OFF_Q:_OFF_GA].T.astype(BF16)
        mod = _ada_call(c, w_ada[l], b_ada[l])
        mod3 = mod.reshape(BATCH, 1, 6 * D_MODEL)
        u, vln, sga, sgb, qT, k, vT, qiT, ki, wiT = _inproj_call(
            xcur, mod3, pos3, norm1_g[l].reshape(1, -1), wn, wt,
            gmlp_ln_g[l].reshape(1, -1), gmlp_ln_b[l].reshape(1, -1),
            idx_k_ln_g[l].reshape(-1, 1), idx_k_ln_b[l].reshape(-1, 1), invf)
        yb = _attn_call(qT, qiT, wiT, k, ki, vT)
        last = l == w_ada.shape[0] - 1
        fg = final_norm_g.reshape(1, -1) if last else None
        assert last, "single-layer block"
        xcur = _post_call(xcur, mod3, u, vln, sga, sgb, yb, gmlp_w_s[l], gmlp_b_s[l].T,
                          w_proj_a[l].astype(BF16), w_proj_b[l].astype(BF16), w_out[l].astype(BF16),
                          norm2_g[l].reshape(1, -1), w_ffn_in[l].astype(BF16), w_ffn_out[l].astype(BF16), fg)
    return xcur.reshape(BATCH, SEQ, D_MODEL)
```

```python
import functools

import numpy as np
import jax
import jax.numpy as jnp
from jax import lax
from jax.experimental import pallas as pl
from jax.experimental.pallas import tpu as pltpu

D_MODEL = 1024
BATCH = 4
SEQ = 4096
CHUNK = 128
A_GROUPS = 8
A_WIDTH = 1024
N_HEADS = 8
N_KV_HEADS = 2
HEAD_DIM = 128
IDX_HEADS = 8
IDX_DIM = 64
TOPK = 256
ROPE_THETA = 500000.0
ROT_DIM = HEAD_DIM // 4
IDX_ROT_DIM = IDX_DIM // 4
D_FF = 2816
EPS = 1e-6
NEG_INF = -1e30

TOKENS = BATCH * SEQ
Q_TILE = 128
KEY_TILE = 512
ACC_ROWS = 32
FIRST_CHECK_AFTER = 16
CHECK_EVERY = 2
TM_IN = 512
TM_OUT = 256
HEADS_PER_KV = N_HEADS // N_KV_HEADS
VMEM_LIMIT = 56 * 1024 * 1024

_OFF_U = 0
_OFF_V = _OFF_U + A_WIDTH
_OFF_Q = _OFF_V + A_WIDTH
_OFF_K = _OFF_Q + N_HEADS * HEAD_DIM
_OFF_VV = _OFF_K + N_KV_HEADS * HEAD_DIM
_OFF_QI = _OFF_VV + N_KV_HEADS * HEAD_DIM
_OFF_KI = _OFF_QI + IDX_HEADS * IDX_DIM
_OFF_WI = _OFF_KI + IDX_DIM
_OFF_GA = _OFF_WI + IDX_HEADS
_OFF_GB = _OFF_GA + D_MODEL
_IN_COLS = _OFF_GB + D_MODEL

_T_Q = 0
_T_K = _T_Q + N_HEADS * HEAD_DIM
_T_V = _T_K + N_KV_HEADS * HEAD_DIM
_T_QI = _T_V + N_KV_HEADS * HEAD_DIM
_T_KI = _T_QI + IDX_HEADS * IDX_DIM
_T_WI = _T_KI + IDX_DIM
_T_ROWS = _T_WI + IDX_HEADS

F32 = jnp.float32
BF16 = jnp.bfloat16


def _gelu_tanh(x):
    return 0.5 * x * (1.0 + jnp.tanh(np.sqrt(2.0 / np.pi).astype(np.float32) * (x + 0.044715 * (x * x * x))))


def _sigmoid(x):
    return 1.0 / (1.0 + jnp.exp(-x))


def _rms_norm(x, g):
    return x * lax.rsqrt(jnp.mean(x * x, axis=-1, keepdims=True) + EPS) * g


def _ada_kernel(c_ref, w_ref, b_ref, o_ref):
    c = c_ref[...]
    a = c * _sigmoid(c)
    w = w_ref[...]
    a_hi = a.astype(BF16)
    a_lo = (a - a_hi.astype(F32)).astype(BF16)
    w_hi = w.astype(BF16)
    w_lo = (w - w_hi.astype(F32)).astype(BF16)
    dot = functools.partial(jnp.dot, preferred_element_type=F32)
    o_ref[...] = dot(a_hi, w_hi) + (dot(a_hi, w_lo) + dot(a_lo, w_hi)) + b_ref[...]


def _ada_call(c, w_ada, b_ada):
    n_out = 6 * D_MODEL
    tn = 1024
    return pl.pallas_call(
        _ada_kernel,
        grid=(n_out // tn,),
        in_specs=[pl.BlockSpec((BATCH, D_MODEL), lambda j: (0, 0)),
                  pl.BlockSpec((D_MODEL, tn), lambda j: (0, j)),
                  pl.BlockSpec((1, tn), lambda j: (0, j))],
        out_specs=pl.BlockSpec((BATCH, tn), lambda j: (0, j)),
        out_shape=jax.ShapeDtypeStruct((BATCH, n_out), F32),
        compiler_params=pltpu.CompilerParams(dimension_semantics=("arbitrary",),
                                             vmem_limit_bytes=VMEM_LIMIT),
        name="ada_mod",
    )(c, w_ada, b_ada.reshape(1, n_out))


def _rope_rows(blk, cos, sin, half):
    x1 = blk[0:half]
    x2 = blk[half:2 * half]
    return x1 * cos - x2 * sin, x2 * cos + x1 * sin


def _inproj_kernel(x_ref, mod_ref, pos_ref, n1g_ref, wn_ref, wt_ref, lng_ref, lnb_ref,
                   kig_ref, kib_ref, invf_ref,
                   u_ref, vln_ref, sga_ref, sgb_ref, qT_ref, k_ref, vT_ref, qiT_ref, ki_ref, wiT_ref):
    tm = x_ref.shape[0]
    x = x_ref[...]
    mod = mod_ref[0]
    shift1 = mod[:, 0:D_MODEL]
    scale1 = mod[:, D_MODEL:2 * D_MODEL]
    h = _rms_norm(x, n1g_ref[...]) * (1.0 + scale1) + shift1
    hb = h.astype(BF16)

    zu = jnp.dot(hb, wn_ref[:, 0:A_WIDTH], preferred_element_type=F32)
    u_ref[...] = _gelu_tanh(zu).astype(BF16)
    zv = _gelu_tanh(jnp.dot(hb, wn_ref[:, A_WIDTH:2 * A_WIDTH], preferred_element_type=F32))
    mu = jnp.mean(zv, axis=-1, keepdims=True)
    zc = zv - mu
    var = jnp.mean(zc * zc, axis=-1, keepdims=True)
    vln_ref[...] = (zc * lax.rsqrt(var + EPS) * lng_ref[...] + lnb_ref[...]).astype(BF16)
    zga = jnp.dot(hb, wn_ref[:, 2 * A_WIDTH:2 * A_WIDTH + D_MODEL], preferred_element_type=F32)
    sga_ref[...] = _sigmoid(zga).astype(BF16)
    zgb = jnp.dot(hb, wn_ref[:, 2 * A_WIDTH + D_MODEL:], preferred_element_type=F32)
    sgb_ref[...] = _sigmoid(zgb).astype(BF16)

    nt = (((1,), (1,)), ((), ()))
    pos = pos_ref[0].astype(F32)
    ang = invf_ref[...] * pos
    cos = jnp.cos(ang)
    sin = jnp.sin(ang)
    hq = ROT_DIM // 2
    hi = IDX_ROT_DIM // 2
    cos_q, sin_q = cos[0:hq], sin[0:hq]
    cos_i, sin_i = cos[hq:hq + hi], sin[hq:hq + hi]
    n_sub = tm // Q_TILE

    zq = lax.dot_general(wt_ref[_T_Q:_T_K, :], hb, nt, preferred_element_type=F32)
    q_scale = HEAD_DIM ** -0.5 * float(np.log2(np.e))
    for hd in range(N_HEADS):
        blk = zq[hd * HEAD_DIM:(hd + 1) * HEAD_DIM]
        r1, r2 = _rope_rows(blk, cos_q, sin_q, hq)
        full = (jnp.concatenate([r1, r2, blk[ROT_DIM:]], axis=0) * q_scale).astype(BF16)
        for s in range(n_sub):
            qT_ref[s, :, hd * Q_TILE:(hd + 1) * Q_TILE] = full[:, s * Q_TILE:(s + 1) * Q_TILE]

    zk = lax.dot_general(wt_ref[_T_K:_T_V, :], hb, nt, preferred_element_type=F32)
    k_rows = []
    for g in range(N_KV_HEADS):
        blk = zk[g * HEAD_DIM:(g + 1) * HEAD_DIM]
        r1, r2 = _rope_rows(blk, cos_q, sin_q, hq)
        k_rows += [r1, r2, blk[ROT_DIM:]]
    k_ref[...] = jnp.concatenate(k_rows, axis=0).T.astype(BF16)

    zvv = lax.dot_general(wt_ref[_T_V:_T_QI, :], hb, nt, preferred_element_type=F32)
    vT_ref[...] = zvv.astype(BF16)

    zqi = lax.dot_general(wt_ref[_T_QI:_T_KI, :], hb, nt, preferred_element_type=F32)
    for hd in range(IDX_HEADS):
        blk = zqi[hd * IDX_DIM:(hd + 1) * IDX_DIM]
        r1, r2 = _rope_rows(blk, cos_i, sin_i, hi)
        full = jnp.concatenate([r1, r2, blk[IDX_ROT_DIM:]], axis=0).astype(BF16)
        for s in range(n_sub):
            qiT_ref[s, :, hd * Q_TILE:(hd + 1) * Q_TILE] = full[:, s * Q_TILE:(s + 1) * Q_TILE]

    zrest = lax.dot_general(wt_ref[_T_KI:_T_ROWS, :], hb, nt, preferred_element_type=F32)
    zki = zrest[0:IDX_DIM]
    kmu = jnp.mean(zki, axis=0, keepdims=True)
    kc = zki - kmu
    kvar = jnp.mean(kc * kc, axis=0, keepdims=True)
    kin = kc * lax.rsqrt(kvar + EPS) * kig_ref[...] + kib_ref[...]
    r1, r2 = _rope_rows(kin, cos_i, sin_i, hi)
    ki_full = jnp.concatenate([r1, r2, kin[IDX_ROT_DIM:], jnp.zeros((128 - IDX_DIM, tm), F32)], axis=0)
    ki_ref[...] = ki_full.T.astype(BF16)
    wiT_ref[...] = zrest[IDX_DIM:IDX_DIM + IDX_HEADS] * ((IDX_HEADS ** -0.5) * (IDX_DIM ** -0.5))


def _inproj_call(x2, mod3, pos3, n1g, wn, wt, lng, lnb, kig, kib, invf):
    tm = TM_IN
    n_tiles = TOKENS // tm
    per_b = SEQ // tm
    n_sub = tm // Q_TILE
    const2 = lambda t: (0, 0)
    row = lambda t: (t, 0)
    in_specs = [
        pl.BlockSpec((tm, D_MODEL), row),
        pl.BlockSpec((1, 1, 6 * D_MODEL), lambda t: (t // per_b, 0, 0)),
        pl.BlockSpec((1, 1, tm), lambda t: (t // per_b, 0, t % per_b)),
        pl.BlockSpec((1, D_MODEL), const2),
        pl.BlockSpec(wn.shape, const2),
        pl.BlockSpec(wt.shape, const2),
        pl.BlockSpec((1, A_WIDTH), const2),
        pl.BlockSpec((1, A_WIDTH), const2),
        pl.BlockSpec((IDX_DIM, 1), const2),
        pl.BlockSpec((IDX_DIM, 1), const2),
        pl.BlockSpec(invf.shape, const2),
    ]
    out_shape = [
        jax.ShapeDtypeStruct((TOKENS, A_WIDTH), BF16),
        jax.ShapeDtypeStruct((TOKENS, A_WIDTH), BF16),
        jax.ShapeDtypeStruct((TOKENS, D_MODEL), BF16),
        jax.ShapeDtypeStruct((TOKENS, D_MODEL), BF16),
        jax.ShapeDtypeStruct((TOKENS // Q_TILE, HEAD_DIM, N_HEADS * Q_TILE), BF16),
        jax.ShapeDtypeStruct((TOKENS, N_KV_HEADS * HEAD_DIM), BF16),
        jax.ShapeDtypeStruct((N_KV_HEADS * HEAD_DIM, TOKENS), BF16),
        jax.ShapeDtypeStruct((TOKENS // Q_TILE, IDX_DIM, IDX_HEADS * Q_TILE), BF16),
        jax.ShapeDtypeStruct((TOKENS, 128), BF16),
        jax.ShapeDtypeStruct((IDX_HEADS, TOKENS), F32),
    ]
    out_specs = [
        pl.BlockSpec((tm, A_WIDTH), row),
        pl.BlockSpec((tm, A_WIDTH), row),
        pl.BlockSpec((tm, D_MODEL), row),
        pl.BlockSpec((tm, D_MODEL), row),
        pl.BlockSpec((n_sub, HEAD_DIM, N_HEADS * Q_TILE), lambda t: (t, 0, 0)),
        pl.BlockSpec((tm, N_KV_HEADS * HEAD_DIM), row),
        pl.BlockSpec((N_KV_HEADS * HEAD_DIM, tm), lambda t: (0, t)),
        pl.BlockSpec((n_sub, IDX_DIM, IDX_HEADS * Q_TILE), lambda t: (t, 0, 0)),
        pl.BlockSpec((tm, 128), row),
        pl.BlockSpec((IDX_HEADS, tm), lambda t: (0, t)),
    ]
    return pl.pallas_call(
        _inproj_kernel,
        grid=(n_tiles,),
        in_specs=in_specs,
        out_specs=out_specs,
        out_shape=out_shape,
        compiler_params=pltpu.CompilerParams(dimension_semantics=("arbitrary",),
                                             vmem_limit_bytes=VMEM_LIMIT),
        name="in_proj",
    )(x2, mod3, pos3, n1g, wn, wt, lng, lnb, kig, kib, invf)


def _col_reduce(x, op):
    return op(x.reshape(x.shape[0] // ACC_ROWS, ACC_ROWS, x.shape[1]), axis=0)


def _key_tiles(tile):
    return (tile * Q_TILE + Q_TILE + KEY_TILE - 1) // KEY_TILE


def _attn_kernel(qT_ref, qiT_ref, wiT_ref, k_ref, ki_ref, vT_ref, y_ref,
                 sc_ref, acc_ref, l_ref, vsel_ref):
    t = pl.program_id(1)
    nq = SEQ // Q_TILE
    i = jnp.minimum(t, nq - 1)
    ia = jnp.maximum(t - 1, 0)
    sc_i = sc_ref.at[i % 2]
    sc_a = sc_ref.at[ia % 2]
    nkt = jnp.where(t < nq, _key_tiles(i), 0)
    nka = jnp.where(t >= 1, _key_tiles(ia), 0)
    kf = float(TOPK)
    gw = HEADS_PER_KV * Q_TILE
    hw = N_HEADS * Q_TILE

    @pl.when(t == 0)
    def _():
        vsel_ref[...] = jnp.zeros_like(vsel_ref)

    qiT = qiT_ref[0]
    qT = qT_ref[0]
    w = wiT_ref[...]
    wrow = jnp.concatenate([w[hd:hd + 1, :] for hd in range(IDX_HEADS)], axis=1)
    vsel_a = vsel_ref[ia % 2]

    row_i = lax.broadcasted_iota(jnp.int32, (KEY_TILE, Q_TILE), 0)
    qidx = i * Q_TILE + lax.broadcasted_iota(jnp.int32, (KEY_TILE, Q_TILE), 1)

    def score_tile(j, mx_a, mn_a):
        sl = pl.ds(pl.multiple_of(j * KEY_TILE, KEY_TILE), KEY_TILE)
        lg = jnp.dot(ki_ref[sl, 0:IDX_DIM], qiT, preferred_element_type=F32)
        r = jnp.maximum(lg, 0.0) * wrow
        s = r[:, 0:Q_TILE]
        for hd in range(1, IDX_HEADS):
            s = s + r[:, hd * Q_TILE:(hd + 1) * Q_TILE]
        causal = j * KEY_TILE + row_i <= qidx
        sc_i[sl, :] = jnp.where(causal, s, -jnp.inf)
        mx_a = jnp.maximum(mx_a, _col_reduce(jnp.where(causal, s, -jnp.inf), jnp.max))
        mn_a = jnp.minimum(mn_a, _col_reduce(jnp.where(causal, s, jnp.inf), jnp.min))
        return mx_a, mn_a

    def key_mask(sl):
        return jnp.where(sc_a[sl, :] >= vsel_a, 0.0, NEG_INF)

    def qk_tile(j):
        sl = pl.ds(pl.multiple_of(j * KEY_TILE, KEY_TILE), KEY_TILE)
        return [jnp.dot(k_ref[sl, g * HEAD_DIM:(g + 1) * HEAD_DIM], qT[:, g * gw:(g + 1) * gw],
                        preferred_element_type=F32) for g in range(N_KV_HEADS)]

    def attend_tile(j, qk, l8, m):
        sl = pl.ds(pl.multiple_of(j * KEY_TILE, KEY_TILE), KEY_TILE)
        mask = key_mask(sl)
        parts = []
        for g in range(N_KV_HEADS):
            a = qk[g]
            ps = []
            for r in range(HEADS_PER_KV):
                e = a[:, r * Q_TILE:(r + 1) * Q_TILE] + mask
                if m is not None:
                    c0 = (g * HEADS_PER_KV + r) * Q_TILE
                    e = e - m[:, c0:c0 + Q_TILE]
                ps.append(jnp.exp2(e))
            p = jnp.concatenate(ps, axis=1)
            acc_ref[:, g * gw:(g + 1) * gw] += jnp.dot(
                vT_ref[g * HEAD_DIM:(g + 1) * HEAD_DIM, sl], p.astype(BF16), preferred_element_type=F32)
            parts.append(jnp.sum(p.reshape(KEY_TILE // 8, 8, gw), axis=0))
        return l8 + jnp.concatenate(parts, axis=1)

    acc_ref[...] = jnp.zeros_like(acc_ref)
    n_both = jnp.minimum(nkt, nka)

    def both_body(j, c):
        mx_a, mn_a, l8 = c
        mx_a, mn_a = score_tile(j, mx_a, mn_a)
        return mx_a, mn_a, attend_tile(j, qk_tile(j), l8, None)

    def score_body(j, c):
        mx_a, mn_a, l8 = c
        mx_a, mn_a = score_tile(j, mx_a, mn_a)
        return mx_a, mn_a, l8

    def attend_body(j, c):
        mx_a, mn_a, l8 = c
        return mx_a, mn_a, attend_tile(j, qk_tile(j), l8, None)

    carry = (jnp.full((ACC_ROWS, Q_TILE), -jnp.inf, F32), jnp.full((ACC_ROWS, Q_TILE), jnp.inf, F32),
             jnp.zeros((8, hw), F32))
    carry = lax.fori_loop(0, n_both, both_body, carry)
    carry = lax.fori_loop(n_both, nkt, score_body, carry)
    mx_a, mn_a, l8 = lax.fori_loop(n_both, nka, attend_body, carry)
    l_ref[...] = jnp.sum(l8, axis=0, keepdims=True)

    def tiles(fn, init):
        def body(j, carry):
            s = sc_i[pl.ds(pl.multiple_of(j * KEY_TILE, KEY_TILE), KEY_TILE), :]
            return fn(s, carry)
        return lax.fori_loop(0, nkt, body, init)

    def count_ge(t):
        c8 = tiles(lambda s, c: c + _col_reduce(jnp.where(s >= t, 1.0, 0.0), jnp.sum),
                   jnp.zeros((ACC_ROWS, Q_TILE), F32))
        return jnp.sum(c8, axis=0, keepdims=True)

    def bisect(state, n):
        def step(_, c):
            lo, hi, n_lo, n_hi = c
            mid = 0.5 * lo + 0.5 * hi
            n_mid = count_ge(mid)
            ok = n_mid >= kf
            return (jnp.where(ok, mid, lo), jnp.where(ok, hi, mid),
                    jnp.where(ok, n_mid, n_lo), jnp.where(ok, n_hi, n_mid))
        return lax.fori_loop(0, n, step, state)

    @pl.when(t < TOPK // Q_TILE)
    def _():
        vsel_ref[i % 2] = jnp.full((1, Q_TILE), jnp.finfo(jnp.float32).min, F32)

    @pl.when((t >= TOPK // Q_TILE) & (t < nq))
    def _():
        mx = jnp.max(mx_a, axis=0, keepdims=True)
        mn = jnp.min(mn_a, axis=0, keepdims=True)
        n_valid = (qidx[0:1, :] + 1).astype(F32)
        hi0 = mx + jnp.maximum(jnp.abs(mx), 1e-30) * 1e-6
        state = bisect((mn, hi0, n_valid, jnp.zeros((1, Q_TILE), F32)), FIRST_CHECK_AFTER - CHECK_EVERY)

        def check(state):
            lo, hi, n_lo, n_hi = state

            def f(s, c):
                a8, b8 = c
                a8 = jnp.maximum(a8, _col_reduce(jnp.where(s < hi, s, -jnp.inf), jnp.max))
                b8 = jnp.minimum(b8, _col_reduce(jnp.where(s >= lo, s, jnp.inf), jnp.min))
                return a8, b8
            a8, b8 = tiles(f, (jnp.full((ACC_ROWS, Q_TILE), -jnp.inf, F32), jnp.full((ACC_ROWS, Q_TILE), jnp.inf, F32)))
            top = jnp.max(a8, axis=0, keepdims=True)
            bottom = jnp.min(b8, axis=0, keepdims=True)
            single = top == bottom
            pinned = single | (n_lo - n_hi < 2.5)
            take_top = jnp.logical_not(single) & (kf - n_hi < 1.5)
            take_bottom = jnp.logical_not(single | take_top)
            v = jnp.where(take_top, top, bottom)
            n_ge = jnp.where(take_top, n_hi + 1.0, n_lo)
            n_gt = jnp.where(take_bottom, n_hi + 1.0, n_hi)
            return (v, n_ge, n_gt), jnp.min(jnp.where(pinned, 1.0, 0.0))

        def w_cond(c):
            return c[2] < 0.5

        def w_body(c):
            st = bisect(c[0], CHECK_EVERY)
            found, done = check(st)
            return st, found, done

        _, (vk, n_ge, n_gt), _ = lax.while_loop(w_cond, w_body, (state, (mn, n_valid, n_valid), jnp.float32(0.0)))
        need = kf - n_gt
        has_tie = jnp.max(n_ge) > kf + 0.5

        @pl.when(jnp.logical_not(has_tie))
        def _():
            vsel_ref[i % 2] = vk

        @pl.when(has_tie)
        def _():
            vsel_ref[i % 2] = jnp.full((1, Q_TILE), -0.5, F32)
            r_i = lax.broadcasted_iota(jnp.int32, (KEY_TILE, KEY_TILE), 0)
            c_i = lax.broadcasted_iota(jnp.int32, (KEY_TILE, KEY_TILE), 1)
            tri = jnp.where(r_i >= c_i, 1.0, 0.0).astype(BF16)

            def body(j, seen):
                sl = pl.ds(pl.multiple_of(j * KEY_TILE, KEY_TILE), KEY_TILE)
                s = sc_i[sl, :]
                eq = s == vk
                rank = jnp.dot(tri, jnp.where(eq, 1.0, 0.0).astype(BF16), preferred_element_type=F32) + seen
                sel = (s > vk) | (eq & (rank <= need))
                sc_i[sl, :] = jnp.where(sel, 0.0, NEG_INF)
                return rank[KEY_TILE - 1:KEY_TILE, :]
            lax.fori_loop(0, nkt, body, jnp.zeros((1, Q_TILE), F32))

    @pl.when(t >= 1)
    def _():
        l_fast = l_ref[...]
        in_range = (jnp.min(l_fast) > 1e-20) & (jnp.max(l_fast) < 1e30)

        @pl.when(jnp.logical_not(in_range))
        def _():
            def max_body(j, m8):
                sl = pl.ds(pl.multiple_of(j * KEY_TILE, KEY_TILE), KEY_TILE)
                mask = key_mask(sl)
                parts = []
                for g in range(N_KV_HEADS):
                    a = jnp.dot(k_ref[sl, g * HEAD_DIM:(g + 1) * HEAD_DIM], qT[:, g * gw:(g + 1) * gw],
                                preferred_element_type=F32)
                    a = a + jnp.concatenate([mask] * HEADS_PER_KV, axis=1)
                    parts.append(jnp.max(a.reshape(KEY_TILE // 8, 8, gw), axis=0))
                return jnp.maximum(m8, jnp.concatenate(parts, axis=1))
            m8 = lax.fori_loop(0, nka, max_body, jnp.full((8, hw), -jnp.inf, F32))
            m = jnp.max(m8, axis=0, keepdims=True)
            acc_ref[...] = jnp.zeros_like(acc_ref)
            l8 = lax.fori_loop(0, nka, lambda j, c: attend_tile(j, qk_tile(j), c, m), jnp.zeros((8, hw), F32))
            l_ref[...] = jnp.sum(l8, axis=0, keepdims=True)

        oT = acc_ref[...] / l_ref[...]
        for hd in range(N_HEADS):
            y_ref[:, hd * HEAD_DIM:(hd + 1) * HEAD_DIM] = oT[:, hd * Q_TILE:(hd + 1) * Q_TILE].T.astype(BF16)


def _attn_call(qT, qiT, wiT, k, ki, vT):
    nq = SEQ // Q_TILE
    scored = lambda b, t: b * nq + jnp.minimum(t, nq - 1)
    attended = lambda b, t: b * nq + jnp.maximum(t - 1, 0)
    return pl.pallas_call(
        _attn_kernel,
        grid=(BATCH, nq + 1),
        in_specs=[
            pl.BlockSpec((1, HEAD_DIM, N_HEADS * Q_TILE), lambda b, t: (attended(b, t), 0, 0)),
            pl.BlockSpec((1, IDX_DIM, IDX_HEADS * Q_TILE), lambda b, t: (scored(b, t), 0, 0)),
            pl.BlockSpec((IDX_HEADS, Q_TILE), lambda b, t: (0, scored(b, t))),
            pl.BlockSpec((SEQ, N_KV_HEADS * HEAD_DIM), lambda b, t: (b, 0)),
            pl.BlockSpec((SEQ, 128), lambda b, t: (b, 0)),
            pl.BlockSpec((N_KV_HEADS * HEAD_DIM, SEQ), lambda b, t: (0, b)),
        ],
        out_specs=pl.BlockSpec((Q_TILE, N_HEADS * HEAD_DIM), lambda b, t: (attended(b, t), 0)),
        out_shape=jax.ShapeDtypeStruct((TOKENS, N_HEADS * HEAD_DIM), BF16),
        scratch_shapes=[pltpu.VMEM((2, SEQ, Q_TILE), F32),
                        pltpu.VMEM((HEAD_DIM, N_HEADS * Q_TILE), F32),
                        pltpu.VMEM((1, N_HEADS * Q_TILE), F32),
                        pltpu.VMEM((2, 1, Q_TILE), F32)],
        compiler_params=pltpu.CompilerParams(dimension_semantics=("arbitrary", "arbitrary"),
                                             vmem_limit_bytes=VMEM_LIMIT),
        name="dsa_attention",
    )(qT, qiT, wiT, k, ki, vT)


def _post_kernel(x_ref, mod_ref, u_ref, vln_ref, sga_ref, sgb_ref, yb_ref, ws_ref, bsT_ref,
                 wpa_ref, wpb_ref, wout_ref, n2g_ref, wfi_ref, wfo_ref, fg_ref, o_ref, ya_ref):
    tm = x_ref.shape[0]
    mod = mod_ref[0]
    gate1 = mod[:, 2 * D_MODEL:3 * D_MODEL]
    shift2 = mod[:, 3 * D_MODEL:4 * D_MODEL]
    scale2 = mod[:, 4 * D_MODEL:5 * D_MODEL]
    gate2 = mod[:, 5 * D_MODEL:6 * D_MODEL]

    r_i = lax.broadcasted_iota(jnp.int32, (CHUNK, CHUNK), 0)
    c_i = lax.broadcasted_iota(jnp.int32, (CHUNK, CHUNK), 1)
    bsT = bsT_ref[...]
    gd = A_WIDTH // A_GROUPS
    for g in range(A_GROUPS):
        wm = jnp.where(r_i >= c_i, ws_ref[g], 0.0).astype(BF16)
        bias = bsT[:, g:g + 1]
        for c in range(tm // CHUNK):
            rows = slice(c * CHUNK, (c + 1) * CHUNK)
            cols = slice(g * gd, (g + 1) * gd)
            mixed = jnp.dot(wm, vln_ref[rows, cols], preferred_element_type=F32) + bias
            ya_ref[rows, cols] = (u_ref[rows, cols].astype(F32) * mixed).astype(BF16)

    pa = jnp.dot(ya_ref[...], wpa_ref[...], preferred_element_type=F32)
    pb = jnp.dot(yb_ref[...], wpb_ref[...], preferred_element_type=F32)
    merged = sga_ref[...].astype(F32) * pa + sgb_ref[...].astype(F32) * pb
    x1 = x_ref[...] + gate1 * jnp.dot(merged.astype(BF16), wout_ref[...], preferred_element_type=F32)

    h2 = (_rms_norm(x1, n2g_ref[...]) * (1.0 + scale2) + shift2).astype(BF16)
    fg = jnp.dot(h2, wfi_ref[:, 0:D_FF], preferred_element_type=F32)
    fu = jnp.dot(h2, wfi_ref[:, D_FF:], preferred_element_type=F32)
    act = (fg * _sigmoid(fg) * fu).astype(BF16)
    x2 = x1 + gate2 * jnp.dot(act, wfo_ref[...], preferred_element_type=F32)
    o_ref[...] = _rms_norm(x2, fg_ref[...])


def _post_call(x2, mod3, u, vln, sga, sgb, yb, ws, bsT, wpa, wpb, wout, n2g, wfi, wfo, fg):
    tm = TM_OUT
    per_b = SEQ // tm
    row = lambda t: (t, 0)
    const2 = lambda t: (0, 0)
    tok = lambda: pl.BlockSpec((tm, D_MODEL), row)
    in_specs = [
        tok(),
        pl.BlockSpec((1, 1, 6 * D_MODEL), lambda t: (t // per_b, 0, 0)),
        tok(), tok(), tok(), tok(), tok(),
        pl.BlockSpec(ws.shape, lambda t: (0, 0, 0)),
        pl.BlockSpec(bsT.shape, const2),
        pl.BlockSpec(wpa.shape, const2),
        pl.BlockSpec(wpb.shape, const2),
        pl.BlockSpec(wout.shape, const2),
        pl.BlockSpec((1, D_MODEL), const2),
        pl.BlockSpec(wfi.shape, const2),
        pl.BlockSpec(wfo.shape, const2),
        pl.BlockSpec((1, D_MODEL), const2),
    ]
    return pl.pallas_call(
        _post_kernel,
        grid=(TOKENS // tm,),
        in_specs=in_specs,
        out_specs=pl.BlockSpec((tm, D_MODEL), row),
        out_shape=jax.ShapeDtypeStruct((TOKENS, D_MODEL), F32),
        scratch_shapes=[pltpu.VMEM((tm, A_WIDTH), BF16)],
        compiler_params=pltpu.CompilerParams(dimension_semantics=("arbitrary",),
                                             vmem_limit_bytes=VMEM_LIMIT),
        name="merge_ffn",
    )(x2, mod3, u, vln, sga, sgb, yb, ws, bsT, wpa, wpb, wout, n2g, wfi, wfo, fg)


def _inv_freq_table():
    fq = ROPE_THETA ** (-np.arange(0, ROT_DIM, 2, dtype=np.float32) / ROT_DIM)
    fi = ROPE_THETA ** (-np.arange(0, IDX_ROT_DIM, 2, dtype=np.float32) / IDX_ROT_DIM)
    return np.concatenate([fq, fi]).astype(np.float32).reshape(-1, 1)


def kernel(x, c, positions, w_ada, b_ada, norm1_g, w_in, gmlp_ln_g, gmlp_ln_b, gmlp_w_s, gmlp_b_s,
           idx_k_ln_g, idx_k_ln_b, w_proj_a, w_proj_b, w_out, norm2_g, w_ffn_in, w_ffn_out, final_norm_g):
    assert x.shape == (BATCH, SEQ, D_MODEL) and w_in.shape == (1, D_MODEL, _IN_COLS)
    x2 = x.reshape(TOKENS, D_MODEL)
    pos3 = positions.reshape(BATCH, 1, SEQ)
    invf = jnp.asarray(_inv_freq_table())
    xcur = x2
    for l in range(w_ada.shape[0]):
        w = w_in[l]
        wn = jnp.concatenate([w[:, _OFF_U:_OFF_Q], w[:, _OFF_GA:]], axis=1).astype(BF16)
        wt = w[:, _OFF_Q:_OFF_GA].astype(BF16).T
        mod = _ada_call(c, w_ada[l], b_ada[l])
        mod3 = mod.reshape(BATCH, 1, 6 * D_MODEL)
        u, vln, sga, sgb, qT, k, vT, qiT, ki, wiT = _inproj_call(
            xcur, mod3, pos3, norm1_g[l].reshape(1, -1), wn, wt,
            gmlp_ln_g[l].reshape(1, -1), gmlp_ln_b[l].reshape(1, -1),
            idx_k_ln_g[l].reshape(-1, 1), idx_k_ln_b[l].reshape(-1, 1), invf)
        yb = _attn_call(qT, qiT, wiT, k, ki, vT)
        last = l == w_ada.shape[0] - 1
        fg = final_norm_g.reshape(1, -1) if last else None
        assert last, "single-layer block"
        xcur = _post_call(xcur, mod3, u, vln, sga, sgb, yb, gmlp_w_s[l], gmlp_b_s[l].T,
                          w_proj_a[l].astype(BF16), w_proj_b[l].astype(BF16), w_out[l].astype(BF16),
                          norm2_g[l].reshape(1, -1), w_ffn_in[l].astype(BF16), w_ffn_out[l].astype(BF16), fg)
    return xcur.reshape(BATCH, SEQ, D_MODEL)
```

```python
import functools

import numpy as np
import jax
import jax.numpy as jnp
from jax import lax
from jax.experimental import pallas as pl
from jax.experimental.pallas import tpu as pltpu

D_MODEL = 1024
BATCH = 4
SEQ = 4096
CHUNK = 128
A_GROUPS = 8
A_WIDTH = 1024
N_HEADS = 8
N_KV_HEADS = 2
HEAD_DIM = 128
IDX_HEADS = 8
IDX_DIM = 64
TOPK = 256
ROPE_THETA = 500000.0
ROT_DIM = HEAD_DIM // 4
IDX_ROT_DIM = IDX_DIM // 4
D_FF = 2816
EPS = 1e-6
NEG_INF = -1e30

TOKENS = BATCH * SEQ
Q_TILE = 128
KEY_TILE = 512
ACC_ROWS = 32
FIRST_CHECK_AFTER = 16
CHECK_EVERY = 2
SWEEP_UNROLL = (4, 2, 1)
TM_IN = 512
TM_OUT = 256
HEADS_PER_KV = N_HEADS // N_KV_HEADS
VMEM_LIMIT = 56 * 1024 * 1024

_OFF_U = 0
_OFF_V = _OFF_U + A_WIDTH
_OFF_Q = _OFF_V + A_WIDTH
_OFF_K = _OFF_Q + N_HEADS * HEAD_DIM
_OFF_VV = _OFF_K + N_KV_HEADS * HEAD_DIM
_OFF_QI = _OFF_VV + N_KV_HEADS * HEAD_DIM
_OFF_KI = _OFF_QI + IDX_HEADS * IDX_DIM
_OFF_WI = _OFF_KI + IDX_DIM
_OFF_GA = _OFF_WI + IDX_HEADS
_OFF_GB = _OFF_GA + D_MODEL
_IN_COLS = _OFF_GB + D_MODEL

_T_Q = 0
_T_K = _T_Q + N_HEADS * HEAD_DIM
_T_V = _T_K + N_KV_HEADS * HEAD_DIM
_T_QI = _T_V + N_KV_HEADS * HEAD_DIM
_T_KI = _T_QI + IDX_HEADS * IDX_DIM
_T_WI = _T_KI + IDX_DIM
_T_ROWS = _T_WI + IDX_HEADS

F32 = jnp.float32
BF16 = jnp.bfloat16


def _gelu_tanh(x):
    return 0.5 * x * (1.0 + jnp.tanh(np.sqrt(2.0 / np.pi).astype(np.float32) * (x + 0.044715 * (x * x * x))))


def _sigmoid(x):
    return 1.0 / (1.0 + jnp.exp(-x))


def _rms_norm(x, g):
    return x * lax.rsqrt(jnp.mean(x * x, axis=-1, keepdims=True) + EPS) * g


def _ada_kernel(c_ref, w_ref, b_ref, o_ref):
    c = c_ref[...]
    a = c * _sigmoid(c)
    w = w_ref[...]
    a_hi = a.astype(BF16)
    a_lo = (a - a_hi.astype(F32)).astype(BF16)
    w_hi = w.astype(BF16)
    w_lo = (w - w_hi.astype(F32)).astype(BF16)
    dot = functools.partial(jnp.dot, preferred_element_type=F32)
    o_ref[...] = dot(a_hi, w_hi) + (dot(a_hi, w_lo) + dot(a_lo, w_hi)) + b_ref[...]


def _ada_call(c, w_ada, b_ada):
    n_out = 6 * D_MODEL
    tn = 1024
    return pl.pallas_call(
        _ada_kernel,
        grid=(n_out // tn,),
        in_specs=[pl.BlockSpec((BATCH, D_MODEL), lambda j: (0, 0)),
                  pl.BlockSpec((D_MODEL, tn), lambda j: (0, j)),
                  pl.BlockSpec((1, tn), lambda j: (0, j))],
        out_specs=pl.BlockSpec((BATCH, tn), lambda j: (0, j)),
        out_shape=jax.ShapeDtypeStruct((BATCH, n_out), F32),
        compiler_params=pltpu.CompilerParams(dimension_semantics=("arbitrary",),
                                             vmem_limit_bytes=VMEM_LIMIT),
        name="ada_mod",
    )(c, w_ada, b_ada.reshape(1, n_out))


def _rope_rows(blk, cos, sin, half):
    x1 = blk[0:half]
    x2 = blk[half:2 * half]
    return x1 * cos - x2 * sin, x2 * cos + x1 * sin


def _inproj_kernel(x_ref, mod_ref, pos_ref, n1g_ref, wn_ref, wt_ref, lng_ref, lnb_ref,
                   kig_ref, kib_ref, invf_ref,
                   u_ref, vln_ref, sga_ref, sgb_ref, qT_ref, k_ref, vT_ref, qiT_ref, ki_ref, wiT_ref):
    tm = x_ref.shape[0]
    x = x_ref[...]
    mod = mod_ref[0]
    shift1 = mod[:, 0:D_MODEL]
    scale1 = mod[:, D_MODEL:2 * D_MODEL]
    h = _rms_norm(x, n1g_ref[...]) * (1.0 + scale1) + shift1
    hb = h.astype(BF16)

    zu = jnp.dot(hb, wn_ref[:, 0:A_WIDTH], preferred_element_type=F32)
    u_ref[...] = _gelu_tanh(zu).astype(BF16)
    zv = _gelu_tanh(jnp.dot(hb, wn_ref[:, A_WIDTH:2 * A_WIDTH], preferred_element_type=F32))
    mu = jnp.mean(zv, axis=-1, keepdims=True)
    zc = zv - mu
    var = jnp.mean(zc * zc, axis=-1, keepdims=True)
    vln_ref[...] = (zc * lax.rsqrt(var + EPS) * lng_ref[...] + lnb_ref[...]).astype(BF16)
    zga = jnp.dot(hb, wn_ref[:, 2 * A_WIDTH:2 * A_WIDTH + D_MODEL], preferred_element_type=F32)
    sga_ref[...] = _sigmoid(zga).astype(BF16)
    zgb = jnp.dot(hb, wn_ref[:, 2 * A_WIDTH + D_MODEL:], preferred_element_type=F32)
    sgb_ref[...] = _sigmoid(zgb).astype(BF16)

    nt = (((1,), (1,)), ((), ()))
    pos = pos_ref[0].astype(F32)
    ang = invf_ref[...] * pos
    cos = jnp.cos(ang)
    sin = jnp.sin(ang)
    hq = ROT_DIM // 2
    hi = IDX_ROT_DIM // 2
    cos_q, sin_q = cos[0:hq], sin[0:hq]
    cos_i, sin_i = cos[hq:hq + hi], sin[hq:hq + hi]
    n_sub = tm // Q_TILE

    zq = lax.dot_general(wt_ref[_T_Q:_T_K, :], hb, nt, preferred_element_type=F32)
    q_scale = HEAD_DIM ** -0.5 * float(np.log2(np.e))
    for hd in range(N_HEADS):
        blk = zq[hd * HEAD_DIM:(hd + 1) * HEAD_DIM]
        r1, r2 = _rope_rows(blk, cos_q, sin_q, hq)
        full = (jnp.concatenate([r1, r2, blk[ROT_DIM:]], axis=0) * q_scale).astype(BF16)
        for s in range(n_sub):
            qT_ref[s, :, hd * Q_TILE:(hd + 1) * Q_TILE] = full[:, s * Q_TILE:(s + 1) * Q_TILE]

    zk = lax.dot_general(wt_ref[_T_K:_T_V, :], hb, nt, preferred_element_type=F32)
    k_rows = []
    for g in range(N_KV_HEADS):
        blk = zk[g * HEAD_DIM:(g + 1) * HEAD_DIM]
        r1, r2 = _rope_rows(blk, cos_q, sin_q, hq)
        k_rows += [r1, r2, blk[ROT_DIM:]]
    k_ref[...] = jnp.concatenate(k_rows, axis=0).T.astype(BF16)

    zvv = lax.dot_general(wt_ref[_T_V:_T_QI, :], hb, nt, preferred_element_type=F32)
    vT_ref[...] = zvv.astype(BF16)

    zqi = lax.dot_general(wt_ref[_T_QI:_T_KI, :], hb, nt, preferred_element_type=F32)
    for hd in range(IDX_HEADS):
        blk = zqi[hd * IDX_DIM:(hd + 1) * IDX_DIM]
        r1, r2 = _rope_rows(blk, cos_i, sin_i, hi)
        full = jnp.concatenate([r1, r2, blk[IDX_ROT_DIM:]], axis=0).astype(BF16)
        for s in range(n_sub):
            qiT_ref[s, :, hd * Q_TILE:(hd + 1) * Q_TILE] = full[:, s * Q_TILE:(s + 1) * Q_TILE]

    zrest = lax.dot_general(wt_ref[_T_KI:_T_ROWS, :], hb, nt, preferred_element_type=F32)
    zki = zrest[0:IDX_DIM]
    kmu = jnp.mean(zki, axis=0, keepdims=True)
    kc = zki - kmu
    kvar = jnp.mean(kc * kc, axis=0, keepdims=True)
    kin = kc * lax.rsqrt(kvar + EPS) * kig_ref[...] + kib_ref[...]
    r1, r2 = _rope_rows(kin, cos_i, sin_i, hi)
    ki_full = jnp.concatenate([r1, r2, kin[IDX_ROT_DIM:], jnp.zeros((128 - IDX_DIM, tm), F32)], axis=0)
    ki_ref[...] = ki_full.T.astype(BF16)
    wiT_ref[...] = zrest[IDX_DIM:IDX_DIM + IDX_HEADS] * ((IDX_HEADS ** -0.5) * (IDX_DIM ** -0.5))


def _inproj_call(x2, mod3, pos3, n1g, wn, wt, lng, lnb, kig, kib, invf):
    tm = TM_IN
    n_tiles = TOKENS // tm
    per_b = SEQ // tm
    n_sub = tm // Q_TILE
    const2 = lambda t: (0, 0)
    row = lambda t: (t, 0)
    in_specs = [
        pl.BlockSpec((tm, D_MODEL), row),
        pl.BlockSpec((1, 1, 6 * D_MODEL), lambda t: (t // per_b, 0, 0)),
        pl.BlockSpec((1, 1, tm), lambda t: (t // per_b, 0, t % per_b)),
        pl.BlockSpec((1, D_MODEL), const2),
        pl.BlockSpec(wn.shape, const2),
        pl.BlockSpec(wt.shape, const2),
        pl.BlockSpec((1, A_WIDTH), const2),
        pl.BlockSpec((1, A_WIDTH), const2),
        pl.BlockSpec((IDX_DIM, 1), const2),
        pl.BlockSpec((IDX_DIM, 1), const2),
        pl.BlockSpec(invf.shape, const2),
    ]
    out_shape = [
        jax.ShapeDtypeStruct((TOKENS, A_WIDTH), BF16),
        jax.ShapeDtypeStruct((TOKENS, A_WIDTH), BF16),
        jax.ShapeDtypeStruct((TOKENS, D_MODEL), BF16),
        jax.ShapeDtypeStruct((TOKENS, D_MODEL), BF16),
        jax.ShapeDtypeStruct((TOKENS // Q_TILE, HEAD_DIM, N_HEADS * Q_TILE), BF16),
        jax.ShapeDtypeStruct((TOKENS, N_KV_HEADS * HEAD_DIM), BF16),
        jax.ShapeDtypeStruct((N_KV_HEADS * HEAD_DIM, TOKENS), BF16),
        jax.ShapeDtypeStruct((TOKENS // Q_TILE, IDX_DIM, IDX_HEADS * Q_TILE), BF16),
        jax.ShapeDtypeStruct((TOKENS, 128), BF16),
        jax.ShapeDtypeStruct((IDX_HEADS, TOKENS), F32),
    ]
    out_specs = [
        pl.BlockSpec((tm, A_WIDTH), row),
        pl.BlockSpec((tm, A_WIDTH), row),
        pl.BlockSpec((tm, D_MODEL), row),
        pl.BlockSpec((tm, D_MODEL), row),
        pl.BlockSpec((n_sub, HEAD_DIM, N_HEADS * Q_TILE), lambda t: (t, 0, 0)),
        pl.BlockSpec((tm, N_KV_HEADS * HEAD_DIM), row),
        pl.BlockSpec((N_KV_HEADS * HEAD_DIM, tm), lambda t: (0, t)),
        pl.BlockSpec((n_sub, IDX_DIM, IDX_HEADS * Q_TILE), lambda t: (t, 0, 0)),
        pl.BlockSpec((tm, 128), row),
        pl.BlockSpec((IDX_HEADS, tm), lambda t: (0, t)),
    ]
    return pl.pallas_call(
        _inproj_kernel,
        grid=(n_tiles,),
        in_specs=in_specs,
        out_specs=out_specs,
        out_shape=out_shape,
        compiler_params=pltpu.CompilerParams(dimension_semantics=("arbitrary",),
                                             vmem_limit_bytes=VMEM_LIMIT),
        name="in_proj",
    )(x2, mod3, pos3, n1g, wn, wt, lng, lnb, kig, kib, invf)


def _col_reduce(x, op):
    return op(x.reshape(x.shape[0] // ACC_ROWS, ACC_ROWS, x.shape[1]), axis=0)


def _key_tiles(tile):
    return (tile * Q_TILE + Q_TILE + KEY_TILE - 1) // KEY_TILE


def _attn_kernel(qT_ref, qiT_ref, wiT_ref, k_ref, ki_ref, vT_ref, y_ref,
                 sc_ref, acc_ref, l_ref, vsel_ref):
    t = pl.program_id(1)
    nq = SEQ // Q_TILE
    i = jnp.minimum(t, nq - 1)
    ia = jnp.maximum(t - 1, 0)
    sc_i = sc_ref.at[i % 2]
    sc_a = sc_ref.at[ia % 2]
    nkt = jnp.where(t < nq, _key_tiles(i), 0)
    nka = jnp.where(t >= 1, _key_tiles(ia), 0)
    kf = float(TOPK)
    gw = HEADS_PER_KV * Q_TILE
    hw = N_HEADS * Q_TILE

    @pl.when(t == 0)
    def _():
        vsel_ref[...] = jnp.zeros_like(vsel_ref)

    qiT = qiT_ref[0]
    qT = qT_ref[0]
    w = wiT_ref[...]
    wrow = jnp.concatenate([w[hd:hd + 1, :] for hd in range(IDX_HEADS)], axis=1)
    vsel_a = vsel_ref[ia % 2]

    row_i = lax.broadcasted_iota(jnp.int32, (KEY_TILE, Q_TILE), 0)
    qidx = i * Q_TILE + lax.broadcasted_iota(jnp.int32, (KEY_TILE, Q_TILE), 1)

    def score_tile(j, mx_a, mn_a):
        sl = pl.ds(pl.multiple_of(j * KEY_TILE, KEY_TILE), KEY_TILE)
        lg = jnp.dot(ki_ref[sl, 0:IDX_DIM], qiT, preferred_element_type=F32)
        r = jnp.maximum(lg, 0.0) * wrow
        s = r[:, 0:Q_TILE]
        for hd in range(1, IDX_HEADS):
            s = s + r[:, hd * Q_TILE:(hd + 1) * Q_TILE]
        causal = j * KEY_TILE + row_i <= qidx
        sc_i[sl, :] = jnp.where(causal, s, -jnp.inf)
        mx_a = jnp.maximum(mx_a, _col_reduce(jnp.where(causal, s, -jnp.inf), jnp.max))
        mn_a = jnp.minimum(mn_a, _col_reduce(jnp.where(causal, s, jnp.inf), jnp.min))
        return mx_a, mn_a

    def key_mask(sl):
        return jnp.where(sc_a[sl, :] >= vsel_a, 0.0, NEG_INF)

    def qk_tile(j):
        sl = pl.ds(pl.multiple_of(j * KEY_TILE, KEY_TILE), KEY_TILE)
        return [jnp.dot(k_ref[sl, g * HEAD_DIM:(g + 1) * HEAD_DIM], qT[:, g * gw:(g + 1) * gw],
                        preferred_element_type=F32) for g in range(N_KV_HEADS)]

    def attend_tile(j, qk, l8, m):
        sl = pl.ds(pl.multiple_of(j * KEY_TILE, KEY_TILE), KEY_TILE)
        mask = key_mask(sl)
        parts = []
        for g in range(N_KV_HEADS):
            a = qk[g]
            ps = []
            for r in range(HEADS_PER_KV):
                e = a[:, r * Q_TILE:(r + 1) * Q_TILE] + mask
                if m is not None:
                    c0 = (g * HEADS_PER_KV + r) * Q_TILE
                    e = e - m[:, c0:c0 + Q_TILE]
                ps.append(jnp.exp2(e))
            p = jnp.concatenate(ps, axis=1)
            acc_ref[:, g * gw:(g + 1) * gw] += jnp.dot(
                vT_ref[g * HEAD_DIM:(g + 1) * HEAD_DIM, sl], p.astype(BF16), preferred_element_type=F32)
            parts.append(jnp.sum(p.reshape(KEY_TILE // 8, 8, gw), axis=0))
        return l8 + jnp.concatenate(parts, axis=1)

    acc_ref[...] = jnp.zeros_like(acc_ref)
    n_both = jnp.minimum(nkt, nka)

    def both_body(j, c):
        mx_a, mn_a, l8 = c
        mx_a, mn_a = score_tile(j, mx_a, mn_a)
        return mx_a, mn_a, attend_tile(j, qk_tile(j), l8, None)

    def score_body(j, c):
        mx_a, mn_a, l8 = c
        mx_a, mn_a = score_tile(j, mx_a, mn_a)
        return mx_a, mn_a, l8

    def attend_body(j, c):
        mx_a, mn_a, l8 = c
        return mx_a, mn_a, attend_tile(j, qk_tile(j), l8, None)

    carry = (jnp.full((ACC_ROWS, Q_TILE), -jnp.inf, F32), jnp.full((ACC_ROWS, Q_TILE), jnp.inf, F32),
             jnp.zeros((8, hw), F32))
    done = 0
    for width in SWEEP_UNROLL:
        def wide_body(jw, c, width=width, done=done):
            for u in range(width):
                c = both_body(done + jw * width + u, c)
            return c
        trips = (n_both - done) // width
        carry = lax.fori_loop(0, trips, wide_body, carry)
        done = done + trips * width
    carry = lax.fori_loop(n_both, nkt, score_body, carry)
    mx_a, mn_a, l8 = lax.fori_loop(n_both, nka, attend_body, carry)
    l_ref[...] = jnp.sum(l8, axis=0, keepdims=True)

    def tiles(fn, init):
        def body(j, carry):
            s = sc_i[pl.ds(pl.multiple_of(j * KEY_TILE, KEY_TILE), KEY_TILE), :]
            return fn(s, carry)
        return lax.fori_loop(0, nkt, body, init)

    def count_ge(t):
        c8 = tiles(lambda s, c: c + _col_reduce(jnp.where(s >= t, 1.0, 0.0), jnp.sum),
                   jnp.zeros((ACC_ROWS, Q_TILE), F32))
        return jnp.sum(c8, axis=0, keepdims=True)

    def bisect(state, n):
        def step(_, c):
            lo, hi, n_lo, n_hi = c
            mid = 0.5 * lo + 0.5 * hi
            n_mid = count_ge(mid)
            ok = n_mid >= kf
            return (jnp.where(ok, mid, lo), jnp.where(ok, hi, mid),
                    jnp.where(ok, n_mid, n_lo), jnp.where(ok, n_hi, n_mid))
        return lax.fori_loop(0, n, step, state)

    @pl.when(t < TOPK // Q_TILE)
    def _():
        vsel_ref[i % 2] = jnp.full((1, Q_TILE), jnp.finfo(jnp.float32).min, F32)

    @pl.when((t >= TOPK // Q_TILE) & (t < nq))
    def _():
        mx = jnp.max(mx_a, axis=0, keepdims=True)
        mn = jnp.min(mn_a, axis=0, keepdims=True)
        n_valid = (qidx[0:1, :] + 1).astype(F32)
        hi0 = mx + jnp.maximum(jnp.abs(mx), 1e-30) * 1e-6
        state = bisect((mn, hi0, n_valid, jnp.zeros((1, Q_TILE), F32)), FIRST_CHECK_AFTER - CHECK_EVERY)

        def check(state):
            lo, hi, n_lo, n_hi = state

            def f(s, c):
                a8, b8 = c
                a8 = jnp.maximum(a8, _col_reduce(jnp.where(s < hi, s, -jnp.inf), jnp.max))
                b8 = jnp.minimum(b8, _col_reduce(jnp.where(s >= lo, s, jnp.inf), jnp.min))
                return a8, b8
            a8, b8 = tiles(f, (jnp.full((ACC_ROWS, Q_TILE), -jnp.inf, F32), jnp.full((ACC_ROWS, Q_TILE), jnp.inf, F32)))
            top = jnp.max(a8, axis=0, keepdims=True)
            bottom = jnp.min(b8, axis=0, keepdims=True)
            single = top == bottom
            pinned = single | (n_lo - n_hi < 2.5)
            take_top = jnp.logical_not(single) & (kf - n_hi < 1.5)
            take_bottom = jnp.logical_not(single | take_top)
            v = jnp.where(take_top, top, bottom)
            n_ge = jnp.where(take_top, n_hi + 1.0, n_lo)
            n_gt = jnp.where(take_bottom, n_hi + 1.0, n_hi)
            return (v, n_ge, n_gt), jnp.min(jnp.where(pinned, 1.0, 0.0))

        def w_cond(c):
            return c[2] < 0.5

        def w_body(c):
            st = bisect(c[0], CHECK_EVERY)
            found, done = check(st)
            return st, found, done

        _, (vk, n_ge, n_gt), _ = lax.while_loop(w_cond, w_body, (state, (mn, n_valid, n_valid), jnp.float32(0.0)))
        need = kf - n_gt
        has_tie = jnp.max(n_ge) > kf + 0.5

        @pl.when(jnp.logical_not(has_tie))
        def _():
            vsel_ref[i % 2] = vk

        @pl.when(has_tie)
        def _():
            vsel_ref[i % 2] = jnp.full((1, Q_TILE), -0.5, F32)
            r_i = lax.broadcasted_iota(jnp.int32, (KEY_TILE, KEY_TILE), 0)
            c_i = lax.broadcasted_iota(jnp.int32, (KEY_TILE, KEY_TILE), 1)
            tri = jnp.where(r_i >= c_i, 1.0, 0.0).astype(BF16)

            def body(j, seen):
                sl = pl.ds(pl.multiple_of(j * KEY_TILE, KEY_TILE), KEY_TILE)
                s = sc_i[sl, :]
                eq = s == vk
                rank = jnp.dot(tri, jnp.where(eq, 1.0, 0.0).astype(BF16), preferred_element_type=F32) + seen
                sel = (s > vk) | (eq & (rank <= need))
                sc_i[sl, :] = jnp.where(sel, 0.0, NEG_INF)
                return rank[KEY_TILE - 1:KEY_TILE, :]
            lax.fori_loop(0, nkt, body, jnp.zeros((1, Q_TILE), F32))

    @pl.when(t >= 1)
    def _():
        l_fast = l_ref[...]
        in_range = (jnp.min(l_fast) > 1e-20) & (jnp.max(l_fast) < 1e30)

        @pl.when(jnp.logical_not(in_range))
        def _():
            def max_body(j, m8):
                sl = pl.ds(pl.multiple_of(j * KEY_TILE, KEY_TILE), KEY_TILE)
                mask = key_mask(sl)
                parts = []
                for g in range(N_KV_HEADS):
                    a = jnp.dot(k_ref[sl, g * HEAD_DIM:(g + 1) * HEAD_DIM], qT[:, g * gw:(g + 1) * gw],
                                preferred_element_type=F32)
                    a = a + jnp.concatenate([mask] * HEADS_PER_KV, axis=1)
                    parts.append(jnp.max(a.reshape(KEY_TILE // 8, 8, gw), axis=0))
                return jnp.maximum(m8, jnp.concatenate(parts, axis=1))
            m8 = lax.fori_loop(0, nka, max_body, jnp.full((8, hw), -jnp.inf, F32))
            m = jnp.max(m8, axis=0, keepdims=True)
            acc_ref[...] = jnp.zeros_like(acc_ref)
            l8 = lax.fori_loop(0, nka, lambda j, c: attend_tile(j, qk_tile(j), c, m), jnp.zeros((8, hw), F32))
            l_ref[...] = jnp.sum(l8, axis=0, keepdims=True)

        oT = acc_ref[...] / l_ref[...]
        for hd in range(N_HEADS):
            y_ref[:, hd * HEAD_DIM:(hd + 1) * HEAD_DIM] = oT[:, hd * Q_TILE:(hd + 1) * Q_TILE].T.astype(BF16)


def _attn_call(qT, qiT, wiT, k, ki, vT):
    nq = SEQ // Q_TILE
    scored = lambda b, t: b * nq + jnp.minimum(t, nq - 1)
    attended = lambda b, t: b * nq + jnp.maximum(t - 1, 0)
    return pl.pallas_call(
        _attn_kernel,
        grid=(BATCH, nq + 1),
        in_specs=[
            pl.BlockSpec((1, HEAD_DIM, N_HEADS * Q_TILE), lambda b, t: (attended(b, t), 0, 0)),
            pl.BlockSpec((1, IDX_DIM, IDX_HEADS * Q_TILE), lambda b, t: (scored(b, t), 0, 0)),
            pl.BlockSpec((IDX_HEADS, Q_TILE), lambda b, t: (0, scored(b, t))),
            pl.BlockSpec((SEQ, N_KV_HEADS * HEAD_DIM), lambda b, t: (b, 0)),
            pl.BlockSpec((SEQ, 128), lambda b, t: (b, 0)),
            pl.BlockSpec((N_KV_HEADS * HEAD_DIM, SEQ), lambda b, t: (0, b)),
        ],
        out_specs=pl.BlockSpec((Q_TILE, N_HEADS * HEAD_DIM), lambda b, t: (attended(b, t), 0)),
        out_shape=jax.ShapeDtypeStruct((TOKENS, N_HEADS * HEAD_DIM), BF16),
        scratch_shapes=[pltpu.VMEM((2, SEQ, Q_TILE), F32),
                        pltpu.VMEM((HEAD_DIM, N_HEADS * Q_TILE), F32),
                        pltpu.VMEM((1, N_HEADS * Q_TILE), F32),
                        pltpu.VMEM((2, 1, Q_TILE), F32)],
        compiler_params=pltpu.CompilerParams(dimension_semantics=("arbitrary", "arbitrary"),
                                             vmem_limit_bytes=VMEM_LIMIT),
        name="dsa_attention",
    )(qT, qiT, wiT, k, ki, vT)


def _post_kernel(x_ref, mod_ref, u_ref, vln_ref, sga_ref, sgb_ref, yb_ref, ws_ref, bsT_ref,
                 wpa_ref, wpb_ref, wout_ref, n2g_ref, wfi_ref, wfo_ref, fg_ref, o_ref, ya_ref):
    tm = x_ref.shape[0]
    mod = mod_ref[0]
    gate1 = mod[:, 2 * D_MODEL:3 * D_MODEL]
    shift2 = mod[:, 3 * D_MODEL:4 * D_MODEL]
    scale2 = mod[:, 4 * D_MODEL:5 * D_MODEL]
    gate2 = mod[:, 5 * D_MODEL:6 * D_MODEL]

    r_i = lax.broadcasted_iota(jnp.int32, (CHUNK, CHUNK), 0)
    c_i = lax.broadcasted_iota(jnp.int32, (CHUNK, CHUNK), 1)
    bsT = bsT_ref[...]
    gd = A_WIDTH // A_GROUPS
    for g in range(A_GROUPS):
        wm = jnp.where(r_i >= c_i, ws_ref[g], 0.0).astype(BF16)
        bias = bsT[:, g:g + 1]
        for c in range(tm // CHUNK):
            rows = slice(c * CHUNK, (c + 1) * CHUNK)
            cols = slice(g * gd, (g + 1) * gd)
            mixed = jnp.dot(wm, vln_ref[rows, cols], preferred_element_type=F32) + bias
            ya_ref[rows, cols] = (u_ref[rows, cols].astype(F32) * mixed).astype(BF16)

    pa = jnp.dot(ya_ref[...], wpa_ref[...], preferred_element_type=F32)
    pb = jnp.dot(yb_ref[...], wpb_ref[...], preferred_element_type=F32)
    merged = sga_ref[...].astype(F32) * pa + sgb_ref[...].astype(F32) * pb
    x1 = x_ref[...] + gate1 * jnp.dot(merged.astype(BF16), wout_ref[...], preferred_element_type=F32)

    h2 = (_rms_norm(x1, n2g_ref[...]) * (1.0 + scale2) + shift2).astype(BF16)
    fg = jnp.dot(h2, wfi_ref[:, 0:D_FF], preferred_element_type=F32)
    fu = jnp.dot(h2, wfi_ref[:, D_FF:], preferred_element_type=F32)
    act = (fg * _sigmoid(fg) * fu).astype(BF16)
    x2 = x1 + gate2 * jnp.dot(act, wfo_ref[...], preferred_element_type=F32)
    o_ref[...] = _rms_norm(x2, fg_ref[...])


def _post_call(x2, mod3, u, vln, sga, sgb, yb, ws, bsT, wpa, wpb, wout, n2g, wfi, wfo, fg):
    tm = TM_OUT
    per_b = SEQ // tm
    row = lambda t: (t, 0)
    const2 = lambda t: (0, 0)
    tok = lambda: pl.BlockSpec((tm, D_MODEL), row)
    in_specs = [
        tok(),
        pl.BlockSpec((1, 1, 6 * D_MODEL), lambda t: (t // per_b, 0, 0)),
        tok(), tok(), tok(), tok(), tok(),
        pl.BlockSpec(ws.shape, lambda t: (0, 0, 0)),
        pl.BlockSpec(bsT.shape, const2),
        pl.BlockSpec(wpa.shape, const2),
        pl.BlockSpec(wpb.shape, const2),
        pl.BlockSpec(wout.shape, const2),
        pl.BlockSpec((1, D_MODEL), const2),
        pl.BlockSpec(wfi.shape, const2),
        pl.BlockSpec(wfo.shape, const2),
        pl.BlockSpec((1, D_MODEL), const2),
    ]
    return pl.pallas_call(
        _post_kernel,
        grid=(TOKENS // tm,),
        in_specs=in_specs,
        out_specs=pl.BlockSpec((tm, D_MODEL), row),
        out_shape=jax.ShapeDtypeStruct((TOKENS, D_MODEL), F32),
        scratch_shapes=[pltpu.VMEM((tm, A_WIDTH), BF16)],
        compiler_params=pltpu.CompilerParams(dimension_semantics=("arbitrary",),
                                             vmem_limit_bytes=VMEM_LIMIT),
        name="merge_ffn",
    )(x2, mod3, u, vln, sga, sgb, yb, ws, bsT, wpa, wpb, wout, n2g, wfi, wfo, fg)


def _inv_freq_table():
    fq = ROPE_THETA ** (-np.arange(0, ROT_DIM, 2, dtype=np.float32) / ROT_DIM)
    fi = ROPE_THETA ** (-np.arange(0, IDX_ROT_DIM, 2, dtype=np.float32) / IDX_ROT_DIM)
    return np.concatenate([fq, fi]).astype(np.float32).reshape(-1, 1)


def kernel(x, c, positions, w_ada, b_ada, norm1_g, w_in, gmlp_ln_g, gmlp_ln_b, gmlp_w_s, gmlp_b_s,
           idx_k_ln_g, idx_k_ln_b, w_proj_a, w_proj_b, w_out, norm2_g, w_ffn_in, w_ffn_out, final_norm_g):
    assert x.shape == (BATCH, SEQ, D_MODEL) and w_in.shape == (1, D_MODEL, _IN_COLS)
    x2 = x.reshape(TOKENS, D_MODEL)
    pos3 = positions.reshape(BATCH, 1, SEQ)
    invf = jnp.asarray(_inv_freq_table())
    xcur = x2
    for l in range(w_ada.shape[0]):
        w = w_in[l]
        wn = jnp.concatenate([w[:, _OFF_U:_OFF_Q], w[:, _OFF_GA:]], axis=1).astype(BF16)
        wt = w[:, _OFF_Q:_OFF_GA].astype(BF16).T
        mod = _ada_call(c, w_ada[l], b_ada[l])
        mod3 = mod.reshape(BATCH, 1, 6 * D_MODEL)
        u, vln, sga, sgb, qT, k, vT, qiT, ki, wiT = _inproj_call(
            xcur, mod3, pos3, norm1_g[l].reshape(1, -1), wn, wt,
            gmlp_ln_g[l].reshape(1, -1), gmlp_ln_b[l].reshape(1, -1),
            idx_k_ln_g[l].reshape(-1, 1), idx_k_ln_b[l].reshape(-1, 1), invf)
        yb = _attn_call(qT, qiT, wiT, k, ki, vT)
        last = l == w_ada.shape[0] - 1
        fg = final_norm_g.reshape(1, -1) if last else None
        assert last, "single-layer block"
        xcur = _post_call(xcur, mod3, u, vln, sga, sgb, yb, gmlp_w_s[l], gmlp_b_s[l].T,
                          w_proj_a[l].astype(BF16), w_proj_b[l].astype(BF16), w_out[l].astype(BF16),
                          norm2_g[l].reshape(1, -1), w_ffn_in[l].astype(BF16), w_ffn_out[l].astype(BF16), fg)
    return xcur.reshape(BATCH, SEQ, D_MODEL)
```

```python
import functools

import numpy as np
import jax
import jax.numpy as jnp
from jax import lax
from jax.experimental import pallas as pl
from jax.experimental.pallas import tpu as pltpu

D_MODEL = 1024
BATCH = 4
SEQ = 4096
CHUNK = 128
A_GROUPS = 8
A_WIDTH = 1024
N_HEADS = 8
N_KV_HEADS = 2
HEAD_DIM = 128
IDX_HEADS = 8
IDX_DIM = 64
TOPK = 256
ROPE_THETA = 500000.0
ROT_DIM = HEAD_DIM // 4
IDX_ROT_DIM = IDX_DIM // 4
D_FF = 2816
EPS = 1e-6
NEG_INF = -1e30

TOKENS = BATCH * SEQ
Q_TILE = 256
KEY_TILE = 512
ACC_ROWS = 32
FIRST_CHECK_AFTER = 16
CHECK_EVERY = 2
SWEEP_UNROLL = (2, 1)
TM_IN = 512
TM_OUT = 256
HEADS_PER_KV = N_HEADS // N_KV_HEADS
VMEM_LIMIT = 56 * 1024 * 1024

_OFF_U = 0
_OFF_V = _OFF_U + A_WIDTH
_OFF_Q = _OFF_V + A_WIDTH
_OFF_K = _OFF_Q + N_HEADS * HEAD_DIM
_OFF_VV = _OFF_K + N_KV_HEADS * HEAD_DIM
_OFF_QI = _OFF_VV + N_KV_HEADS * HEAD_DIM
_OFF_KI = _OFF_QI + IDX_HEADS * IDX_DIM
_OFF_WI = _OFF_KI + IDX_DIM
_OFF_GA = _OFF_WI + IDX_HEADS
_OFF_GB = _OFF_GA + D_MODEL
_IN_COLS = _OFF_GB + D_MODEL

_T_Q = 0
_T_K = _T_Q + N_HEADS * HEAD_DIM
_T_V = _T_K + N_KV_HEADS * HEAD_DIM
_T_QI = _T_V + N_KV_HEADS * HEAD_DIM
_T_KI = _T_QI + IDX_HEADS * IDX_DIM
_T_WI = _T_KI + IDX_DIM
_T_ROWS = _T_WI + IDX_HEADS

F32 = jnp.float32
BF16 = jnp.bfloat16


def _gelu_tanh(x):
    return 0.5 * x * (1.0 + jnp.tanh(np.sqrt(2.0 / np.pi).astype(np.float32) * (x + 0.044715 * (x * x * x))))


def _sigmoid(x):
    return 1.0 / (1.0 + jnp.exp(-x))


def _rms_norm(x, g):
    return x * lax.rsqrt(jnp.mean(x * x, axis=-1, keepdims=True) + EPS) * g


def _ada_kernel(c_ref, w_ref, b_ref, o_ref):
    c = c_ref[...]
    a = c * _sigmoid(c)
    w = w_ref[...]
    a_hi = a.astype(BF16)
    a_lo = (a - a_hi.astype(F32)).astype(BF16)
    w_hi = w.astype(BF16)
    w_lo = (w - w_hi.astype(F32)).astype(BF16)
    dot = functools.partial(jnp.dot, preferred_element_type=F32)
    o_ref[...] = dot(a_hi, w_hi) + (dot(a_hi, w_lo) + dot(a_lo, w_hi)) + b_ref[...]


def _ada_call(c, w_ada, b_ada):
    n_out = 6 * D_MODEL
    tn = 1024
    return pl.pallas_call(
        _ada_kernel,
        grid=(n_out // tn,),
        in_specs=[pl.BlockSpec((BATCH, D_MODEL), lambda j: (0, 0)),
                  pl.BlockSpec((D_MODEL, tn), lambda j: (0, j)),
                  pl.BlockSpec((1, tn), lambda j: (0, j))],
        out_specs=pl.BlockSpec((BATCH, tn), lambda j: (0, j)),
        out_shape=jax.ShapeDtypeStruct((BATCH, n_out), F32),
        compiler_params=pltpu.CompilerParams(dimension_semantics=("arbitrary",),
                                             vmem_limit_bytes=VMEM_LIMIT),
        name="ada_mod",
    )(c, w_ada, b_ada.reshape(1, n_out))


def _rope_rows(blk, cos, sin, half):
    x1 = blk[0:half]
    x2 = blk[half:2 * half]
    return x1 * cos - x2 * sin, x2 * cos + x1 * sin


def _inproj_kernel(x_ref, mod_ref, pos_ref, n1g_ref, wn_ref, wt_ref, lng_ref, lnb_ref,
                   kig_ref, kib_ref, invf_ref,
                   u_ref, vln_ref, sga_ref, sgb_ref, qT_ref, k_ref, vT_ref, qiT_ref, ki_ref, wiT_ref):
    tm = x_ref.shape[0]
    x = x_ref[...]
    mod = mod_ref[0]
    shift1 = mod[:, 0:D_MODEL]
    scale1 = mod[:, D_MODEL:2 * D_MODEL]
    h = _rms_norm(x, n1g_ref[...]) * (1.0 + scale1) + shift1
    hb = h.astype(BF16)

    zu = jnp.dot(hb, wn_ref[:, 0:A_WIDTH], preferred_element_type=F32)
    u_ref[...] = _gelu_tanh(zu).astype(BF16)
    zv = _gelu_tanh(jnp.dot(hb, wn_ref[:, A_WIDTH:2 * A_WIDTH], preferred_element_type=F32))
    mu = jnp.mean(zv, axis=-1, keepdims=True)
    zc = zv - mu
    var = jnp.mean(zc * zc, axis=-1, keepdims=True)
    vln_ref[...] = (zc * lax.rsqrt(var + EPS) * lng_ref[...] + lnb_ref[...]).astype(BF16)
    zga = jnp.dot(hb, wn_ref[:, 2 * A_WIDTH:2 * A_WIDTH + D_MODEL], preferred_element_type=F32)
    sga_ref[...] = _sigmoid(zga).astype(BF16)
    zgb = jnp.dot(hb, wn_ref[:, 2 * A_WIDTH + D_MODEL:], preferred_element_type=F32)
    sgb_ref[...] = _sigmoid(zgb).astype(BF16)

    nt = (((1,), (1,)), ((), ()))
    pos = pos_ref[0].astype(F32)
    ang = invf_ref[...] * pos
    cos = jnp.cos(ang)
    sin = jnp.sin(ang)
    hq = ROT_DIM // 2
    hi = IDX_ROT_DIM // 2
    cos_q, sin_q = cos[0:hq], sin[0:hq]
    cos_i, sin_i = cos[hq:hq + hi], sin[hq:hq + hi]
    n_sub = tm // Q_TILE

    zq = lax.dot_general(wt_ref[_T_Q:_T_K, :], hb, nt, preferred_element_type=F32)
    q_scale = HEAD_DIM ** -0.5 * float(np.log2(np.e))
    for hd in range(N_HEADS):
        blk = zq[hd * HEAD_DIM:(hd + 1) * HEAD_DIM]
        r1, r2 = _rope_rows(blk, cos_q, sin_q, hq)
        full = (jnp.concatenate([r1, r2, blk[ROT_DIM:]], axis=0) * q_scale).astype(BF16)
        for s in range(n_sub):
            qT_ref[s, :, hd * Q_TILE:(hd + 1) * Q_TILE] = full[:, s * Q_TILE:(s + 1) * Q_TILE]

    zk = lax.dot_general(wt_ref[_T_K:_T_V, :], hb, nt, preferred_element_type=F32)
    k_rows = []
    for g in range(N_KV_HEADS):
        blk = zk[g * HEAD_DIM:(g + 1) * HEAD_DIM]
        r1, r2 = _rope_rows(blk, cos_q, sin_q, hq)
        k_rows += [r1, r2, blk[ROT_DIM:]]
    k_ref[...] = jnp.concatenate(k_rows, axis=0).T.astype(BF16)

    zvv = lax.dot_general(wt_ref[_T_V:_T_QI, :], hb, nt, preferred_element_type=F32)
    vT_ref[...] = zvv.astype(BF16)

    zqi = lax.dot_general(wt_ref[_T_QI:_T_KI, :], hb, nt, preferred_element_type=F32)
    for hd in range(IDX_HEADS):
        blk = zqi[hd * IDX_DIM:(hd + 1) * IDX_DIM]
        r1, r2 = _rope_rows(blk, cos_i, sin_i, hi)
        full = jnp.concatenate([r1, r2, blk[IDX_ROT_DIM:]], axis=0).astype(BF16)
        for s in range(n_sub):
            qiT_ref[s, :, hd * Q_TILE:(hd + 1) * Q_TILE] = full[:, s * Q_TILE:(s + 1) * Q_TILE]

    zrest = lax.dot_general(wt_ref[_T_KI:_T_ROWS, :], hb, nt, preferred_element_type=F32)
    zki = zrest[0:IDX_DIM]
    kmu = jnp.mean(zki, axis=0, keepdims=True)
    kc = zki - kmu
    kvar = jnp.mean(kc * kc, axis=0, keepdims=True)
    kin = kc * lax.rsqrt(kvar + EPS) * kig_ref[...] + kib_ref[...]
    r1, r2 = _rope_rows(kin, cos_i, sin_i, hi)
    ki_full = jnp.concatenate([r1, r2, kin[IDX_ROT_DIM:], jnp.zeros((128 - IDX_DIM, tm), F32)], axis=0)
    ki_ref[...] = ki_full.T.astype(BF16)
    wiT_ref[...] = zrest[IDX_DIM:IDX_DIM + IDX_HEADS] * ((IDX_HEADS ** -0.5) * (IDX_DIM ** -0.5))


def _inproj_call(x2, mod3, pos3, n1g, wn, wt, lng, lnb, kig, kib, invf):
    tm = TM_IN
    n_tiles = TOKENS // tm
    per_b = SEQ // tm
    n_sub = tm // Q_TILE
    const2 = lambda t: (0, 0)
    row = lambda t: (t, 0)
    in_specs = [
        pl.BlockSpec((tm, D_MODEL), row),
        pl.BlockSpec((1, 1, 6 * D_MODEL), lambda t: (t // per_b, 0, 0)),
        pl.BlockSpec((1, 1, tm), lambda t: (t // per_b, 0, t % per_b)),
        pl.BlockSpec((1, D_MODEL), const2),
        pl.BlockSpec(wn.shape, const2),
        pl.BlockSpec(wt.shape, const2),
        pl.BlockSpec((1, A_WIDTH), const2),
        pl.BlockSpec((1, A_WIDTH), const2),
        pl.BlockSpec((IDX_DIM, 1), const2),
        pl.BlockSpec((IDX_DIM, 1), const2),
        pl.BlockSpec(invf.shape, const2),
    ]
    out_shape = [
        jax.ShapeDtypeStruct((TOKENS, A_WIDTH), BF16),
        jax.ShapeDtypeStruct((TOKENS, A_WIDTH), BF16),
        jax.ShapeDtypeStruct((TOKENS, D_MODEL), BF16),
        jax.ShapeDtypeStruct((TOKENS, D_MODEL), BF16),
        jax.ShapeDtypeStruct((TOKENS // Q_TILE, HEAD_DIM, N_HEADS * Q_TILE), BF16),
        jax.ShapeDtypeStruct((TOKENS, N_KV_HEADS * HEAD_DIM), BF16),
        jax.ShapeDtypeStruct((N_KV_HEADS * HEAD_DIM, TOKENS), BF16),
        jax.ShapeDtypeStruct((TOKENS // Q_TILE, IDX_DIM, IDX_HEADS * Q_TILE), BF16),
        jax.ShapeDtypeStruct((TOKENS, 128), BF16),
        jax.ShapeDtypeStruct((IDX_HEADS, TOKENS), F32),
    ]
    out_specs = [
        pl.BlockSpec((tm, A_WIDTH), row),
        pl.BlockSpec((tm, A_WIDTH), row),
        pl.BlockSpec((tm, D_MODEL), row),
        pl.BlockSpec((tm, D_MODEL), row),
        pl.BlockSpec((n_sub, HEAD_DIM, N_HEADS * Q_TILE), lambda t: (t, 0, 0)),
        pl.BlockSpec((tm, N_KV_HEADS * HEAD_DIM), row),
        pl.BlockSpec((N_KV_HEADS * HEAD_DIM, tm), lambda t: (0, t)),
        pl.BlockSpec((n_sub, IDX_DIM, IDX_HEADS * Q_TILE), lambda t: (t, 0, 0)),
        pl.BlockSpec((tm, 128), row),
        pl.BlockSpec((IDX_HEADS, tm), lambda t: (0, t)),
    ]
    return pl.pallas_call(
        _inproj_kernel,
        grid=(n_tiles,),
        in_specs=in_specs,
        out_specs=out_specs,
        out_shape=out_shape,
        compiler_params=pltpu.CompilerParams(dimension_semantics=("arbitrary",),
                                             vmem_limit_bytes=VMEM_LIMIT),
        name="in_proj",
    )(x2, mod3, pos3, n1g, wn, wt, lng, lnb, kig, kib, invf)


def _col_reduce(x, op):
    return op(x.reshape(x.shape[0] // ACC_ROWS, ACC_ROWS, x.shape[1]), axis=0)


def _key_tiles(tile):
    return (tile * Q_TILE + Q_TILE + KEY_TILE - 1) // KEY_TILE


def _attn_kernel(qT_ref, qiT_ref, wiT_ref, k_ref, ki_ref, vT_ref, y_ref,
                 sc_ref, acc_ref, l_ref, vsel_ref):
    t = pl.program_id(1)
    nq = SEQ // Q_TILE
    i = jnp.minimum(t, nq - 1)
    ia = jnp.maximum(t - 1, 0)
    sc_i = sc_ref.at[i % 2]
    sc_a = sc_ref.at[ia % 2]
    nkt = jnp.where(t < nq, _key_tiles(i), 0)
    nka = jnp.where(t >= 1, _key_tiles(ia), 0)
    kf = float(TOPK)
    gw = HEADS_PER_KV * Q_TILE
    hw = N_HEADS * Q_TILE

    @pl.when(t == 0)
    def _():
        vsel_ref[...] = jnp.zeros_like(vsel_ref)

    qiT = qiT_ref[0]
    qT = qT_ref[0]
    w = wiT_ref[...]
    wrow = jnp.concatenate([w[hd:hd + 1, :] for hd in range(IDX_HEADS)], axis=1)
    vsel_a = vsel_ref[ia % 2]

    row_i = lax.broadcasted_iota(jnp.int32, (KEY_TILE, Q_TILE), 0)
    qidx = i * Q_TILE + lax.broadcasted_iota(jnp.int32, (KEY_TILE, Q_TILE), 1)

    def score_tile(j, mx_a, mn_a):
        sl = pl.ds(pl.multiple_of(j * KEY_TILE, KEY_TILE), KEY_TILE)
        lg = jnp.dot(ki_ref[sl, 0:IDX_DIM], qiT, preferred_element_type=F32)
        r = jnp.maximum(lg, 0.0) * wrow
        s = r[:, 0:Q_TILE]
        for hd in range(1, IDX_HEADS):
            s = s + r[:, hd * Q_TILE:(hd + 1) * Q_TILE]
        causal = j * KEY_TILE + row_i <= qidx
        sc_i[sl, :] = jnp.where(causal, s, -jnp.inf)
        mx_a = jnp.maximum(mx_a, _col_reduce(jnp.where(causal, s, -jnp.inf), jnp.max))
        mn_a = jnp.minimum(mn_a, _col_reduce(jnp.where(causal, s, jnp.inf), jnp.min))
        return mx_a, mn_a

    def key_mask(sl):
        return jnp.where(sc_a[sl, :] >= vsel_a, 0.0, NEG_INF)

    def qk_tile(j):
        sl = pl.ds(pl.multiple_of(j * KEY_TILE, KEY_TILE), KEY_TILE)
        return [jnp.dot(k_ref[sl, g * HEAD_DIM:(g + 1) * HEAD_DIM], qT[:, g * gw:(g + 1) * gw],
                        preferred_element_type=F32) for g in range(N_KV_HEADS)]

    def attend_tile(j, qk, l8, m):
        sl = pl.ds(pl.multiple_of(j * KEY_TILE, KEY_TILE), KEY_TILE)
        mask = key_mask(sl)
        parts = []
        for g in range(N_KV_HEADS):
            a = qk[g]
            ps = []
            for r in range(HEADS_PER_KV):
                e = a[:, r * Q_TILE:(r + 1) * Q_TILE] + mask
                if m is not None:
                    c0 = (g * HEADS_PER_KV + r) * Q_TILE
                    e = e - m[:, c0:c0 + Q_TILE]
                ps.append(jnp.exp2(e))
            p = jnp.concatenate(ps, axis=1)
            acc_ref[:, g * gw:(g + 1) * gw] += jnp.dot(
                vT_ref[g * HEAD_DIM:(g + 1) * HEAD_DIM, sl], p.astype(BF16), preferred_element_type=F32)
            parts.append(jnp.sum(p.reshape(KEY_TILE // 8, 8, gw), axis=0))
        return l8 + jnp.concatenate(parts, axis=1)

    acc_ref[...] = jnp.zeros_like(acc_ref)
    n_both = jnp.minimum(nkt, nka)

    def both_body(j, c):
        mx_a, mn_a, l8 = c
        mx_a, mn_a = score_tile(j, mx_a, mn_a)
        return mx_a, mn_a, attend_tile(j, qk_tile(j), l8, None)

    def score_body(j, c):
        mx_a, mn_a, l8 = c
        mx_a, mn_a = score_tile(j, mx_a, mn_a)
        return mx_a, mn_a, l8

    def attend_body(j, c):
        mx_a, mn_a, l8 = c
        return mx_a, mn_a, attend_tile(j, qk_tile(j), l8, None)

    carry = (jnp.full((ACC_ROWS, Q_TILE), -jnp.inf, F32), jnp.full((ACC_ROWS, Q_TILE), jnp.inf, F32),
             jnp.zeros((8, hw), F32))
    done = 0
    for width in SWEEP_UNROLL:
        def wide_body(jw, c, width=width, done=done):
            for u in range(width):
                c = both_body(done + jw * width + u, c)
            return c
        trips = (n_both - done) // width
        carry = lax.fori_loop(0, trips, wide_body, carry)
        done = done + trips * width
    carry = lax.fori_loop(n_both, nkt, score_body, carry)
    mx_a, mn_a, l8 = lax.fori_loop(n_both, nka, attend_body, carry)
    l_ref[...] = jnp.sum(l8, axis=0, keepdims=True)

    def tiles(fn, init):
        def body(j, carry):
            s = sc_i[pl.ds(pl.multiple_of(j * KEY_TILE, KEY_TILE), KEY_TILE), :]
            return fn(s, carry)
        return lax.fori_loop(0, nkt, body, init)

    def count_ge(t):
        c8 = tiles(lambda s, c: c + _col_reduce(jnp.where(s >= t, 1.0, 0.0), jnp.sum),
                   jnp.zeros((ACC_ROWS, Q_TILE), F32))
        return jnp.sum(c8, axis=0, keepdims=True)

    def bisect(state, n):
        def step(_, c):
            lo, hi, n_lo, n_hi = c
            mid = 0.5 * lo + 0.5 * hi
            n_mid = count_ge(mid)
            ok = n_mid >= kf
            return (jnp.where(ok, mid, lo), jnp.where(ok, hi, mid),
                    jnp.where(ok, n_mid, n_lo), jnp.where(ok, n_hi, n_mid))
        return lax.fori_loop(0, n, step, state)

    @pl.when(t < TOPK // Q_TILE)
    def _():
        vsel_ref[i % 2] = jnp.full((1, Q_TILE), jnp.finfo(jnp.float32).min, F32)

    @pl.when((t >= TOPK // Q_TILE) & (t < nq))
    def _():
        mx = jnp.max(mx_a, axis=0, keepdims=True)
        mn = jnp.min(mn_a, axis=0, keepdims=True)
        n_valid = (qidx[0:1, :] + 1).astype(F32)
        hi0 = mx + jnp.maximum(jnp.abs(mx), 1e-30) * 1e-6
        state = bisect((mn, hi0, n_valid, jnp.zeros((1, Q_TILE), F32)), FIRST_CHECK_AFTER - CHECK_EVERY)

        def check(state):
            lo, hi, n_lo, n_hi = state

            def f(s, c):
                a8, b8 = c
                a8 = jnp.maximum(a8, _col_reduce(jnp.where(s < hi, s, -jnp.inf), jnp.max))
                b8 = jnp.minimum(b8, _col_reduce(jnp.where(s >= lo, s, jnp.inf), jnp.min))
                return a8, b8
            a8, b8 = tiles(f, (jnp.full((ACC_ROWS, Q_TILE), -jnp.inf, F32), jnp.full((ACC_ROWS, Q_TILE), jnp.inf, F32)))
            top = jnp.max(a8, axis=0, keepdims=True)
            bottom = jnp.min(b8, axis=0, keepdims=True)
            single = top == bottom
            pinned = single | (n_lo - n_hi < 2.5)
            take_top = jnp.logical_not(single) & (kf - n_hi < 1.5)
            take_bottom = jnp.logical_not(single | take_top)
            v = jnp.where(take_top, top, bottom)
            n_ge = jnp.where(take_top, n_hi + 1.0, n_lo)
            n_gt = jnp.where(take_bottom, n_hi + 1.0, n_hi)
            return (v, n_ge, n_gt), jnp.min(jnp.where(pinned, 1.0, 0.0))

        def w_cond(c):
            return c[2] < 0.5

        def w_body(c):
            st = bisect(c[0], CHECK_EVERY)
            found, done = check(st)
            return st, found, done

        _, (vk, n_ge, n_gt), _ = lax.while_loop(w_cond, w_body, (state, (mn, n_valid, n_valid), jnp.float32(0.0)))
        need = kf - n_gt
        has_tie = jnp.max(n_ge) > kf + 0.5

        @pl.when(jnp.logical_not(has_tie))
        def _():
            vsel_ref[i % 2] = vk

        @pl.when(has_tie)
        def _():
            vsel_ref[i % 2] = jnp.full((1, Q_TILE), -0.5, F32)
            r_i = lax.broadcasted_iota(jnp.int32, (KEY_TILE, KEY_TILE), 0)
            c_i = lax.broadcasted_iota(jnp.int32, (KEY_TILE, KEY_TILE), 1)
            tri = jnp.where(r_i >= c_i, 1.0, 0.0).astype(BF16)

            def body(j, seen):
                sl = pl.ds(pl.multiple_of(j * KEY_TILE, KEY_TILE), KEY_TILE)
                s = sc_i[sl, :]
                eq = s == vk
                rank = jnp.dot(tri, jnp.where(eq, 1.0, 0.0).astype(BF16), preferred_element_type=F32) + seen
                sel = (s > vk) | (eq & (rank <= need))
                sc_i[sl, :] = jnp.where(sel, 0.0, NEG_INF)
                return rank[KEY_TILE - 1:KEY_TILE, :]
            lax.fori_loop(0, nkt, body, jnp.zeros((1, Q_TILE), F32))

    @pl.when(t >= 1)
    def _():
        l_fast = l_ref[...]
        in_range = (jnp.min(l_fast) > 1e-20) & (jnp.max(l_fast) < 1e30)

        @pl.when(jnp.logical_not(in_range))
        def _():
            def max_body(j, m8):
                sl = pl.ds(pl.multiple_of(j * KEY_TILE, KEY_TILE), KEY_TILE)
                mask = key_mask(sl)
                parts = []
                for g in range(N_KV_HEADS):
                    a = jnp.dot(k_ref[sl, g * HEAD_DIM:(g + 1) * HEAD_DIM], qT[:, g * gw:(g + 1) * gw],
                                preferred_element_type=F32)
                    a = a + jnp.concatenate([mask] * HEADS_PER_KV, axis=1)
                    parts.append(jnp.max(a.reshape(KEY_TILE // 8, 8, gw), axis=0))
                return jnp.maximum(m8, jnp.concatenate(parts, axis=1))
            m8 = lax.fori_loop(0, nka, max_body, jnp.full((8, hw), -jnp.inf, F32))
            m = jnp.max(m8, axis=0, keepdims=True)
            acc_ref[...] = jnp.zeros_like(acc_ref)
            l8 = lax.fori_loop(0, nka, lambda j, c: attend_tile(j, qk_tile(j), c, m), jnp.zeros((8, hw), F32))
            l_ref[...] = jnp.sum(l8, axis=0, keepdims=True)

        oT = acc_ref[...] / l_ref[...]
        for hd in range(N_HEADS):
            y_ref[:, hd * HEAD_DIM:(hd + 1) * HEAD_DIM] = oT[:, hd * Q_TILE:(hd + 1) * Q_TILE].T.astype(BF16)


def _attn_call(qT, qiT, wiT, k, ki, vT):
    nq = SEQ // Q_TILE
    scored = lambda b, t: b * nq + jnp.minimum(t, nq - 1)
    attended = lambda b, t: b * nq + jnp.maximum(t - 1, 0)
    return pl.pallas_call(
        _attn_kernel,
        grid=(BATCH, nq + 1),
        in_specs=[
            pl.BlockSpec((1, HEAD_DIM, N_HEADS * Q_TILE), lambda b, t: (attended(b, t), 0, 0)),
            pl.BlockSpec((1, IDX_DIM, IDX_HEADS * Q_TILE), lambda b, t: (scored(b, t), 0, 0)),
            pl.BlockSpec((IDX_HEADS, Q_TILE), lambda b, t: (0, scored(b, t))),
            pl.BlockSpec((SEQ, N_KV_HEADS * HEAD_DIM), lambda b, t: (b, 0)),
            pl.BlockSpec((SEQ, 128), lambda b, t: (b, 0)),
            pl.BlockSpec((N_KV_HEADS * HEAD_DIM, SEQ), lambda b, t: (0, b)),
        ],
        out_specs=pl.BlockSpec((Q_TILE, N_HEADS * HEAD_DIM), lambda b, t: (attended(b, t), 0)),
        out_shape=jax.ShapeDtypeStruct((TOKENS, N_HEADS * HEAD_DIM), BF16),
        scratch_shapes=[pltpu.VMEM((2, SEQ, Q_TILE), F32),
                        pltpu.VMEM((HEAD_DIM, N_HEADS * Q_TILE), F32),
                        pltpu.VMEM((1, N_HEADS * Q_TILE), F32),
                        pltpu.VMEM((2, 1, Q_TILE), F32)],
        compiler_params=pltpu.CompilerParams(dimension_semantics=("arbitrary", "arbitrary"),
                                             vmem_limit_bytes=VMEM_LIMIT),
        name="dsa_attention",
    )(qT, qiT, wiT, k, ki, vT)


def _post_kernel(x_ref, mod_ref, u_ref, vln_ref, sga_ref, sgb_ref, yb_ref, ws_ref, bsT_ref,
                 wpa_ref, wpb_ref, wout_ref, n2g_ref, wfi_ref, wfo_ref, fg_ref, o_ref, ya_ref):
    tm = x_ref.shape[0]
    mod = mod_ref[0]
    gate1 = mod[:, 2 * D_MODEL:3 * D_MODEL]
    shift2 = mod[:, 3 * D_MODEL:4 * D_MODEL]
    scale2 = mod[:, 4 * D_MODEL:5 * D_MODEL]
    gate2 = mod[:, 5 * D_MODEL:6 * D_MODEL]

    r_i = lax.broadcasted_iota(jnp.int32, (CHUNK, CHUNK), 0)
    c_i = lax.broadcasted_iota(jnp.int32, (CHUNK, CHUNK), 1)
    bsT = bsT_ref[...]
    gd = A_WIDTH // A_GROUPS
    for g in range(A_GROUPS):
        wm = jnp.where(r_i >= c_i, ws_ref[g], 0.0).astype(BF16)
        bias = bsT[:, g:g + 1]
        for c in range(tm // CHUNK):
            rows = slice(c * CHUNK, (c + 1) * CHUNK)
            cols = slice(g * gd, (g + 1) * gd)
            mixed = jnp.dot(wm, vln_ref[rows, cols], preferred_element_type=F32) + bias
            ya_ref[rows, cols] = (u_ref[rows, cols].astype(F32) * mixed).astype(BF16)

    pa = jnp.dot(ya_ref[...], wpa_ref[...], preferred_element_type=F32)
    pb = jnp.dot(yb_ref[...], wpb_ref[...], preferred_element_type=F32)
    merged = sga_ref[...].astype(F32) * pa + sgb_ref[...].astype(F32) * pb
    x1 = x_ref[...] + gate1 * jnp.dot(merged.astype(BF16), wout_ref[...], preferred_element_type=F32)

    h2 = (_rms_norm(x1, n2g_ref[...]) * (1.0 + scale2) + shift2).astype(BF16)
    fg = jnp.dot(h2, wfi_ref[:, 0:D_FF], preferred_element_type=F32)
    fu = jnp.dot(h2, wfi_ref[:, D_FF:], preferred_element_type=F32)
    act = (fg * _sigmoid(fg) * fu).astype(BF16)
    x2 = x1 + gate2 * jnp.dot(act, wfo_ref[...], preferred_element_type=F32)
    o_ref[...] = _rms_norm(x2, fg_ref[...])


def _post_call(x2, mod3, u, vln, sga, sgb, yb, ws, bsT, wpa, wpb, wout, n2g, wfi, wfo, fg):
    tm = TM_OUT
    per_b = SEQ // tm
    row = lambda t: (t, 0)
    const2 = lambda t: (0, 0)
    tok = lambda: pl.BlockSpec((tm, D_MODEL), row)
    in_specs = [
        tok(),
        pl.BlockSpec((1, 1, 6 * D_MODEL), lambda t: (t // per_b, 0, 0)),
        tok(), tok(), tok(), tok(), tok(),
        pl.BlockSpec(ws.shape, lambda t: (0, 0, 0)),
        pl.BlockSpec(bsT.shape, const2),
        pl.BlockSpec(wpa.shape, const2),
        pl.BlockSpec(wpb.shape, const2),
        pl.BlockSpec(wout.shape, const2),
        pl.BlockSpec((1, D_MODEL), const2),
        pl.BlockSpec(wfi.shape, const2),
        pl.BlockSpec(wfo.shape, const2),
        pl.BlockSpec((1, D_MODEL), const2),
    ]
    return pl.pallas_call(
        _post_kernel,
        grid=(TOKENS // tm,),
        in_specs=in_specs,
        out_specs=pl.BlockSpec((tm, D_MODEL), row),
        out_shape=jax.ShapeDtypeStruct((TOKENS, D_MODEL), F32),
        scratch_shapes=[pltpu.VMEM((tm, A_WIDTH), BF16)],
        compiler_params=pltpu.CompilerParams(dimension_semantics=("arbitrary",),
                                             vmem_limit_bytes=VMEM_LIMIT),
        name="merge_ffn",
    )(x2, mod3, u, vln, sga, sgb, yb, ws, bsT, wpa, wpb, wout, n2g, wfi, wfo, fg)


def _inv_freq_table():
    fq = ROPE_THETA ** (-np.arange(0, ROT_DIM, 2, dtype=np.float32) / ROT_DIM)
    fi = ROPE_THETA ** (-np.arange(0, IDX_ROT_DIM, 2, dtype=np.float32) / IDX_ROT_DIM)
    return np.concatenate([fq, fi]).astype(np.float32).reshape(-1, 1)


def kernel(x, c, positions, w_ada, b_ada, norm1_g, w_in, gmlp_ln_g, gmlp_ln_b, gmlp_w_s, gmlp_b_s,
           idx_k_ln_g, idx_k_ln_b, w_proj_a, w_proj_b, w_out, norm2_g, w_ffn_in, w_ffn_out, final_norm_g):
    assert x.shape == (BATCH, SEQ, D_MODEL) and w_in.shape == (1, D_MODEL, _IN_COLS)
    x2 = x.reshape(TOKENS, D_MODEL)
    pos3 = positions.reshape(BATCH, 1, SEQ)
    invf = jnp.asarray(_inv_freq_table())
    xcur = x2
    for l in range(w_ada.shape[0]):
        w = w_in[l]
        wn = jnp.concatenate([w[:, _OFF_U:_OFF_Q], w[:, _OFF_GA:]], axis=1).astype(BF16)
        wt = w[:, _OFF_Q:_OFF_GA].astype(BF16).T
        mod = _ada_call(c, w_ada[l], b_ada[l])
        mod3 = mod.reshape(BATCH, 1, 6 * D_MODEL)
        u, vln, sga, sgb, qT, k, vT, qiT, ki, wiT = _inproj_call(
            xcur, mod3, pos3, norm1_g[l].reshape(1, -1), wn, wt,
            gmlp_ln_g[l].reshape(1, -1), gmlp_ln_b[l].reshape(1, -1),
            idx_k_ln_g[l].reshape(-1, 1), idx_k_ln_b[l].reshape(-1, 1), invf)
        yb = _attn_call(qT, qiT, wiT, k, ki, vT)
        last = l == w_ada.shape[0] - 1
        fg = final_norm_g.reshape(1, -1) if last else None
        assert last, "single-layer block"
        xcur = _post_call(xcur, mod3, u, vln, sga, sgb, yb, gmlp_w_s[l], gmlp_b_s[l].T,
                          w_proj_a[l].astype(BF16), w_proj_b[l].astype(BF16), w_out[l].astype(BF16),
                          norm2_g[l].reshape(1, -1), w_ffn_in[l].astype(BF16), w_ffn_out[l].astype(BF16), fg)
    return xcur.reshape(BATCH, SEQ, D_MODEL)
```

```python
import functools

import numpy as np
import jax
import jax.numpy as jnp
from jax import lax
from jax.experimental import pallas as pl
from jax.experimental.pallas import tpu as pltpu

D_MODEL = 1024
BATCH = 4
SEQ = 4096
CHUNK = 128
A_GROUPS = 8
A_WIDTH = 1024
N_HEADS = 8
N_KV_HEADS = 2
HEAD_DIM = 128
IDX_HEADS = 8
IDX_DIM = 64
TOPK = 256
ROPE_THETA = 500000.0
ROT_DIM = HEAD_DIM // 4
IDX_ROT_DIM = IDX_DIM // 4
D_FF = 2816
EPS = 1e-6
NEG_INF = -1e30

TOKENS = BATCH * SEQ
Q_TILE = 128
KEY_TILE = 512
IDX_K_LANES = 128
ACC_ROWS = 32
FIRST_CHECK_AFTER = 16
CHECK_EVERY = 2
SWEEP_UNROLL = (4, 2, 1)
TM_IN = 512
TM_OUT = 256
HEADS_PER_KV = N_HEADS // N_KV_HEADS
VMEM_LIMIT = 56 * 1024 * 1024

_OFF_U = 0
_OFF_V = _OFF_U + A_WIDTH
_OFF_Q = _OFF_V + A_WIDTH
_OFF_K = _OFF_Q + N_HEADS * HEAD_DIM
_OFF_VV = _OFF_K + N_KV_HEADS * HEAD_DIM
_OFF_QI = _OFF_VV + N_KV_HEADS * HEAD_DIM
_OFF_KI = _OFF_QI + IDX_HEADS * IDX_DIM
_OFF_WI = _OFF_KI + IDX_DIM
_OFF_GA = _OFF_WI + IDX_HEADS
_OFF_GB = _OFF_GA + D_MODEL
_IN_COLS = _OFF_GB + D_MODEL

_T_Q = 0
_T_K = _T_Q + N_HEADS * HEAD_DIM
_T_V = _T_K + N_KV_HEADS * HEAD_DIM
_T_QI = _T_V + N_KV_HEADS * HEAD_DIM
_T_KI = _T_QI + IDX_HEADS * IDX_DIM
_T_WI = _T_KI + IDX_DIM
_T_ROWS = _T_WI + IDX_HEADS

F32 = jnp.float32
BF16 = jnp.bfloat16


def _gelu_tanh(x):
    return 0.5 * x * (1.0 + jnp.tanh(np.sqrt(2.0 / np.pi).astype(np.float32) * (x + 0.044715 * (x * x * x))))


def _sigmoid(x):
    return 1.0 / (1.0 + jnp.exp(-x))


def _rms_norm(x, g):
    return x * lax.rsqrt(jnp.mean(x * x, axis=-1, keepdims=True) + EPS) * g


def _ada_kernel(c_ref, w_ref, b_ref, o_ref):
    c = c_ref[...]
    a = c * _sigmoid(c)
    w = w_ref[...]
    a_hi = a.astype(BF16)
    a_lo = (a - a_hi.astype(F32)).astype(BF16)
    w_hi = w.astype(BF16)
    w_lo = (w - w_hi.astype(F32)).astype(BF16)
    dot = functools.partial(jnp.dot, preferred_element_type=F32)
    o_ref[...] = dot(a_hi, w_hi) + (dot(a_hi, w_lo) + dot(a_lo, w_hi)) + b_ref[...]


def _ada_call(c, w_ada, b_ada):
    n_out = 6 * D_MODEL
    tn = 1024
    return pl.pallas_call(
        _ada_kernel,
        grid=(n_out // tn,),
        in_specs=[pl.BlockSpec((BATCH, D_MODEL), lambda j: (0, 0)),
                  pl.BlockSpec((D_MODEL, tn), lambda j: (0, j)),
                  pl.BlockSpec((1, tn), lambda j: (0, j))],
        out_specs=pl.BlockSpec((BATCH, tn), lambda j: (0, j)),
        out_shape=jax.ShapeDtypeStruct((BATCH, n_out), F32),
        compiler_params=pltpu.CompilerParams(dimension_semantics=("arbitrary",),
                                             vmem_limit_bytes=VMEM_LIMIT),
        name="ada_mod",
    )(c, w_ada, b_ada.reshape(1, n_out))


def _rope_rows(blk, cos, sin, half):
    x1 = blk[0:half]
    x2 = blk[half:2 * half]
    return x1 * cos - x2 * sin, x2 * cos + x1 * sin


def _inproj_kernel(x_ref, mod_ref, pos_ref, n1g_ref, wn_ref, wt_ref, lng_ref, lnb_ref,
                   kig_ref, kib_ref, invf_ref,
                   u_ref, vln_ref, sga_ref, sgb_ref, qT_ref, k_ref, vT_ref, qiT_ref, ki_ref, wiT_ref):
    tm = x_ref.shape[0]
    x = x_ref[...]
    mod = mod_ref[0]
    shift1 = mod[:, 0:D_MODEL]
    scale1 = mod[:, D_MODEL:2 * D_MODEL]
    h = _rms_norm(x, n1g_ref[...]) * (1.0 + scale1) + shift1
    hb = h.astype(BF16)

    zu = jnp.dot(hb, wn_ref[:, 0:A_WIDTH], preferred_element_type=F32)
    u_ref[...] = _gelu_tanh(zu).astype(BF16)
    zv = _gelu_tanh(jnp.dot(hb, wn_ref[:, A_WIDTH:2 * A_WIDTH], preferred_element_type=F32))
    mu = jnp.mean(zv, axis=-1, keepdims=True)
    zc = zv - mu
    var = jnp.mean(zc * zc, axis=-1, keepdims=True)
    vln_ref[...] = (zc * lax.rsqrt(var + EPS) * lng_ref[...] + lnb_ref[...]).astype(BF16)
    zga = jnp.dot(hb, wn_ref[:, 2 * A_WIDTH:2 * A_WIDTH + D_MODEL], preferred_element_type=F32)
    sga_ref[...] = _sigmoid(zga).astype(BF16)
    zgb = jnp.dot(hb, wn_ref[:, 2 * A_WIDTH + D_MODEL:], preferred_element_type=F32)
    sgb_ref[...] = _sigmoid(zgb).astype(BF16)

    nt = (((1,), (1,)), ((), ()))
    pos = pos_ref[0].astype(F32)
    ang = invf_ref[...] * pos
    cos = jnp.cos(ang)
    sin = jnp.sin(ang)
    hq = ROT_DIM // 2
    hi = IDX_ROT_DIM // 2
    cos_q, sin_q = cos[0:hq], sin[0:hq]
    cos_i, sin_i = cos[hq:hq + hi], sin[hq:hq + hi]
    n_sub = tm // Q_TILE

    zt = lax.dot_general(wt_ref[...], hb, nt, preferred_element_type=F32)
    zq = zt[_T_Q:_T_K]
    q_scale = HEAD_DIM ** -0.5 * float(np.log2(np.e))
    for hd in range(N_HEADS):
        blk = zq[hd * HEAD_DIM:(hd + 1) * HEAD_DIM]
        r1, r2 = _rope_rows(blk, cos_q, sin_q, hq)
        full = (jnp.concatenate([r1, r2, blk[ROT_DIM:]], axis=0) * q_scale).astype(BF16)
        for s in range(n_sub):
            qT_ref[s, :, hd * Q_TILE:(hd + 1) * Q_TILE] = full[:, s * Q_TILE:(s + 1) * Q_TILE]

    zk = zt[_T_K:_T_V]
    k_rows = []
    for g in range(N_KV_HEADS):
        blk = zk[g * HEAD_DIM:(g + 1) * HEAD_DIM]
        r1, r2 = _rope_rows(blk, cos_q, sin_q, hq)
        k_rows += [r1, r2, blk[ROT_DIM:]]
    k_ref[...] = jnp.concatenate(k_rows, axis=0).T.astype(BF16)

    vT_ref[...] = zt[_T_V:_T_QI].astype(BF16)

    zqi = zt[_T_QI:_T_KI]
    for hd in range(IDX_HEADS):
        blk = zqi[hd * IDX_DIM:(hd + 1) * IDX_DIM]
        r1, r2 = _rope_rows(blk, cos_i, sin_i, hi)
        full = jnp.concatenate([r1, r2, blk[IDX_ROT_DIM:]], axis=0).astype(BF16)
        for s in range(n_sub):
            qiT_ref[s, :, hd * Q_TILE:(hd + 1) * Q_TILE] = full[:, s * Q_TILE:(s + 1) * Q_TILE]

    zrest = zt[_T_KI:_T_ROWS]
    zki = zrest[0:IDX_DIM]
    kmu = jnp.mean(zki, axis=0, keepdims=True)
    kc = zki - kmu
    kvar = jnp.mean(kc * kc, axis=0, keepdims=True)
    kin = kc * lax.rsqrt(kvar + EPS) * kig_ref[...] + kib_ref[...]
    r1, r2 = _rope_rows(kin, cos_i, sin_i, hi)
    ki_full = jnp.concatenate([r1, r2, kin[IDX_ROT_DIM:], jnp.zeros((IDX_K_LANES - IDX_DIM, tm), F32)], axis=0)
    ki_ref[...] = ki_full.T.astype(BF16)
    wiT_ref[...] = zrest[IDX_DIM:IDX_DIM + IDX_HEADS] * ((IDX_HEADS ** -0.5) * (IDX_DIM ** -0.5))


def _inproj_call(x2, mod3, pos3, n1g, wn, wt, lng, lnb, kig, kib, invf):
    tm = TM_IN
    n_tiles = TOKENS // tm
    per_b = SEQ // tm
    n_sub = tm // Q_TILE
    const2 = lambda t: (0, 0)
    row = lambda t: (t, 0)
    in_specs = [
        pl.BlockSpec((tm, D_MODEL), row),
        pl.BlockSpec((1, 1, 6 * D_MODEL), lambda t: (t // per_b, 0, 0)),
        pl.BlockSpec((1, 1, tm), lambda t: (t // per_b, 0, t % per_b)),
        pl.BlockSpec((1, D_MODEL), const2),
        pl.BlockSpec(wn.shape, const2),
        pl.BlockSpec(wt.shape, const2),
        pl.BlockSpec((1, A_WIDTH), const2),
        pl.BlockSpec((1, A_WIDTH), const2),
        pl.BlockSpec((IDX_DIM, 1), const2),
        pl.BlockSpec((IDX_DIM, 1), const2),
        pl.BlockSpec(invf.shape, const2),
    ]
    out_shape = [
        jax.ShapeDtypeStruct((TOKENS, A_WIDTH), BF16),
        jax.ShapeDtypeStruct((TOKENS, A_WIDTH), BF16),
        jax.ShapeDtypeStruct((TOKENS, D_MODEL), BF16),
        jax.ShapeDtypeStruct((TOKENS, D_MODEL), BF16),
        jax.ShapeDtypeStruct((TOKENS // Q_TILE, HEAD_DIM, N_HEADS * Q_TILE), BF16),
        jax.ShapeDtypeStruct((TOKENS, N_KV_HEADS * HEAD_DIM), BF16),
        jax.ShapeDtypeStruct((N_KV_HEADS * HEAD_DIM, TOKENS), BF16),
        jax.ShapeDtypeStruct((TOKENS // Q_TILE, IDX_DIM, IDX_HEADS * Q_TILE), BF16),
        jax.ShapeDtypeStruct((TOKENS, IDX_K_LANES), BF16),
        jax.ShapeDtypeStruct((IDX_HEADS, TOKENS), F32),
    ]
    out_specs = [
        pl.BlockSpec((tm, A_WIDTH), row),
        pl.BlockSpec((tm, A_WIDTH), row),
        pl.BlockSpec((tm, D_MODEL), row),
        pl.BlockSpec((tm, D_MODEL), row),
        pl.BlockSpec((n_sub, HEAD_DIM, N_HEADS * Q_TILE), lambda t: (t, 0, 0)),
        pl.BlockSpec((tm, N_KV_HEADS * HEAD_DIM), row),
        pl.BlockSpec((N_KV_HEADS * HEAD_DIM, tm), lambda t: (0, t)),
        pl.BlockSpec((n_sub, IDX_DIM, IDX_HEADS * Q_TILE), lambda t: (t, 0, 0)),
        pl.BlockSpec((tm, IDX_K_LANES), row),
        pl.BlockSpec((IDX_HEADS, tm), lambda t: (0, t)),
    ]
    return pl.pallas_call(
        _inproj_kernel,
        grid=(n_tiles,),
        in_specs=in_specs,
        out_specs=out_specs,
        out_shape=out_shape,
        compiler_params=pltpu.CompilerParams(dimension_semantics=("arbitrary",),
                                             vmem_limit_bytes=VMEM_LIMIT),
        name="in_proj",
    )(x2, mod3, pos3, n1g, wn, wt, lng, lnb, kig, kib, invf)


def _col_reduce(x, op):
    return op(x.reshape(x.shape[0] // ACC_ROWS, ACC_ROWS, x.shape[1]), axis=0)


def _key_tiles(tile):
    return (tile * Q_TILE + Q_TILE + KEY_TILE - 1) // KEY_TILE


def _attn_kernel(qT_ref, qiT_ref, wiT_ref, k_ref, ki_ref, vT_ref, y_ref,
                 sc_ref, acc_ref, l_ref, vsel_ref):
    t = pl.program_id(1)
    nq = SEQ // Q_TILE
    i = jnp.minimum(t, nq - 1)
    ia = jnp.maximum(t - 1, 0)
    sc_i = sc_ref.at[i % 2]
    sc_a = sc_ref.at[ia % 2]
    nkt = jnp.where(t < nq, _key_tiles(i), 0)
    nka = jnp.where(t >= 1, _key_tiles(ia), 0)
    kf = float(TOPK)
    gw = HEADS_PER_KV * Q_TILE
    hw = N_HEADS * Q_TILE

    @pl.when(t == 0)
    def _():
        vsel_ref[...] = jnp.zeros_like(vsel_ref)

    qiT = qiT_ref[0]
    qT = qT_ref[0]
    w = wiT_ref[...]
    wrow = jnp.concatenate([w[hd:hd + 1, :] for hd in range(IDX_HEADS)], axis=1)
    vsel_a = vsel_ref[ia % 2]

    row_i = lax.broadcasted_iota(jnp.int32, (KEY_TILE, Q_TILE), 0)
    qidx = i * Q_TILE + lax.broadcasted_iota(jnp.int32, (KEY_TILE, Q_TILE), 1)

    def score_tile(j, mx_a, mn_a):
        sl = pl.ds(pl.multiple_of(j * KEY_TILE, KEY_TILE), KEY_TILE)
        lg = jnp.dot(ki_ref[sl, 0:IDX_DIM], qiT, preferred_element_type=F32)
        r = jnp.maximum(lg, 0.0) * wrow
        s = r[:, 0:Q_TILE]
        for hd in range(1, IDX_HEADS):
            s = s + r[:, hd * Q_TILE:(hd + 1) * Q_TILE]
        causal = j * KEY_TILE + row_i <= qidx
        sc_i[sl, :] = jnp.where(causal, s, -jnp.inf)
        mx_a = jnp.maximum(mx_a, _col_reduce(jnp.where(causal, s, -jnp.inf), jnp.max))
        mn_a = jnp.minimum(mn_a, _col_reduce(jnp.where(causal, s, jnp.inf), jnp.min))
        return mx_a, mn_a

    def key_mask(sl):
        return jnp.where(sc_a[sl, :] >= vsel_a, 0.0, NEG_INF)

    def qk_tile(j):
        sl = pl.ds(pl.multiple_of(j * KEY_TILE, KEY_TILE), KEY_TILE)
        return [jnp.dot(k_ref[sl, g * HEAD_DIM:(g + 1) * HEAD_DIM], qT[:, g * gw:(g + 1) * gw],
                        preferred_element_type=F32) for g in range(N_KV_HEADS)]

    def attend_tile(j, qk, l8, m):
        sl = pl.ds(pl.multiple_of(j * KEY_TILE, KEY_TILE), KEY_TILE)
        mask = key_mask(sl)
        parts = []
        for g in range(N_KV_HEADS):
            a = qk[g]
            ps = []
            for r in range(HEADS_PER_KV):
                e = a[:, r * Q_TILE:(r + 1) * Q_TILE] + mask
                if m is not None:
                    c0 = (g * HEADS_PER_KV + r) * Q_TILE
                    e = e - m[:, c0:c0 + Q_TILE]
                ps.append(jnp.exp2(e))
            p = jnp.concatenate(ps, axis=1)
            acc_ref[:, g * gw:(g + 1) * gw] += jnp.dot(
                vT_ref[g * HEAD_DIM:(g + 1) * HEAD_DIM, sl], p.astype(BF16), preferred_element_type=F32)
            parts.append(jnp.sum(p.reshape(KEY_TILE // 8, 8, gw), axis=0))
        return l8 + jnp.concatenate(parts, axis=1)

    acc_ref[...] = jnp.zeros_like(acc_ref)
    n_both = jnp.minimum(nkt, nka)

    def both_body(j, c):
        mx_a, mn_a, l8 = c
        mx_a, mn_a = score_tile(j, mx_a, mn_a)
        return mx_a, mn_a, attend_tile(j, qk_tile(j), l8, None)

    def score_body(j, c):
        mx_a, mn_a, l8 = c
        mx_a, mn_a = score_tile(j, mx_a, mn_a)
        return mx_a, mn_a, l8

    def attend_body(j, c):
        mx_a, mn_a, l8 = c
        return mx_a, mn_a, attend_tile(j, qk_tile(j), l8, None)

    carry = (jnp.full((ACC_ROWS, Q_TILE), -jnp.inf, F32), jnp.full((ACC_ROWS, Q_TILE), jnp.inf, F32),
             jnp.zeros((8, hw), F32))
    done = 0
    for width in SWEEP_UNROLL:
        def wide_body(jw, c, width=width, done=done):
            for u in range(width):
                c = both_body(done + jw * width + u, c)
            return c
        trips = (n_both - done) // width
        carry = lax.fori_loop(0, trips, wide_body, carry)
        done = done + trips * width
    carry = lax.fori_loop(n_both, nkt, score_body, carry)
    mx_a, mn_a, l8 = lax.fori_loop(n_both, nka, attend_body, carry)
    l_ref[...] = jnp.sum(l8, axis=0, keepdims=True)

    def tiles(fn, init):
        def body(j, carry):
            s = sc_i[pl.ds(pl.multiple_of(j * KEY_TILE, KEY_TILE), KEY_TILE), :]
            return fn(s, carry)
        return lax.fori_loop(0, nkt, body, init)

    def count_ge(t):
        c8 = tiles(lambda s, c: c + _col_reduce(jnp.where(s >= t, 1.0, 0.0), jnp.sum),
                   jnp.zeros((ACC_ROWS, Q_TILE), F32))
        return jnp.sum(c8, axis=0, keepdims=True)

    def bisect(state, n):
        def step(_, c):
            lo, hi, n_lo, n_hi = c
            mid = 0.5 * lo + 0.5 * hi
            n_mid = count_ge(mid)
            ok = n_mid >= kf
            return (jnp.where(ok, mid, lo), jnp.where(ok, hi, mid),
                    jnp.where(ok, n_mid, n_lo), jnp.where(ok, n_hi, n_mid))
        return lax.fori_loop(0, n, step, state)

    @pl.when(t < TOPK // Q_TILE)
    def _():
        vsel_ref[i % 2] = jnp.full((1, Q_TILE), jnp.finfo(jnp.float32).min, F32)

    @pl.when((t >= TOPK // Q_TILE) & (t < nq))
    def _():
        mx = jnp.max(mx_a, axis=0, keepdims=True)
        mn = jnp.min(mn_a, axis=0, keepdims=True)
        n_valid = (qidx[0:1, :] + 1).astype(F32)
        hi0 = mx + jnp.maximum(jnp.abs(mx), 1e-30) * 1e-6
        state = bisect((mn, hi0, n_valid, jnp.zeros((1, Q_TILE), F32)), FIRST_CHECK_AFTER - CHECK_EVERY)

        def check(state):
            lo, hi, n_lo, n_hi = state

            def f(s, c):
                a8, b8 = c
                a8 = jnp.maximum(a8, _col_reduce(jnp.where(s < hi, s, -jnp.inf), jnp.max))
                b8 = jnp.minimum(b8, _col_reduce(jnp.where(s >= lo, s, jnp.inf), jnp.min))
                return a8, b8
            a8, b8 = tiles(f, (jnp.full((ACC_ROWS, Q_TILE), -jnp.inf, F32), jnp.full((ACC_ROWS, Q_TILE), jnp.inf, F32)))
            top = jnp.max(a8, axis=0, keepdims=True)
            bottom = jnp.min(b8, axis=0, keepdims=True)
            single = top == bottom
            pinned = single | (n_lo - n_hi < 2.5)
            take_top = jnp.logical_not(single) & (kf - n_hi < 1.5)
            take_bottom = jnp.logical_not(single | take_top)
            v = jnp.where(take_top, top, bottom)
            n_ge = jnp.where(take_top, n_hi + 1.0, n_lo)
            n_gt = jnp.where(take_bottom, n_hi + 1.0, n_hi)
            return (v, n_ge, n_gt), jnp.min(jnp.where(pinned, 1.0, 0.0))

        def w_cond(c):
            return c[2] < 0.5

        def w_body(c):
            st = bisect(c[0], CHECK_EVERY)
            found, done = check(st)
            return st, found, done

        _, (vk, n_ge, n_gt), _ = lax.while_loop(w_cond, w_body, (state, (mn, n_valid, n_valid), jnp.float32(0.0)))
        need = kf - n_gt
        has_tie = jnp.max(n_ge) > kf + 0.5

        @pl.when(jnp.logical_not(has_tie))
        def _():
            vsel_ref[i % 2] = vk

        @pl.when(has_tie)
        def _():
            vsel_ref[i % 2] = jnp.full((1, Q_TILE), -0.5, F32)
            r_i = lax.broadcasted_iota(jnp.int32, (KEY_TILE, KEY_TILE), 0)
            c_i = lax.broadcasted_iota(jnp.int32, (KEY_TILE, KEY_TILE), 1)
            tri = jnp.where(r_i >= c_i, 1.0, 0.0).astype(BF16)

            def body(j, seen):
                sl = pl.ds(pl.multiple_of(j * KEY_TILE, KEY_TILE), KEY_TILE)
                s = sc_i[sl, :]
                eq = s == vk
                rank = jnp.dot(tri, jnp.where(eq, 1.0, 0.0).astype(BF16), preferred_element_type=F32) + seen
                sel = (s > vk) | (eq & (rank <= need))
                sc_i[sl, :] = jnp.where(sel, 0.0, NEG_INF)
                return rank[KEY_TILE - 1:KEY_TILE, :]
            lax.fori_loop(0, nkt, body, jnp.zeros((1, Q_TILE), F32))

    @pl.when(t >= 1)
    def _():
        l_fast = l_ref[...]
        in_range = (jnp.min(l_fast) > 1e-20) & (jnp.max(l_fast) < 1e30)

        @pl.when(jnp.logical_not(in_range))
        def _():
            def max_body(j, m8):
                sl = pl.ds(pl.multiple_of(j * KEY_TILE, KEY_TILE), KEY_TILE)
                mask = key_mask(sl)
                parts = []
                for g in range(N_KV_HEADS):
                    a = jnp.dot(k_ref[sl, g * HEAD_DIM:(g + 1) * HEAD_DIM], qT[:, g * gw:(g + 1) * gw],
                                preferred_element_type=F32)
                    a = a + jnp.concatenate([mask] * HEADS_PER_KV, axis=1)
                    parts.append(jnp.max(a.reshape(KEY_TILE // 8, 8, gw), axis=0))
                return jnp.maximum(m8, jnp.concatenate(parts, axis=1))
            m8 = lax.fori_loop(0, nka, max_body, jnp.full((8, hw), -jnp.inf, F32))
            m = jnp.max(m8, axis=0, keepdims=True)
            acc_ref[...] = jnp.zeros_like(acc_ref)
            l8 = lax.fori_loop(0, nka, lambda j, c: attend_tile(j, qk_tile(j), c, m), jnp.zeros((8, hw), F32))
            l_ref[...] = jnp.sum(l8, axis=0, keepdims=True)

        oT = acc_ref[...] / l_ref[...]
        for hd in range(N_HEADS):
            y_ref[:, hd * HEAD_DIM:(hd + 1) * HEAD_DIM] = oT[:, hd * Q_TILE:(hd + 1) * Q_TILE].T.astype(BF16)


def _attn_call(qT, qiT, wiT, k, ki, vT):
    nq = SEQ // Q_TILE
    scored = lambda b, t: b * nq + jnp.minimum(t, nq - 1)
    attended = lambda b, t: b * nq + jnp.maximum(t - 1, 0)
    return pl.pallas_call(
        _attn_kernel,
        grid=(BATCH, nq + 1),
        in_specs=[
            pl.BlockSpec((1, HEAD_DIM, N_HEADS * Q_TILE), lambda b, t: (attended(b, t), 0, 0)),
            pl.BlockSpec((1, IDX_DIM, IDX_HEADS * Q_TILE), lambda b, t: (scored(b, t), 0, 0)),
            pl.BlockSpec((IDX_HEADS, Q_TILE), lambda b, t: (0, scored(b, t))),
            pl.BlockSpec((SEQ, N_KV_HEADS * HEAD_DIM), lambda b, t: (b, 0)),
            pl.BlockSpec((SEQ, IDX_K_LANES), lambda b, t: (b, 0)),
            pl.BlockSpec((N_KV_HEADS * HEAD_DIM, SEQ), lambda b, t: (0, b)),
        ],
        out_specs=pl.BlockSpec((Q_TILE, N_HEADS * HEAD_DIM), lambda b, t: (attended(b, t), 0)),
        out_shape=jax.ShapeDtypeStruct((TOKENS, N_HEADS * HEAD_DIM), BF16),
        scratch_shapes=[pltpu.VMEM((2, SEQ, Q_TILE), F32),
                        pltpu.VMEM((HEAD_DIM, N_HEADS * Q_TILE), F32),
                        pltpu.VMEM((1, N_HEADS * Q_TILE), F32),
                        pltpu.VMEM((2, 1, Q_TILE), F32)],
        compiler_params=pltpu.CompilerParams(dimension_semantics=("arbitrary", "arbitrary"),
                                             vmem_limit_bytes=VMEM_LIMIT),
        name="dsa_attention",
    )(qT, qiT, wiT, k, ki, vT)


def _post_kernel(x_ref, mod_ref, u_ref, vln_ref, sga_ref, sgb_ref, yb_ref, ws_ref, bsT_ref,
                 wpa_ref, wpb_ref, wout_ref, n2g_ref, wfi_ref, wfo_ref, fg_ref, o_ref, ya_ref):
    tm = x_ref.shape[0]
    mod = mod_ref[0]
    gate1 = mod[:, 2 * D_MODEL:3 * D_MODEL]
    shift2 = mod[:, 3 * D_MODEL:4 * D_MODEL]
    scale2 = mod[:, 4 * D_MODEL:5 * D_MODEL]
    gate2 = mod[:, 5 * D_MODEL:6 * D_MODEL]

    r_i = lax.broadcasted_iota(jnp.int32, (CHUNK, CHUNK), 0)
    c_i = lax.broadcasted_iota(jnp.int32, (CHUNK, CHUNK), 1)
    bsT = bsT_ref[...]
    gd = A_WIDTH // A_GROUPS
    for g in range(A_GROUPS):
        wm = jnp.where(r_i >= c_i, ws_ref[g], 0.0).astype(BF16)
        bias = bsT[:, g:g + 1]
        for c in range(tm // CHUNK):
            rows = slice(c * CHUNK, (c + 1) * CHUNK)
            cols = slice(g * gd, (g + 1) * gd)
            mixed = jnp.dot(wm, vln_ref[rows, cols], preferred_element_type=F32) + bias
            ya_ref[rows, cols] = (u_ref[rows, cols].astype(F32) * mixed).astype(BF16)

    pa = jnp.dot(ya_ref[...], wpa_ref[...], preferred_element_type=F32)
    pb = jnp.dot(yb_ref[...], wpb_ref[...], preferred_element_type=F32)
    merged = sga_ref[...].astype(F32) * pa + sgb_ref[...].astype(F32) * pb
    x1 = x_ref[...] + gate1 * jnp.dot(merged.astype(BF16), wout_ref[...], preferred_element_type=F32)

    h2 = (_rms_norm(x1, n2g_ref[...]) * (1.0 + scale2) + shift2).astype(BF16)
    fg = jnp.dot(h2, wfi_ref[:, 0:D_FF], preferred_element_type=F32)
    fu = jnp.dot(h2, wfi_ref[:, D_FF:], preferred_element_type=F32)
    act = (fg * _sigmoid(fg) * fu).astype(BF16)
    x2 = x1 + gate2 * jnp.dot(act, wfo_ref[...], preferred_element_type=F32)
    o_ref[...] = _rms_norm(x2, fg_ref[...])


def _post_call(x2, mod3, u, vln, sga, sgb, yb, ws, bsT, wpa, wpb, wout, n2g, wfi, wfo, fg):
    tm = TM_OUT
    per_b = SEQ // tm
    row = lambda t: (t, 0)
    const2 = lambda t: (0, 0)
    tok = lambda: pl.BlockSpec((tm, D_MODEL), row)
    in_specs = [
        tok(),
        pl.BlockSpec((1, 1, 6 * D_MODEL), lambda t: (t // per_b, 0, 0)),
        tok(), tok(), tok(), tok(), tok(),
        pl.BlockSpec(ws.shape, lambda t: (0, 0, 0)),
        pl.BlockSpec(bsT.shape, const2),
        pl.BlockSpec(wpa.shape, const2),
        pl.BlockSpec(wpb.shape, const2),
        pl.BlockSpec(wout.shape, const2),
        pl.BlockSpec((1, D_MODEL), const2),
        pl.BlockSpec(wfi.shape, const2),
        pl.BlockSpec(wfo.shape, const2),
        pl.BlockSpec((1, D_MODEL), const2),
    ]
    return pl.pallas_call(
        _post_kernel,
        grid=(TOKENS // tm,),
        in_specs=in_specs,
        out_specs=pl.BlockSpec((tm, D_MODEL), row),
        out_shape=jax.ShapeDtypeStruct((TOKENS, D_MODEL), F32),
        scratch_shapes=[pltpu.VMEM((tm, A_WIDTH), BF16)],
        compiler_params=pltpu.CompilerParams(dimension_semantics=("arbitrary",),
                                             vmem_limit_bytes=VMEM_LIMIT),
        name="merge_ffn",
    )(x2, mod3, u, vln, sga, sgb, yb, ws, bsT, wpa, wpb, wout, n2g, wfi, wfo, fg)


def _inv_freq_table():
    fq = ROPE_THETA ** (-np.arange(0, ROT_DIM, 2, dtype=np.float32) / ROT_DIM)
    fi = ROPE_THETA ** (-np.arange(0, IDX_ROT_DIM, 2, dtype=np.float32) / IDX_ROT_DIM)
    return np.concatenate([fq, fi]).astype(np.float32).reshape(-1, 1)


def kernel(x, c, positions, w_ada, b_ada, norm1_g, w_in, gmlp_ln_g, gmlp_ln_b, gmlp_w_s, gmlp_b_s,
           idx_k_ln_g, idx_k_ln_b, w_proj_a, w_proj_b, w_out, norm2_g, w_ffn_in, w_ffn_out, final_norm_g):
    assert x.shape == (BATCH, SEQ, D_MODEL) and w_in.shape == (1, D_MODEL, _IN_COLS)
    x2 = x.reshape(TOKENS, D_MODEL)
    pos3 = positions.reshape(BATCH, 1, SEQ)
    invf = jnp.asarray(_inv_freq_table())
    xcur = x2
    for l in range(w_ada.shape[0]):
        w = w_in[l]
        wn = jnp.concatenate([w[:, _OFF_U:_OFF_Q], w[:, _OFF_GA:]], axis=1).astype(BF16)
        wt = w[:, _OFF_Q:_OFF_GA].astype(BF16).T
        mod = _ada_call(c, w_ada[l], b_ada[l])
        mod3 = mod.reshape(BATCH, 1, 6 * D_MODEL)
        u, vln, sga, sgb, qT, k, vT, qiT, ki, wiT = _inproj_call(
            xcur, mod3, pos3, norm1_g[l].reshape(1, -1), wn, wt,
            gmlp_ln_g[l].reshape(1, -1), gmlp_ln_b[l].reshape(1, -1),
            idx_k_ln_g[l].reshape(-1, 1), idx_k_ln_b[l].reshape(-1, 1), invf)
        yb = _attn_call(qT, qiT, wiT, k, ki, vT)
        last = l == w_ada.shape[0] - 1
        fg = final_norm_g.reshape(1, -1) if last else None
        assert last, "single-layer block"
        xcur = _post_call(xcur, mod3, u, vln, sga, sgb, yb, gmlp_w_s[l], gmlp_b_s[l].T,
                          w_proj_a[l].astype(BF16), w_proj_b[l].astype(BF16), w_out[l].astype(BF16),
                          norm2_g[l].reshape(1, -1), w_ffn_in[l].astype(BF16), w_ffn_out[l].astype(BF16), fg)
    return xcur.reshape(BATCH, SEQ, D_MODEL)
```

```python
import functools

import numpy as np
import jax
import jax.numpy as jnp
from jax import lax
from jax.experimental import pallas as pl
from jax.experimental.pallas import tpu as pltpu

D_MODEL = 1024
BATCH = 4
SEQ = 4096
CHUNK = 128
A_GROUPS = 8
A_WIDTH = 1024
N_HEADS = 8
N_KV_HEADS = 2
HEAD_DIM = 128
IDX_HEADS = 8
IDX_DIM = 64
TOPK = 256
ROPE_THETA = 500000.0
ROT_DIM = HEAD_DIM // 4
IDX_ROT_DIM = IDX_DIM // 4
D_FF = 2816
EPS = 1e-6
NEG_INF = -1e30

TOKENS = BATCH * SEQ
Q_TILE = 128
KEY_TILE = 512
IDX_K_LANES = 128
ACC_ROWS = 32
FIRST_CHECK_AFTER = 16
CHECK_EVERY = 2
SWEEP_UNROLL = (4, 2, 1)
TM_IN = 512
TM_OUT = 256
HEADS_PER_KV = N_HEADS // N_KV_HEADS
VMEM_LIMIT = 56 * 1024 * 1024

_OFF_U = 0
_OFF_V = _OFF_U + A_WIDTH
_OFF_Q = _OFF_V + A_WIDTH
_OFF_K = _OFF_Q + N_HEADS * HEAD_DIM
_OFF_VV = _OFF_K + N_KV_HEADS * HEAD_DIM
_OFF_QI = _OFF_VV + N_KV_HEADS * HEAD_DIM
_OFF_KI = _OFF_QI + IDX_HEADS * IDX_DIM
_OFF_WI = _OFF_KI + IDX_DIM
_OFF_GA = _OFF_WI + IDX_HEADS
_OFF_GB = _OFF_GA + D_MODEL
_IN_COLS = _OFF_GB + D_MODEL

_T_Q = 0
_T_K = _T_Q + N_HEADS * HEAD_DIM
_T_V = _T_K + N_KV_HEADS * HEAD_DIM
_T_QI = _T_V + N_KV_HEADS * HEAD_DIM
_T_KI = _T_QI + IDX_HEADS * IDX_DIM
_T_WI = _T_KI + IDX_DIM
_T_ROWS = _T_WI + IDX_HEADS
_T_PAD = -(-_T_ROWS // 128) * 128

F32 = jnp.float32
BF16 = jnp.bfloat16


def _gelu_tanh(x):
    return 0.5 * x * (1.0 + jnp.tanh(np.sqrt(2.0 / np.pi).astype(np.float32) * (x + 0.044715 * (x * x * x))))


def _sigmoid(x):
    return 1.0 / (1.0 + jnp.exp(-x))


def _rms_norm(x, g):
    return x * lax.rsqrt(jnp.mean(x * x, axis=-1, keepdims=True) + EPS) * g


def _ada_kernel(c_ref, w_ref, b_ref, o_ref):
    c = c_ref[...]
    a = c * _sigmoid(c)
    w = w_ref[...]
    a_hi = a.astype(BF16)
    a_lo = (a - a_hi.astype(F32)).astype(BF16)
    w_hi = w.astype(BF16)
    w_lo = (w - w_hi.astype(F32)).astype(BF16)
    dot = functools.partial(jnp.dot, preferred_element_type=F32)
    o_ref[...] = dot(a_hi, w_hi) + (dot(a_hi, w_lo) + dot(a_lo, w_hi)) + b_ref[...]


def _ada_call(c, w_ada, b_ada):
    n_out = 6 * D_MODEL
    tn = 1024
    return pl.pallas_call(
        _ada_kernel,
        grid=(n_out // tn,),
        in_specs=[pl.BlockSpec((BATCH, D_MODEL), lambda j: (0, 0)),
                  pl.BlockSpec((D_MODEL, tn), lambda j: (0, j)),
                  pl.BlockSpec((1, tn), lambda j: (0, j))],
        out_specs=pl.BlockSpec((BATCH, tn), lambda j: (0, j)),
        out_shape=jax.ShapeDtypeStruct((BATCH, n_out), F32),
        compiler_params=pltpu.CompilerParams(dimension_semantics=("arbitrary",),
                                             vmem_limit_bytes=VMEM_LIMIT),
        name="ada_mod",
    )(c, w_ada, b_ada.reshape(1, n_out))


def _rope_rows(blk, cos, sin, half):
    x1 = blk[0:half]
    x2 = blk[half:2 * half]
    return x1 * cos - x2 * sin, x2 * cos + x1 * sin


def _inproj_kernel(x_ref, mod_ref, pos_ref, n1g_ref, w_ref, lng_ref, lnb_ref,
                   kig_ref, kib_ref, invf_ref,
                   u_ref, vln_ref, sga_ref, sgb_ref, qT_ref, k_ref, vT_ref, qiT_ref, ki_ref, wiT_ref,
                   wg_ref, wt_ref):
    tm = x_ref.shape[0]

    @pl.when(pl.program_id(0) == 0)
    def _():
        wg_ref[:, 0:D_MODEL] = w_ref[:, _OFF_GA:_OFF_GB]
        wg_ref[:, D_MODEL:] = w_ref[:, _OFF_GB:_IN_COLS]
        for c in range(_T_PAD // 128):
            cols = slice(_OFF_Q + c * 128, _OFF_Q + (c + 1) * 128)
            wt_ref[c * 128:(c + 1) * 128, :] = w_ref[:, cols].astype(F32).T.astype(BF16)

    x = x_ref[...]
    mod = mod_ref[0]
    shift1 = mod[:, 0:D_MODEL]
    scale1 = mod[:, D_MODEL:2 * D_MODEL]
    h = _rms_norm(x, n1g_ref[...]) * (1.0 + scale1) + shift1
    hb = h.astype(BF16)

    zu = jnp.dot(hb, w_ref[:, _OFF_U:_OFF_V], preferred_element_type=F32)
    u_ref[...] = _gelu_tanh(zu).astype(BF16)
    zv = _gelu_tanh(jnp.dot(hb, w_ref[:, _OFF_V:_OFF_Q], preferred_element_type=F32))
    mu = jnp.mean(zv, axis=-1, keepdims=True)
    zc = zv - mu
    var = jnp.mean(zc * zc, axis=-1, keepdims=True)
    vln_ref[...] = (zc * lax.rsqrt(var + EPS) * lng_ref[...] + lnb_ref[...]).astype(BF16)
    zga = jnp.dot(hb, wg_ref[:, 0:D_MODEL], preferred_element_type=F32)
    sga_ref[...] = _sigmoid(zga).astype(BF16)
    zgb = jnp.dot(hb, wg_ref[:, D_MODEL:], preferred_element_type=F32)
    sgb_ref[...] = _sigmoid(zgb).astype(BF16)

    nt = (((1,), (1,)), ((), ()))
    pos = pos_ref[0].astype(F32)
    ang = invf_ref[...] * pos
    cos = jnp.cos(ang)
    sin = jnp.sin(ang)
    hq = ROT_DIM // 2
    hi = IDX_ROT_DIM // 2
    cos_q, sin_q = cos[0:hq], sin[0:hq]
    cos_i, sin_i = cos[hq:hq + hi], sin[hq:hq + hi]
    n_sub = tm // Q_TILE

    zt = lax.dot_general(wt_ref[0:_T_ROWS, :], hb, nt, preferred_element_type=F32)
    zq = zt[_T_Q:_T_K]
    q_scale = HEAD_DIM ** -0.5 * float(np.log2(np.e))
    for hd in range(N_HEADS):
        blk = zq[hd * HEAD_DIM:(hd + 1) * HEAD_DIM]
        r1, r2 = _rope_rows(blk, cos_q, sin_q, hq)
        full = (jnp.concatenate([r1, r2, blk[ROT_DIM:]], axis=0) * q_scale).astype(BF16)
        for s in range(n_sub):
            qT_ref[s, :, hd * Q_TILE:(hd + 1) * Q_TILE] = full[:, s * Q_TILE:(s + 1) * Q_TILE]

    zk = zt[_T_K:_T_V]
    k_rows = []
    for g in range(N_KV_HEADS):
        blk = zk[g * HEAD_DIM:(g + 1) * HEAD_DIM]
        r1, r2 = _rope_rows(blk, cos_q, sin_q, hq)
        k_rows += [r1, r2, blk[ROT_DIM:]]
    k_ref[...] = jnp.concatenate(k_rows, axis=0).T.astype(BF16)

    vT_ref[...] = zt[_T_V:_T_QI].astype(BF16)

    zqi = zt[_T_QI:_T_KI]
    for hd in range(IDX_HEADS):
        blk = zqi[hd * IDX_DIM:(hd + 1) * IDX_DIM]
        r1, r2 = _rope_rows(blk, cos_i, sin_i, hi)
        full = jnp.concatenate([r1, r2, blk[IDX_ROT_DIM:]], axis=0).astype(BF16)
        for s in range(n_sub):
            qiT_ref[s, :, hd * Q_TILE:(hd + 1) * Q_TILE] = full[:, s * Q_TILE:(s + 1) * Q_TILE]

    zrest = zt[_T_KI:_T_ROWS]
    zki = zrest[0:IDX_DIM]
    kmu = jnp.mean(zki, axis=0, keepdims=True)
    kc = zki - kmu
    kvar = jnp.mean(kc * kc, axis=0, keepdims=True)
    kin = kc * lax.rsqrt(kvar + EPS) * kig_ref[...] + kib_ref[...]
    r1, r2 = _rope_rows(kin, cos_i, sin_i, hi)
    ki_full = jnp.concatenate([r1, r2, kin[IDX_ROT_DIM:], jnp.zeros((IDX_K_LANES - IDX_DIM, tm), F32)], axis=0)
    ki_ref[...] = ki_full.T.astype(BF16)
    wiT_ref[...] = zrest[IDX_DIM:IDX_DIM + IDX_HEADS] * ((IDX_HEADS ** -0.5) * (IDX_DIM ** -0.5))


def _inproj_call(x2, mod3, pos3, n1g, w, lng, lnb, kig, kib, invf):
    tm = TM_IN
    n_tiles = TOKENS // tm
    per_b = SEQ // tm
    n_sub = tm // Q_TILE
    const2 = lambda t: (0, 0)
    row = lambda t: (t, 0)
    in_specs = [
        pl.BlockSpec((tm, D_MODEL), row),
        pl.BlockSpec((1, 1, 6 * D_MODEL), lambda t: (t // per_b, 0, 0)),
        pl.BlockSpec((1, 1, tm), lambda t: (t // per_b, 0, t % per_b)),
        pl.BlockSpec((1, D_MODEL), const2),
        pl.BlockSpec(w.shape, const2),
        pl.BlockSpec((1, A_WIDTH), const2),
        pl.BlockSpec((1, A_WIDTH), const2),
        pl.BlockSpec((IDX_DIM, 1), const2),
        pl.BlockSpec((IDX_DIM, 1), const2),
        pl.BlockSpec(invf.shape, const2),
    ]
    out_shape = [
        jax.ShapeDtypeStruct((TOKENS, A_WIDTH), BF16),
        jax.ShapeDtypeStruct((TOKENS, A_WIDTH), BF16),
        jax.ShapeDtypeStruct((TOKENS, D_MODEL), BF16),
        jax.ShapeDtypeStruct((TOKENS, D_MODEL), BF16),
        jax.ShapeDtypeStruct((TOKENS // Q_TILE, HEAD_DIM, N_HEADS * Q_TILE), BF16),
        jax.ShapeDtypeStruct((TOKENS, N_KV_HEADS * HEAD_DIM), BF16),
        jax.ShapeDtypeStruct((N_KV_HEADS * HEAD_DIM, TOKENS), BF16),
        jax.ShapeDtypeStruct((TOKENS // Q_TILE, IDX_DIM, IDX_HEADS * Q_TILE), BF16),
        jax.ShapeDtypeStruct((TOKENS, IDX_K_LANES), BF16),
        jax.ShapeDtypeStruct((IDX_HEADS, TOKENS), F32),
    ]
    out_specs = [
        pl.BlockSpec((tm, A_WIDTH), row),
        pl.BlockSpec((tm, A_WIDTH), row),
        pl.BlockSpec((tm, D_MODEL), row),
        pl.BlockSpec((tm, D_MODEL), row),
        pl.BlockSpec((n_sub, HEAD_DIM, N_HEADS * Q_TILE), lambda t: (t, 0, 0)),
        pl.BlockSpec((tm, N_KV_HEADS * HEAD_DIM), row),
        pl.BlockSpec((N_KV_HEADS * HEAD_DIM, tm), lambda t: (0, t)),
        pl.BlockSpec((n_sub, IDX_DIM, IDX_HEADS * Q_TILE), lambda t: (t, 0, 0)),
        pl.BlockSpec((tm, IDX_K_LANES), row),
        pl.BlockSpec((IDX_HEADS, tm), lambda t: (0, t)),
    ]
    return pl.pallas_call(
        _inproj_kernel,
        grid=(n_tiles,),
        in_specs=in_specs,
        out_specs=out_specs,
        out_shape=out_shape,
        scratch_shapes=[pltpu.VMEM((D_MODEL, 2 * D_MODEL), BF16),
                        pltpu.VMEM((_T_PAD, D_MODEL), BF16)],
        compiler_params=pltpu.CompilerParams(dimension_semantics=("arbitrary",),
                                             vmem_limit_bytes=VMEM_LIMIT),
        name="in_proj",
    )(x2, mod3, pos3, n1g, w, lng, lnb, kig, kib, invf)


def _col_reduce(x, op):
    return op(x.reshape(x.shape[0] // ACC_ROWS, ACC_ROWS, x.shape[1]), axis=0)


def _key_tiles(tile):
    return (tile * Q_TILE + Q_TILE + KEY_TILE - 1) // KEY_TILE


def _attn_kernel(qT_ref, qiT_ref, wiT_ref, k_ref, ki_ref, vT_ref, y_ref,
                 sc_ref, acc_ref, l_ref, vsel_ref):
    t = pl.program_id(1)
    nq = SEQ // Q_TILE
    i = jnp.minimum(t, nq - 1)
    ia = jnp.maximum(t - 1, 0)
    sc_i = sc_ref.at[i % 2]
    sc_a = sc_ref.at[ia % 2]
    nkt = jnp.where(t < nq, _key_tiles(i), 0)
    nka = jnp.where(t >= 1, _key_tiles(ia), 0)
    kf = float(TOPK)
    gw = HEADS_PER_KV * Q_TILE
    hw = N_HEADS * Q_TILE

    @pl.when(t == 0)
    def _():
        vsel_ref[...] = jnp.zeros_like(vsel_ref)

    qiT = qiT_ref[0]
    qT = qT_ref[0]
    w = wiT_ref[...]
    wrow = jnp.concatenate([w[hd:hd + 1, :] for hd in range(IDX_HEADS)], axis=1)
    vsel_a = vsel_ref[ia % 2]

    row_i = lax.broadcasted_iota(jnp.int32, (KEY_TILE, Q_TILE), 0)
    qidx = i * Q_TILE + lax.broadcasted_iota(jnp.int32, (KEY_TILE, Q_TILE), 1)

    def score_tile(j, mx_a, mn_a):
        sl = pl.ds(pl.multiple_of(j * KEY_TILE, KEY_TILE), KEY_TILE)
        lg = jnp.dot(ki_ref[sl, 0:IDX_DIM], qiT, preferred_element_type=F32)
        r = jnp.maximum(lg, 0.0) * wrow
        s = r[:, 0:Q_TILE]
        for hd in range(1, IDX_HEADS):
            s = s + r[:, hd * Q_TILE:(hd + 1) * Q_TILE]
        causal = j * KEY_TILE + row_i <= qidx
        sc_i[sl, :] = jnp.where(causal, s, -jnp.inf)
        mx_a = jnp.maximum(mx_a, _col_reduce(jnp.where(causal, s, -jnp.inf), jnp.max))
        mn_a = jnp.minimum(mn_a, _col_reduce(jnp.where(causal, s, jnp.inf), jnp.min))
        return mx_a, mn_a

    def key_mask(sl):
        return jnp.where(sc_a[sl, :] >= vsel_a, 0.0, NEG_INF)

    def qk_tile(j):
        sl = pl.ds(pl.multiple_of(j * KEY_TILE, KEY_TILE), KEY_TILE)
        return [jnp.dot(k_ref[sl, g * HEAD_DIM:(g + 1) * HEAD_DIM], qT[:, g * gw:(g + 1) * gw],
                        preferred_element_type=F32) for g in range(N_KV_HEADS)]

    def attend_tile(j, qk, l8, m):
        sl = pl.ds(pl.multiple_of(j * KEY_TILE, KEY_TILE), KEY_TILE)
        mask = key_mask(sl)
        parts = []
        for g in range(N_KV_HEADS):
            a = qk[g]
            ps = []
            for r in range(HEADS_PER_KV):
                e = a[:, r * Q_TILE:(r + 1) * Q_TILE] + mask
                if m is not None:
                    c0 = (g * HEADS_PER_KV + r) * Q_TILE
                    e = e - m[:, c0:c0 + Q_TILE]
                ps.append(jnp.exp2(e))
            p = jnp.concatenate(ps, axis=1)
            acc_ref[:, g * gw:(g + 1) * gw] += jnp.dot(
                vT_ref[g * HEAD_DIM:(g + 1) * HEAD_DIM, sl], p.astype(BF16), preferred_element_type=F32)
            parts.append(jnp.sum(p.reshape(KEY_TILE // 8, 8, gw), axis=0))
        return l8 + jnp.concatenate(parts, axis=1)

    acc_ref[...] = jnp.zeros_like(acc_ref)
    n_both = jnp.minimum(nkt, nka)

    def both_body(j, c):
        mx_a, mn_a, l8 = c
        mx_a, mn_a = score_tile(j, mx_a, mn_a)
        return mx_a, mn_a, attend_tile(j, qk_tile(j), l8, None)

    def score_body(j, c):
        mx_a, mn_a, l8 = c
        mx_a, mn_a = score_tile(j, mx_a, mn_a)
        return mx_a, mn_a, l8

    def attend_body(j, c):
        mx_a, mn_a, l8 = c
        return mx_a, mn_a, attend_tile(j, qk_tile(j), l8, None)

    carry = (jnp.full((ACC_ROWS, Q_TILE), -jnp.inf, F32), jnp.full((ACC_ROWS, Q_TILE), jnp.inf, F32),
             jnp.zeros((8, hw), F32))
    done = 0
    for width in SWEEP_UNROLL:
        def wide_body(jw, c, width=width, done=done):
            for u in range(width):
                c = both_body(done + jw * width + u, c)
            return c
        trips = (n_both - done) // width
        carry = lax.fori_loop(0, trips, wide_body, carry)
        done = done + trips * width
    carry = lax.fori_loop(n_both, nkt, score_body, carry)
    mx_a, mn_a, l8 = lax.fori_loop(n_both, nka, attend_body, carry)
    l_ref[...] = jnp.sum(l8, axis=0, keepdims=True)

    def tiles(fn, init):
        def body(j, carry):
            s = sc_i[pl.ds(pl.multiple_of(j * KEY_TILE, KEY_TILE), KEY_TILE), :]
            return fn(s, carry)
        return lax.fori_loop(0, nkt, body, init)

    def count_ge(t):
        c8 = tiles(lambda s, c: c + _col_reduce(jnp.where(s >= t, 1.0, 0.0), jnp.sum),
                   jnp.zeros((ACC_ROWS, Q_TILE), F32))
        return jnp.sum(c8, axis=0, keepdims=True)

    def bisect(state, n):
        def step(_, c):
            lo, hi, n_lo, n_hi = c
            mid = 0.5 * lo + 0.5 * hi
            n_mid = count_ge(mid)
            ok = n_mid >= kf
            return (jnp.where(ok, mid, lo), jnp.where(ok, hi, mid),
                    jnp.where(ok, n_mid, n_lo), jnp.where(ok, n_hi, n_mid))
        return lax.fori_loop(0, n, step, state)

    @pl.when(t < TOPK // Q_TILE)
    def _():
        vsel_ref[i % 2] = jnp.full((1, Q_TILE), jnp.finfo(jnp.float32).min, F32)

    @pl.when((t >= TOPK // Q_TILE) & (t < nq))
    def _():
        mx = jnp.max(mx_a, axis=0, keepdims=True)
        mn = jnp.min(mn_a, axis=0, keepdims=True)
        n_valid = (qidx[0:1, :] + 1).astype(F32)
        hi0 = mx + jnp.maximum(jnp.abs(mx), 1e-30) * 1e-6
        state = bisect((mn, hi0, n_valid, jnp.zeros((1, Q_TILE), F32)), FIRST_CHECK_AFTER - CHECK_EVERY)

        def check(state):
            lo, hi, n_lo, n_hi = state

            def f(s, c):
                a8, b8 = c
                a8 = jnp.maximum(a8, _col_reduce(jnp.where(s < hi, s, -jnp.inf), jnp.max))
                b8 = jnp.minimum(b8, _col_reduce(jnp.where(s >= lo, s, jnp.inf), jnp.min))
                return a8, b8
            a8, b8 = tiles(f, (jnp.full((ACC_ROWS, Q_TILE), -jnp.inf, F32), jnp.full((ACC_ROWS, Q_TILE), jnp.inf, F32)))
            top = jnp.max(a8, axis=0, keepdims=True)
            bottom = jnp.min(b8, axis=0, keepdims=True)
            single = top == bottom
            pinned = single | (n_lo - n_hi < 2.5)
            take_top = jnp.logical_not(single) & (kf - n_hi < 1.5)
            take_bottom = jnp.logical_not(single | take_top)
            v = jnp.where(take_top, top, bottom)
            n_ge = jnp.where(take_top, n_hi + 1.0, n_lo)
            n_gt = jnp.where(take_bottom, n_hi + 1.0, n_hi)
            return (v, n_ge, n_gt), jnp.min(jnp.where(pinned, 1.0, 0.0))

        def w_cond(c):
            return c[2] < 0.5

        def w_body(c):
            st = bisect(c[0], CHECK_EVERY)
            found, done = check(st)
            return st, found, done

        _, (vk, n_ge, n_gt), _ = lax.while_loop(w_cond, w_body, (state, (mn, n_valid, n_valid), jnp.float32(0.0)))
        need = kf - n_gt
        has_tie = jnp.max(n_ge) > kf + 0.5

        @pl.when(jnp.logical_not(has_tie))
        def _():
            vsel_ref[i % 2] = vk

        @pl.when(has_tie)
        def _():
            vsel_ref[i % 2] = jnp.full((1, Q_TILE), -0.5, F32)
            r_i = lax.broadcasted_iota(jnp.int32, (KEY_TILE, KEY_TILE), 0)
            c_i = lax.broadcasted_iota(jnp.int32, (KEY_TILE, KEY_TILE), 1)
            tri = jnp.where(r_i >= c_i, 1.0, 0.0).astype(BF16)

            def body(j, seen):
                sl = pl.ds(pl.multiple_of(j * KEY_TILE, KEY_TILE), KEY_TILE)
                s = sc_i[sl, :]
                eq = s == vk
                rank = jnp.dot(tri, jnp.where(eq, 1.0, 0.0).astype(BF16), preferred_element_type=F32) + seen
                sel = (s > vk) | (eq & (rank <= need))
                sc_i[sl, :] = jnp.where(sel, 0.0, NEG_INF)
                return rank[KEY_TILE - 1:KEY_TILE, :]
            lax.fori_loop(0, nkt, body, jnp.zeros((1, Q_TILE), F32))

    @pl.when(t >= 1)
    def _():
        l_fast = l_ref[...]
        in_range = (jnp.min(l_fast) > 1e-20) & (jnp.max(l_fast) < 1e30)

        @pl.when(jnp.logical_not(in_range))
        def _():
            def max_body(j, m8):
                sl = pl.ds(pl.multiple_of(j * KEY_TILE, KEY_TILE), KEY_TILE)
                mask = key_mask(sl)
                parts = []
                for g in range(N_KV_HEADS):
                    a = jnp.dot(k_ref[sl, g * HEAD_DIM:(g + 1) * HEAD_DIM], qT[:, g * gw:(g + 1) * gw],
                                preferred_element_type=F32)
                    a = a + jnp.concatenate([mask] * HEADS_PER_KV, axis=1)
                    parts.append(jnp.max(a.reshape(KEY_TILE // 8, 8, gw), axis=0))
                return jnp.maximum(m8, jnp.concatenate(parts, axis=1))
            m8 = lax.fori_loop(0, nka, max_body, jnp.full((8, hw), -jnp.inf, F32))
            m = jnp.max(m8, axis=0, keepdims=True)
            acc_ref[...] = jnp.zeros_like(acc_ref)
            l8 = lax.fori_loop(0, nka, lambda j, c: attend_tile(j, qk_tile(j), c, m), jnp.zeros((8, hw), F32))
            l_ref[...] = jnp.sum(l8, axis=0, keepdims=True)

        oT = acc_ref[...] / l_ref[...]
        for hd in range(N_HEADS):
            y_ref[:, hd * HEAD_DIM:(hd + 1) * HEAD_DIM] = oT[:, hd * Q_TILE:(hd + 1) * Q_TILE].T.astype(BF16)


def _attn_call(qT, qiT, wiT, k, ki, vT):
    nq = SEQ // Q_TILE
    scored = lambda b, t: b * nq + jnp.minimum(t, nq - 1)
    attended = lambda b, t: b * nq + jnp.maximum(t - 1, 0)
    return pl.pallas_call(
        _attn_kernel,
        grid=(BATCH, nq + 1),
        in_specs=[
            pl.BlockSpec((1, HEAD_DIM, N_HEADS * Q_TILE), lambda b, t: (attended(b, t), 0, 0)),
            pl.BlockSpec((1, IDX_DIM, IDX_HEADS * Q_TILE), lambda b, t: (scored(b, t), 0, 0)),
            pl.BlockSpec((IDX_HEADS, Q_TILE), lambda b, t: (0, scored(b, t))),
            pl.BlockSpec((SEQ, N_KV_HEADS * HEAD_DIM), lambda b, t: (b, 0)),
            pl.BlockSpec((SEQ, IDX_K_LANES), lambda b, t: (b, 0)),
            pl.BlockSpec((N_KV_HEADS * HEAD_DIM, SEQ), lambda b, t: (0, b)),
        ],
        out_specs=pl.BlockSpec((Q_TILE, N_HEADS * HEAD_DIM), lambda b, t: (attended(b, t), 0)),
        out_shape=jax.ShapeDtypeStruct((TOKENS, N_HEADS * HEAD_DIM), BF16),
        scratch_shapes=[pltpu.VMEM((2, SEQ, Q_TILE), F32),
                        pltpu.VMEM((HEAD_DIM, N_HEADS * Q_TILE), F32),
                        pltpu.VMEM((1, N_HEADS * Q_TILE), F32),
                        pltpu.VMEM((2, 1, Q_TILE), F32)],
        compiler_params=pltpu.CompilerParams(dimension_semantics=("arbitrary", "arbitrary"),
                                             vmem_limit_bytes=VMEM_LIMIT),
        name="dsa_attention",
    )(qT, qiT, wiT, k, ki, vT)


def _post_kernel(x_ref, mod_ref, u_ref, vln_ref, sga_ref, sgb_ref, yb_ref, ws_ref, bsT_ref,
                 wpa_ref, wpb_ref, wout_ref, n2g_ref, wfi_ref, wfo_ref, fg_ref, o_ref, ya_ref):
    tm = x_ref.shape[0]
    mod = mod_ref[0]
    gate1 = mod[:, 2 * D_MODEL:3 * D_MODEL]
    shift2 = mod[:, 3 * D_MODEL:4 * D_MODEL]
    scale2 = mod[:, 4 * D_MODEL:5 * D_MODEL]
    gate2 = mod[:, 5 * D_MODEL:6 * D_MODEL]

    r_i = lax.broadcasted_iota(jnp.int32, (CHUNK, CHUNK), 0)
    c_i = lax.broadcasted_iota(jnp.int32, (CHUNK, CHUNK), 1)
    bsT = bsT_ref[...]
    gd = A_WIDTH // A_GROUPS
    for g in range(A_GROUPS):
        wm = jnp.where(r_i >= c_i, ws_ref[g], 0.0).astype(BF16)
        bias = bsT[:, g:g + 1]
        for c in range(tm // CHUNK):
            rows = slice(c * CHUNK, (c + 1) * CHUNK)
            cols = slice(g * gd, (g + 1) * gd)
            mixed = jnp.dot(wm, vln_ref[rows, cols], preferred_element_type=F32) + bias
            ya_ref[rows, cols] = (u_ref[rows, cols].astype(F32) * mixed).astype(BF16)

    pa = jnp.dot(ya_ref[...], wpa_ref[...], preferred_element_type=F32)
    pb = jnp.dot(yb_ref[...], wpb_ref[...], preferred_element_type=F32)
    merged = sga_ref[...].astype(F32) * pa + sgb_ref[...].astype(F32) * pb
    x1 = x_ref[...] + gate1 * jnp.dot(merged.astype(BF16), wout_ref[...], preferred_element_type=F32)

    h2 = (_rms_norm(x1, n2g_ref[...]) * (1.0 + scale2) + shift2).astype(BF16)
    fg = jnp.dot(h2, wfi_ref[:, 0:D_FF], preferred_element_type=F32)
    fu = jnp.dot(h2, wfi_ref[:, D_FF:], preferred_element_type=F32)
    act = (fg * _sigmoid(fg) * fu).astype(BF16)
    x2 = x1 + gate2 * jnp.dot(act, wfo_ref[...], preferred_element_type=F32)
    o_ref[...] = _rms_norm(x2, fg_ref[...])


def _post_call(x2, mod3, u, vln, sga, sgb, yb, ws, bsT, wpa, wpb, wout, n2g, wfi, wfo, fg):
    tm = TM_OUT
    per_b = SEQ // tm
    row = lambda t: (t, 0)
    const2 = lambda t: (0, 0)
    tok = lambda: pl.BlockSpec((tm, D_MODEL), row)
    in_specs = [
        tok(),
        pl.BlockSpec((1, 1, 6 * D_MODEL), lambda t: (t // per_b, 0, 0)),
        tok(), tok(), tok(), tok(), tok(),
        pl.BlockSpec(ws.shape, lambda t: (0, 0, 0)),
        pl.BlockSpec(bsT.shape, const2),
        pl.BlockSpec(wpa.shape, const2),
        pl.BlockSpec(wpb.shape, const2),
        pl.BlockSpec(wout.shape, const2),
        pl.BlockSpec((1, D_MODEL), const2),
        pl.BlockSpec(wfi.shape, const2),
        pl.BlockSpec(wfo.shape, const2),
        pl.BlockSpec((1, D_MODEL), const2),
    ]
    return pl.pallas_call(
        _post_kernel,
        grid=(TOKENS // tm,),
        in_specs=in_specs,
        out_specs=pl.BlockSpec((tm, D_MODEL), row),
        out_shape=jax.ShapeDtypeStruct((TOKENS, D_MODEL), F32),
        scratch_shapes=[pltpu.VMEM((tm, A_WIDTH), BF16)],
        compiler_params=pltpu.CompilerParams(dimension_semantics=("arbitrary",),
                                             vmem_limit_bytes=VMEM_LIMIT),
        name="merge_ffn",
    )(x2, mod3, u, vln, sga, sgb, yb, ws, bsT, wpa, wpb, wout, n2g, wfi, wfo, fg)


def _inv_freq_table():
    fq = ROPE_THETA ** (-np.arange(0, ROT_DIM, 2, dtype=np.float32) / ROT_DIM)
    fi = ROPE_THETA ** (-np.arange(0, IDX_ROT_DIM, 2, dtype=np.float32) / IDX_ROT_DIM)
    return np.concatenate([fq, fi]).astype(np.float32).reshape(-1, 1)


def kernel(x, c, positions, w_ada, b_ada, norm1_g, w_in, gmlp_ln_g, gmlp_ln_b, gmlp_w_s, gmlp_b_s,
           idx_k_ln_g, idx_k_ln_b, w_proj_a, w_proj_b, w_out, norm2_g, w_ffn_in, w_ffn_out, final_norm_g):
    assert x.shape == (BATCH, SEQ, D_MODEL) and w_in.shape == (1, D_MODEL, _IN_COLS)
    x2 = x.reshape(TOKENS, D_MODEL)
    pos3 = positions.reshape(BATCH, 1, SEQ)
    invf = jnp.asarray(_inv_freq_table())
    xcur = x2
    for l in range(w_ada.shape[0]):
        mod = _ada_call(c, w_ada[l], b_ada[l])
        mod3 = mod.reshape(BATCH, 1, 6 * D_MODEL)
        u, vln, sga, sgb, qT, k, vT, qiT, ki, wiT = _inproj_call(
            xcur, mod3, pos3, norm1_g[l].reshape(1, -1), w_in[l].astype(BF16),
            gmlp_ln_g[l].reshape(1, -1), gmlp_ln_b[l].reshape(1, -1),
            idx_k_ln_g[l].reshape(-1, 1), idx_k_ln_b[l].reshape(-1, 1), invf)
        yb = _attn_call(qT, qiT, wiT, k, ki, vT)
        last = l == w_ada.shape[0] - 1
        fg = final_norm_g.reshape(1, -1) if last else None
        assert last, "single-layer block"
        xcur = _post_call(xcur, mod3, u, vln, sga, sgb, yb, gmlp_w_s[l], gmlp_b_s[l].T,
                          w_proj_a[l].astype(BF16), w_proj_b[l].astype(BF16), w_out[l].astype(BF16),
                          norm2_g[l].reshape(1, -1), w_ffn_in[l].astype(BF16), w_ffn_out[l].astype(BF16), fg)
    return xcur.reshape(BATCH, SEQ, D_MODEL)
```

```python
import functools

import numpy as np
import jax
import jax.numpy as jnp
from jax import lax
from jax.experimental import pallas as pl
from jax.experimental.pallas import tpu as pltpu

D_MODEL = 1024
BATCH = 4
SEQ = 4096
CHUNK = 128
A_GROUPS = 8
A_WIDTH = 1024
N_HEADS = 8
N_KV_HEADS = 2
HEAD_DIM = 128
IDX_HEADS = 8
IDX_DIM = 64
TOPK = 256
ROPE_THETA = 500000.0
ROT_DIM = HEAD_DIM // 4
IDX_ROT_DIM = IDX_DIM // 4
D_FF = 2816
EPS = 1e-6
NEG_INF = -1e30

TOKENS = BATCH * SEQ
Q_TILE = 128
KEY_TILE = 512
IDX_K_LANES = 128
ACC_ROWS = 32
COARSE_HALVINGS = 8
SPLIT_7_BITS = 2.0 ** 17 + 1.0
BF16_STEP_DOWN = 2.0 ** -8 + 2.0 ** -10
FIRST_CHECK_AFTER = 16
CHECK_EVERY = 2
SWEEP_UNROLL = (4, 2, 1)
TM_IN = 512
TM_OUT = 256
HEADS_PER_KV = N_HEADS // N_KV_HEADS
VMEM_LIMIT = 56 * 1024 * 1024

_OFF_U = 0
_OFF_V = _OFF_U + A_WIDTH
_OFF_Q = _OFF_V + A_WIDTH
_OFF_K = _OFF_Q + N_HEADS * HEAD_DIM
_OFF_VV = _OFF_K + N_KV_HEADS * HEAD_DIM
_OFF_QI = _OFF_VV + N_KV_HEADS * HEAD_DIM
_OFF_KI = _OFF_QI + IDX_HEADS * IDX_DIM
_OFF_WI = _OFF_KI + IDX_DIM
_OFF_GA = _OFF_WI + IDX_HEADS
_OFF_GB = _OFF_GA + D_MODEL
_IN_COLS = _OFF_GB + D_MODEL

_T_Q = 0
_T_K = _T_Q + N_HEADS * HEAD_DIM
_T_V = _T_K + N_KV_HEADS * HEAD_DIM
_T_QI = _T_V + N_KV_HEADS * HEAD_DIM
_T_KI = _T_QI + IDX_HEADS * IDX_DIM
_T_WI = _T_KI + IDX_DIM
_T_ROWS = _T_WI + IDX_HEADS
_T_PAD = -(-_T_ROWS // 128) * 128

F32 = jnp.float32
BF16 = jnp.bfloat16


def _gelu_tanh(x):
    return 0.5 * x * (1.0 + jnp.tanh(np.sqrt(2.0 / np.pi).astype(np.float32) * (x + 0.044715 * (x * x * x))))


def _sigmoid(x):
    return 1.0 / (1.0 + jnp.exp(-x))


def _rms_norm(x, g):
    return x * lax.rsqrt(jnp.mean(x * x, axis=-1, keepdims=True) + EPS) * g


def _ada_kernel(c_ref, w_ref, b_ref, o_ref):
    c = c_ref[...]
    a = c * _sigmoid(c)
    w = w_ref[...]
    a_hi = a.astype(BF16)
    a_lo = (a - a_hi.astype(F32)).astype(BF16)
    w_hi = w.astype(BF16)
    w_lo = (w - w_hi.astype(F32)).astype(BF16)
    dot = functools.partial(jnp.dot, preferred_element_type=F32)
    o_ref[...] = dot(a_hi, w_hi) + (dot(a_hi, w_lo) + dot(a_lo, w_hi)) + b_ref[...]


def _ada_call(c, w_ada, b_ada):
    n_out = 6 * D_MODEL
    tn = 1024
    return pl.pallas_call(
        _ada_kernel,
        grid=(n_out // tn,),
        in_specs=[pl.BlockSpec((BATCH, D_MODEL), lambda j: (0, 0)),
                  pl.BlockSpec((D_MODEL, tn), lambda j: (0, j)),
                  pl.BlockSpec((1, tn), lambda j: (0, j))],
        out_specs=pl.BlockSpec((BATCH, tn), lambda j: (0, j)),
        out_shape=jax.ShapeDtypeStruct((BATCH, n_out), F32),
        compiler_params=pltpu.CompilerParams(dimension_semantics=("arbitrary",),
                                             vmem_limit_bytes=VMEM_LIMIT),
        name="ada_mod",
    )(c, w_ada, b_ada.reshape(1, n_out))


def _rope_rows(blk, cos, sin, half):
    x1 = blk[0:half]
    x2 = blk[half:2 * half]
    return x1 * cos - x2 * sin, x2 * cos + x1 * sin


def _inproj_kernel(x_ref, mod_ref, pos_ref, n1g_ref, w_ref, lng_ref, lnb_ref,
                   kig_ref, kib_ref, invf_ref,
                   u_ref, vln_ref, sga_ref, sgb_ref, qT_ref, k_ref, vT_ref, qiT_ref, ki_ref, wiT_ref,
                   wg_ref, wt_ref):
    tm = x_ref.shape[0]

    @pl.when(pl.program_id(0) == 0)
    def _():
        wg_ref[:, 0:D_MODEL] = w_ref[:, _OFF_GA:_OFF_GB]
        wg_ref[:, D_MODEL:] = w_ref[:, _OFF_GB:_IN_COLS]
        for c in range(_T_PAD // 128):
            cols = slice(_OFF_Q + c * 128, _OFF_Q + (c + 1) * 128)
            wt_ref[c * 128:(c + 1) * 128, :] = w_ref[:, cols].astype(F32).T.astype(BF16)

    x = x_ref[...]
    mod = mod_ref[0]
    shift1 = mod[:, 0:D_MODEL]
    scale1 = mod[:, D_MODEL:2 * D_MODEL]
    h = _rms_norm(x, n1g_ref[...]) * (1.0 + scale1) + shift1
    hb = h.astype(BF16)

    zu = jnp.dot(hb, w_ref[:, _OFF_U:_OFF_V], preferred_element_type=F32)
    u_ref[...] = _gelu_tanh(zu).astype(BF16)
    zv = _gelu_tanh(jnp.dot(hb, w_ref[:, _OFF_V:_OFF_Q], preferred_element_type=F32))
    mu = jnp.mean(zv, axis=-1, keepdims=True)
    zc = zv - mu
    var = jnp.mean(zc * zc, axis=-1, keepdims=True)
    vln_ref[...] = (zc * lax.rsqrt(var + EPS) * lng_ref[...] + lnb_ref[...]).astype(BF16)
    zga = jnp.dot(hb, wg_ref[:, 0:D_MODEL], preferred_element_type=F32)
    sga_ref[...] = _sigmoid(zga).astype(BF16)
    zgb = jnp.dot(hb, wg_ref[:, D_MODEL:], preferred_element_type=F32)
    sgb_ref[...] = _sigmoid(zgb).astype(BF16)

    nt = (((1,), (1,)), ((), ()))
    pos = pos_ref[0].astype(F32)
    ang = invf_ref[...] * pos
    cos = jnp.cos(ang)
    sin = jnp.sin(ang)
    hq = ROT_DIM // 2
    hi = IDX_ROT_DIM // 2
    cos_q, sin_q = cos[0:hq], sin[0:hq]
    cos_i, sin_i = cos[hq:hq + hi], sin[hq:hq + hi]
    n_sub = tm // Q_TILE

    zt = lax.dot_general(wt_ref[0:_T_ROWS, :], hb, nt, preferred_element_type=F32)
    zq = zt[_T_Q:_T_K]
    q_scale = HEAD_DIM ** -0.5 * float(np.log2(np.e))
    for hd in range(N_HEADS):
        blk = zq[hd * HEAD_DIM:(hd + 1) * HEAD_DIM]
        r1, r2 = _rope_rows(blk, cos_q, sin_q, hq)
        full = (jnp.concatenate([r1, r2, blk[ROT_DIM:]], axis=0) * q_scale).astype(BF16)
        for s in range(n_sub):
            qT_ref[s, :, hd * Q_TILE:(hd + 1) * Q_TILE] = full[:, s * Q_TILE:(s + 1) * Q_TILE]

    zk = zt[_T_K:_T_V]
    k_rows = []
    for g in range(N_KV_HEADS):
        blk = zk[g * HEAD_DIM:(g + 1) * HEAD_DIM]
        r1, r2 = _rope_rows(blk, cos_q, sin_q, hq)
        k_rows += [r1, r2, blk[ROT_DIM:]]
    k_ref[...] = jnp.concatenate(k_rows, axis=0).T.astype(BF16)

    vT_ref[...] = zt[_T_V:_T_QI].astype(BF16)

    zqi = zt[_T_QI:_T_KI]
    for hd in range(IDX_HEADS):
        blk = zqi[hd * IDX_DIM:(hd + 1) * IDX_DIM]
        r1, r2 = _rope_rows(blk, cos_i, sin_i, hi)
        full = jnp.concatenate([r1, r2, blk[IDX_ROT_DIM:]], axis=0).astype(BF16)
        for s in range(n_sub):
            qiT_ref[s, :, hd * Q_TILE:(hd + 1) * Q_TILE] = full[:, s * Q_TILE:(s + 1) * Q_TILE]

    zrest = zt[_T_KI:_T_ROWS]
    zki = zrest[0:IDX_DIM]
    kmu = jnp.mean(zki, axis=0, keepdims=True)
    kc = zki - kmu
    kvar = jnp.mean(kc * kc, axis=0, keepdims=True)
    kin = kc * lax.rsqrt(kvar + EPS) * kig_ref[...] + kib_ref[...]
    r1, r2 = _rope_rows(kin, cos_i, sin_i, hi)
    ki_full = jnp.concatenate([r1, r2, kin[IDX_ROT_DIM:], jnp.zeros((IDX_K_LANES - IDX_DIM, tm), F32)], axis=0)
    ki_ref[...] = ki_full.T.astype(BF16)
    wiT_ref[...] = zrest[IDX_DIM:IDX_DIM + IDX_HEADS] * ((IDX_HEADS ** -0.5) * (IDX_DIM ** -0.5))


def _inproj_call(x2, mod3, pos3, n1g, w, lng, lnb, kig, kib, invf):
    tm = TM_IN
    n_tiles = TOKENS // tm
    per_b = SEQ // tm
    n_sub = tm // Q_TILE
    const2 = lambda t: (0, 0)
    row = lambda t: (t, 0)
    in_specs = [
        pl.BlockSpec((tm, D_MODEL), row),
        pl.BlockSpec((1, 1, 6 * D_MODEL), lambda t: (t // per_b, 0, 0)),
        pl.BlockSpec((1, 1, tm), lambda t: (t // per_b, 0, t % per_b)),
        pl.BlockSpec((1, D_MODEL), const2),
        pl.BlockSpec(w.shape, const2),
        pl.BlockSpec((1, A_WIDTH), const2),
        pl.BlockSpec((1, A_WIDTH), const2),
        pl.BlockSpec((IDX_DIM, 1), const2),
        pl.BlockSpec((IDX_DIM, 1), const2),
        pl.BlockSpec(invf.shape, const2),
    ]
    out_shape = [
        jax.ShapeDtypeStruct((TOKENS, A_WIDTH), BF16),
        jax.ShapeDtypeStruct((TOKENS, A_WIDTH), BF16),
        jax.ShapeDtypeStruct((TOKENS, D_MODEL), BF16),
        jax.ShapeDtypeStruct((TOKENS, D_MODEL), BF16),
        jax.ShapeDtypeStruct((TOKENS // Q_TILE, HEAD_DIM, N_HEADS * Q_TILE), BF16),
        jax.ShapeDtypeStruct((TOKENS, N_KV_HEADS * HEAD_DIM), BF16),
        jax.ShapeDtypeStruct((N_KV_HEADS * HEAD_DIM, TOKENS), BF16),
        jax.ShapeDtypeStruct((TOKENS // Q_TILE, IDX_DIM, IDX_HEADS * Q_TILE), BF16),
        jax.ShapeDtypeStruct((TOKENS, IDX_K_LANES), BF16),
        jax.ShapeDtypeStruct((IDX_HEADS, TOKENS), F32),
    ]
    out_specs = [
        pl.BlockSpec((tm, A_WIDTH), row),
        pl.BlockSpec((tm, A_WIDTH), row),
        pl.BlockSpec((tm, D_MODEL), row),
        pl.BlockSpec((tm, D_MODEL), row),
        pl.BlockSpec((n_sub, HEAD_DIM, N_HEADS * Q_TILE), lambda t: (t, 0, 0)),
        pl.BlockSpec((tm, N_KV_HEADS * HEAD_DIM), row),
        pl.BlockSpec((N_KV_HEADS * HEAD_DIM, tm), lambda t: (0, t)),
        pl.BlockSpec((n_sub, IDX_DIM, IDX_HEADS * Q_TILE), lambda t: (t, 0, 0)),
        pl.BlockSpec((tm, IDX_K_LANES), row),
        pl.BlockSpec((IDX_HEADS, tm), lambda t: (0, t)),
    ]
    return pl.pallas_call(
        _inproj_kernel,
        grid=(n_tiles,),
        in_specs=in_specs,
        out_specs=out_specs,
        out_shape=out_shape,
        scratch_shapes=[pltpu.VMEM((D_MODEL, 2 * D_MODEL), BF16),
                        pltpu.VMEM((_T_PAD, D_MODEL), BF16)],
        compiler_params=pltpu.CompilerParams(dimension_semantics=("arbitrary",),
                                             vmem_limit_bytes=VMEM_LIMIT),
        name="in_proj",
    )(x2, mod3, pos3, n1g, w, lng, lnb, kig, kib, invf)


def _col_reduce(x, op):
    return op(x.reshape(x.shape[0] // ACC_ROWS, ACC_ROWS, x.shape[1]), axis=0)


def _key_tiles(tile):
    return (tile * Q_TILE + Q_TILE + KEY_TILE - 1) // KEY_TILE


def _attn_kernel(qT_ref, qiT_ref, wiT_ref, k_ref, ki_ref, vT_ref, y_ref,
                 sc_ref, scb_ref, acc_ref, l_ref, vsel_ref):
    t = pl.program_id(1)
    nq = SEQ // Q_TILE
    i = jnp.minimum(t, nq - 1)
    ia = jnp.maximum(t - 1, 0)
    sc_i = sc_ref.at[i % 2]
    sc_a = sc_ref.at[ia % 2]
    nkt = jnp.where(t < nq, _key_tiles(i), 0)
    nka = jnp.where(t >= 1, _key_tiles(ia), 0)
    kf = float(TOPK)
    gw = HEADS_PER_KV * Q_TILE
    hw = N_HEADS * Q_TILE

    @pl.when(t == 0)
    def _():
        vsel_ref[...] = jnp.zeros_like(vsel_ref)

    qiT = qiT_ref[0]
    qT = qT_ref[0]
    w = wiT_ref[...]
    wrow = jnp.concatenate([w[hd:hd + 1, :] for hd in range(IDX_HEADS)], axis=1)
    vsel_a = vsel_ref[ia % 2]

    row_i = lax.broadcasted_iota(jnp.int32, (KEY_TILE, Q_TILE), 0)
    qidx = i * Q_TILE + lax.broadcasted_iota(jnp.int32, (KEY_TILE, Q_TILE), 1)

    def score_tile(j, mx_a, mn_a):
        sl = pl.ds(pl.multiple_of(j * KEY_TILE, KEY_TILE), KEY_TILE)
        lg = jnp.dot(ki_ref[sl, 0:IDX_DIM], qiT, preferred_element_type=F32)
        r = jnp.maximum(lg, 0.0) * wrow
        s = r[:, 0:Q_TILE]
        for hd in range(1, IDX_HEADS):
            s = s + r[:, hd * Q_TILE:(hd + 1) * Q_TILE]
        causal = j * KEY_TILE + row_i <= qidx
        sm = jnp.where(causal, s, -jnp.inf)
        sc_i[sl, :] = sm
        scb_ref[sl, :] = sm.astype(BF16)
        mx_a = jnp.maximum(mx_a, _col_reduce(jnp.where(causal, s, -jnp.inf), jnp.max))
        mn_a = jnp.minimum(mn_a, _col_reduce(jnp.where(causal, s, jnp.inf), jnp.min))
        return mx_a, mn_a

    def key_mask(sl):
        return jnp.where(sc_a[sl, :] >= vsel_a, 0.0, NEG_INF)

    def qk_tile(j):
        sl = pl.ds(pl.multiple_of(j * KEY_TILE, KEY_TILE), KEY_TILE)
        return [jnp.dot(k_ref[sl, g * HEAD_DIM:(g + 1) * HEAD_DIM], qT[:, g * gw:(g + 1) * gw],
                        preferred_element_type=F32) for g in range(N_KV_HEADS)]

    def attend_tile(j, qk, l8, m):
        sl = pl.ds(pl.multiple_of(j * KEY_TILE, KEY_TILE), KEY_TILE)
        mask = key_mask(sl)
        parts = []
        for g in range(N_KV_HEADS):
            a = qk[g]
            ps = []
            for r in range(HEADS_PER_KV):
                e = a[:, r * Q_TILE:(r + 1) * Q_TILE] + mask
                if m is not None:
                    c0 = (g * HEADS_PER_KV + r) * Q_TILE
                    e = e - m[:, c0:c0 + Q_TILE]
                ps.append(jnp.exp2(e))
            p = jnp.concatenate(ps, axis=1)
            acc_ref[:, g * gw:(g + 1) * gw] += jnp.dot(
                vT_ref[g * HEAD_DIM:(g + 1) * HEAD_DIM, sl], p.astype(BF16), preferred_element_type=F32)
            parts.append(jnp.sum(p.reshape(KEY_TILE // 8, 8, gw), axis=0))
        return l8 + jnp.concatenate(parts, axis=1)

    acc_ref[...] = jnp.zeros_like(acc_ref)
    n_both = jnp.minimum(nkt, nka)

    def both_body(j, c):
        mx_a, mn_a, l8 = c
        mx_a, mn_a = score_tile(j, mx_a, mn_a)
        return mx_a, mn_a, attend_tile(j, qk_tile(j), l8, None)

    def score_body(j, c):
        mx_a, mn_a, l8 = c
        mx_a, mn_a = score_tile(j, mx_a, mn_a)
        return mx_a, mn_a, l8

    def attend_body(j, c):
        mx_a, mn_a, l8 = c
        return mx_a, mn_a, attend_tile(j, qk_tile(j), l8, None)

    carry = (jnp.full((ACC_ROWS, Q_TILE), -jnp.inf, F32), jnp.full((ACC_ROWS, Q_TILE), jnp.inf, F32),
             jnp.zeros((8, hw), F32))
    done = 0
    for width in SWEEP_UNROLL:
        def wide_body(jw, c, width=width, done=done):
            for u in range(width):
                c = both_body(done + jw * width + u, c)
            return c
        trips = (n_both - done) // width
        carry = lax.fori_loop(0, trips, wide_body, carry)
        done = done + trips * width
    carry = lax.fori_loop(n_both, nkt, score_body, carry)
    mx_a, mn_a, l8 = lax.fori_loop(n_both, nka, attend_body, carry)
    l_ref[...] = jnp.sum(l8, axis=0, keepdims=True)

    def tiles(fn, init):
        def body(j, carry):
            s = sc_i[pl.ds(pl.multiple_of(j * KEY_TILE, KEY_TILE), KEY_TILE), :]
            return fn(s, carry)
        return lax.fori_loop(0, nkt, body, init)

    def count_ge(t):
        c8 = tiles(lambda s, c: c + _col_reduce(jnp.where(s >= t, 1.0, 0.0), jnp.sum),
                   jnp.zeros((ACC_ROWS, Q_TILE), F32))
        return jnp.sum(c8, axis=0, keepdims=True)

    def count_coarse(g):
        g16 = g.astype(BF16)
        rows = 2 * ACC_ROWS

        def body(j, acc):
            sl = pl.ds(pl.multiple_of(j * KEY_TILE, KEY_TILE), KEY_TILE)
            w = jnp.where(scb_ref[sl, :] >= g16, jnp.ones((), BF16), jnp.zeros((), BF16))
            parts = [w[r * rows:(r + 1) * rows] for r in range(KEY_TILE // rows)]
            while len(parts) > 1:
                parts = [parts[a] + parts[a + 1] for a in range(0, len(parts), 2)]
            return acc + parts[0]
        acc = lax.fori_loop(0, nkt, body, jnp.zeros((rows, Q_TILE), BF16))
        return jnp.sum(acc.astype(F32), axis=0, keepdims=True)

    def bisect_coarse(state, n):
        def step(_, c):
            lo, hi, n_lo, n_hi = c
            mid = 0.5 * lo + 0.5 * hi
            p = mid * SPLIT_7_BITS
            g = p - (p - mid)
            below = (g - jnp.abs(g) * BF16_STEP_DOWN).astype(BF16).astype(F32)
            half = 0.5 * (g + below)
            th = g - (half.astype(BF16).astype(F32) - half)
            usable = (th <= g) & (th > lo) & (th < hi)
            n_t = count_coarse(g)
            up = usable & (n_t >= kf)
            dn = usable & (n_t < kf)
            return (jnp.where(up, th, lo), jnp.where(dn, th, hi),
                    jnp.where(up, n_t, n_lo), jnp.where(dn, n_t, n_hi))
        return lax.fori_loop(0, n, step, state)

    def bisect(state, n):
        def step(_, c):
            lo, hi, n_lo, n_hi = c
            mid = 0.5 * lo + 0.5 * hi
            n_mid = count_ge(mid)
            ok = n_mid >= kf
            return (jnp.where(ok, mid, lo), jnp.where(ok, hi, mid),
                    jnp.where(ok, n_mid, n_lo), jnp.where(ok, n_hi, n_mid))
        return lax.fori_loop(0, n, step, state)

    @pl.when(t < TOPK // Q_TILE)
    def _():
        vsel_ref[i % 2] = jnp.full((1, Q_TILE), jnp.finfo(jnp.float32).min, F32)

    @pl.when((t >= TOPK // Q_TILE) & (t < nq))
    def _():
        mx = jnp.max(mx_a, axis=0, keepdims=True)
        mn = jnp.min(mn_a, axis=0, keepdims=True)
        n_valid = (qidx[0:1, :] + 1).astype(F32)
        hi0 = mx + jnp.maximum(jnp.abs(mx), 1e-30) * 1e-6
        state = bisect_coarse((mn, hi0, n_valid, jnp.zeros((1, Q_TILE), F32)), COARSE_HALVINGS)
        state = bisect(state, FIRST_CHECK_AFTER - CHECK_EVERY - COARSE_HALVINGS)

        def check(state):
            lo, hi, n_lo, n_hi = state

            def f(s, c):
                a8, b8 = c
                a8 = jnp.maximum(a8, _col_reduce(jnp.where(s < hi, s, -jnp.inf), jnp.max))
                b8 = jnp.minimum(b8, _col_reduce(jnp.where(s >= lo, s, jnp.inf), jnp.min))
                return a8, b8
            a8, b8 = tiles(f, (jnp.full((ACC_ROWS, Q_TILE), -jnp.inf, F32), jnp.full((ACC_ROWS, Q_TILE), jnp.inf, F32)))
            top = jnp.max(a8, axis=0, keepdims=True)
            bottom = jnp.min(b8, axis=0, keepdims=True)
            single = top == bottom
            pinned = single | (n_lo - n_hi < 2.5)
            take_top = jnp.logical_not(single) & (kf - n_hi < 1.5)
            take_bottom = jnp.logical_not(single | take_top)
            v = jnp.where(take_top, top, bottom)
            n_ge = jnp.where(take_top, n_hi + 1.0, n_lo)
            n_gt = jnp.where(take_bottom, n_hi + 1.0, n_hi)
            return (v, n_ge, n_gt), jnp.min(jnp.where(pinned, 1.0, 0.0))

        def w_cond(c):
            return c[2] < 0.5

        def w_body(c):
            st = bisect(c[0], CHECK_EVERY)
            found, done = check(st)
            return st, found, done

        _, (vk, n_ge, n_gt), _ = lax.while_loop(w_cond, w_body, (state, (mn, n_valid, n_valid), jnp.float32(0.0)))
        need = kf - n_gt
        has_tie = jnp.max(n_ge) > kf + 0.5

        @pl.when(jnp.logical_not(has_tie))
        def _():
            vsel_ref[i % 2] = vk

        @pl.when(has_tie)
        def _():
            vsel_ref[i % 2] = jnp.full((1, Q_TILE), -0.5, F32)
            r_i = lax.broadcasted_iota(jnp.int32, (KEY_TILE, KEY_TILE), 0)
            c_i = lax.broadcasted_iota(jnp.int32, (KEY_TILE, KEY_TILE), 1)
            tri = jnp.where(r_i >= c_i, 1.0, 0.0).astype(BF16)

            def body(j, seen):
                sl = pl.ds(pl.multiple_of(j * KEY_TILE, KEY_TILE), KEY_TILE)
                s = sc_i[sl, :]
                eq = s == vk
                rank = jnp.dot(tri, jnp.where(eq, 1.0, 0.0).astype(BF16), preferred_element_type=F32) + seen
                sel = (s > vk) | (eq & (rank <= need))
                sc_i[sl, :] = jnp.where(sel, 0.0, NEG_INF)
                return rank[KEY_TILE - 1:KEY_TILE, :]
            lax.fori_loop(0, nkt, body, jnp.zeros((1, Q_TILE), F32))

    @pl.when(t >= 1)
    def _():
        l_fast = l_ref[...]
        in_range = (jnp.min(l_fast) > 1e-20) & (jnp.max(l_fast) < 1e30)

        @pl.when(jnp.logical_not(in_range))
        def _():
            def max_body(j, m8):
                sl = pl.ds(pl.multiple_of(j * KEY_TILE, KEY_TILE), KEY_TILE)
                mask = key_mask(sl)
                parts = []
                for g in range(N_KV_HEADS):
                    a = jnp.dot(k_ref[sl, g * HEAD_DIM:(g + 1) * HEAD_DIM], qT[:, g * gw:(g + 1) * gw],
                                preferred_element_type=F32)
                    a = a + jnp.concatenate([mask] * HEADS_PER_KV, axis=1)
                    parts.append(jnp.max(a.reshape(KEY_TILE // 8, 8, gw), axis=0))
                return jnp.maximum(m8, jnp.concatenate(parts, axis=1))
            m8 = lax.fori_loop(0, nka, max_body, jnp.full((8, hw), -jnp.inf, F32))
            m = jnp.max(m8, axis=0, keepdims=True)
            acc_ref[...] = jnp.zeros_like(acc_ref)
            l8 = lax.fori_loop(0, nka, lambda j, c: attend_tile(j, qk_tile(j), c, m), jnp.zeros((8, hw), F32))
            l_ref[...] = jnp.sum(l8, axis=0, keepdims=True)

        oT = acc_ref[...] / l_ref[...]
        for hd in range(N_HEADS):
            y_ref[:, hd * HEAD_DIM:(hd + 1) * HEAD_DIM] = oT[:, hd * Q_TILE:(hd + 1) * Q_TILE].T.astype(BF16)


def _attn_call(qT, qiT, wiT, k, ki, vT):
    nq = SEQ // Q_TILE
    scored = lambda b, t: b * nq + jnp.minimum(t, nq - 1)
    attended = lambda b, t: b * nq + jnp.maximum(t - 1, 0)
    return pl.pallas_call(
        _attn_kernel,
        grid=(BATCH, nq + 1),
        in_specs=[
            pl.BlockSpec((1, HEAD_DIM, N_HEADS * Q_TILE), lambda b, t: (attended(b, t), 0, 0)),
            pl.BlockSpec((1, IDX_DIM, IDX_HEADS * Q_TILE), lambda b, t: (scored(b, t), 0, 0)),
            pl.BlockSpec((IDX_HEADS, Q_TILE), lambda b, t: (0, scored(b, t))),
            pl.BlockSpec((SEQ, N_KV_HEADS * HEAD_DIM), lambda b, t: (b, 0)),
            pl.BlockSpec((SEQ, IDX_K_LANES), lambda b, t: (b, 0)),
            pl.BlockSpec((N_KV_HEADS * HEAD_DIM, SEQ), lambda b, t: (0, b)),
        ],
        out_specs=pl.BlockSpec((Q_TILE, N_HEADS * HEAD_DIM), lambda b, t: (attended(b, t), 0)),
        out_shape=jax.ShapeDtypeStruct((TOKENS, N_HEADS * HEAD_DIM), BF16),
        scratch_shapes=[pltpu.VMEM((2, SEQ, Q_TILE), F32),
                        pltpu.VMEM((SEQ, Q_TILE), BF16),
                        pltpu.VMEM((HEAD_DIM, N_HEADS * Q_TILE), F32),
                        pltpu.VMEM((1, N_HEADS * Q_TILE), F32),
                        pltpu.VMEM((2, 1, Q_TILE), F32)],
        compiler_params=pltpu.CompilerParams(dimension_semantics=("arbitrary", "arbitrary"),
                                             vmem_limit_bytes=VMEM_LIMIT),
        name="dsa_attention",
    )(qT, qiT, wiT, k, ki, vT)


def _post_kernel(x_ref, mod_ref, u_ref, vln_ref, sga_ref, sgb_ref, yb_ref, ws_ref, bsT_ref,
                 wpa_ref, wpb_ref, wout_ref, n2g_ref, wfi_ref, wfo_ref, fg_ref, o_ref, ya_ref):
    tm = x_ref.shape[0]
    mod = mod_ref[0]
    gate1 = mod[:, 2 * D_MODEL:3 * D_MODEL]
    shift2 = mod[:, 3 * D_MODEL:4 * D_MODEL]
    scale2 = mod[:, 4 * D_MODEL:5 * D_MODEL]
    gate2 = mod[:, 5 * D_MODEL:6 * D_MODEL]

    r_i = lax.broadcasted_iota(jnp.int32, (CHUNK, CHUNK), 0)
    c_i = lax.broadcasted_iota(jnp.int32, (CHUNK, CHUNK), 1)
    bsT = bsT_ref[...]
    gd = A_WIDTH // A_GROUPS
    for g in range(A_GROUPS):
        wm = jnp.where(r_i >= c_i, ws_ref[g], 0.0).astype(BF16)
        bias = bsT[:, g:g + 1]
        for c in range(tm // CHUNK):
            rows = slice(c * CHUNK, (c + 1) * CHUNK)
            cols = slice(g * gd, (g + 1) * gd)
            mixed = jnp.dot(wm, vln_ref[rows, cols], preferred_element_type=F32) + bias
            ya_ref[rows, cols] = (u_ref[rows, cols].astype(F32) * mixed).astype(BF16)

    pa = jnp.dot(ya_ref[...], wpa_ref[...], preferred_element_type=F32)
    pb = jnp.dot(yb_ref[...], wpb_ref[...], preferred_element_type=F32)
    merged = sga_ref[...].astype(F32) * pa + sgb_ref[...].astype(F32) * pb
    x1 = x_ref[...] + gate1 * jnp.dot(merged.astype(BF16), wout_ref[...], preferred_element_type=F32)

    h2 = (_rms_norm(x1, n2g_ref[...]) * (1.0 + scale2) + shift2).astype(BF16)
    fg = jnp.dot(h2, wfi_ref[:, 0:D_FF], preferred_element_type=F32)
    fu = jnp.dot(h2, wfi_ref[:, D_FF:], preferred_element_type=F32)
    act = (fg * _sigmoid(fg) * fu).astype(BF16)
    x2 = x1 + gate2 * jnp.dot(act, wfo_ref[...], preferred_element_type=F32)
    o_ref[...] = _rms_norm(x2, fg_ref[...])


def _post_call(x2, mod3, u, vln, sga, sgb, yb, ws, bsT, wpa, wpb, wout, n2g, wfi, wfo, fg):
    tm = TM_OUT
    per_b = SEQ // tm
    row = lambda t: (t, 0)
    const2 = lambda t: (0, 0)
    tok = lambda: pl.BlockSpec((tm, D_MODEL), row)
    in_specs = [
        tok(),
        pl.BlockSpec((1, 1, 6 * D_MODEL), lambda t: (t // per_b, 0, 0)),
        tok(), tok(), tok(), tok(), tok(),
        pl.BlockSpec(ws.shape, lambda t: (0, 0, 0)),
        pl.BlockSpec(bsT.shape, const2),
        pl.BlockSpec(wpa.shape, const2),
        pl.BlockSpec(wpb.shape, const2),
        pl.BlockSpec(wout.shape, const2),
        pl.BlockSpec((1, D_MODEL), const2),
        pl.BlockSpec(wfi.shape, const2),
        pl.BlockSpec(wfo.shape, const2),
        pl.BlockSpec((1, D_MODEL), const2),
    ]
    return pl.pallas_call(
        _post_kernel,
        grid=(TOKENS // tm,),
        in_specs=in_specs,
        out_specs=pl.BlockSpec((tm, D_MODEL), row),
        out_shape=jax.ShapeDtypeStruct((TOKENS, D_MODEL), F32),
        scratch_shapes=[pltpu.VMEM((tm, A_WIDTH), BF16)],
        compiler_params=pltpu.CompilerParams(dimension_semantics=("arbitrary",),
                                             vmem_limit_bytes=VMEM_LIMIT),
        name="merge_ffn",
    )(x2, mod3, u, vln, sga, sgb, yb, ws, bsT, wpa, wpb, wout, n2g, wfi, wfo, fg)


def _inv_freq_table():
    fq = ROPE_THETA ** (-np.arange(0, ROT_DIM, 2, dtype=np.float32) / ROT_DIM)
    fi = ROPE_THETA ** (-np.arange(0, IDX_ROT_DIM, 2, dtype=np.float32) / IDX_ROT_DIM)
    return np.concatenate([fq, fi]).astype(np.float32).reshape(-1, 1)


def kernel(x, c, positions, w_ada, b_ada, norm1_g, w_in, gmlp_ln_g, gmlp_ln_b, gmlp_w_s, gmlp_b_s,
           idx_k_ln_g, idx_k_ln_b, w_proj_a, w_proj_b, w_out, norm2_g, w_ffn_in, w_ffn_out, final_norm_g):
    assert x.shape == (BATCH, SEQ, D_MODEL) and w_in.shape == (1, D_MODEL, _IN_COLS)
    x2 = x.reshape(TOKENS, D_MODEL)
    pos3 = positions.reshape(BATCH, 1, SEQ)
    invf = jnp.asarray(_inv_freq_table())
    xcur = x2
    for l in range(w_ada.shape[0]):
        mod = _ada_call(c, w_ada[l], b_ada[l])
        mod3 = mod.reshape(BATCH, 1, 6 * D_MODEL)
        u, vln, sga, sgb, qT, k, vT, qiT, ki, wiT = _inproj_call(
            xcur, mod3, pos3, norm1_g[l].reshape(1, -1), w_in[l].astype(BF16),
            gmlp_ln_g[l].reshape(1, -1), gmlp_ln_b[l].reshape(1, -1),
            idx_k_ln_g[l].reshape(-1, 1), idx_k_ln_b[l].reshape(-1, 1), invf)
        yb = _attn_call(qT, qiT, wiT, k, ki, vT)
        last = l == w_ada.shape[0] - 1
        fg = final_norm_g.reshape(1, -1) if last else None
        assert last, "single-layer block"
        xcur = _post_call(xcur, mod3, u, vln, sga, sgb, yb, gmlp_w_s[l], gmlp_b_s[l].T,
                          w_proj_a[l].astype(BF16), w_proj_b[l].astype(BF16), w_out[l].astype(BF16),
                          norm2_g[l].reshape(1, -1), w_ffn_in[l].astype(BF16), w_ffn_out[l].astype(BF16), fg)
    return xcur.reshape(BATCH, SEQ, D_MODEL)
```

```python
import functools

import numpy as np
import jax
import jax.numpy as jnp
from jax import lax
from jax.experimental import pallas as pl
from jax.experimental.pallas import tpu as pltpu

D_MODEL = 1024
BATCH = 4
SEQ = 4096
CHUNK = 128
A_GROUPS = 8
A_WIDTH = 1024
N_HEADS = 8
N_KV_HEADS = 2
HEAD_DIM = 128
IDX_HEADS = 8
IDX_DIM = 64
TOPK = 256
ROPE_THETA = 500000.0
ROT_DIM = HEAD_DIM // 4
IDX_ROT_DIM = IDX_DIM // 4
D_FF = 2816
EPS = 1e-6
NEG_INF = -1e30

TOKENS = BATCH * SEQ
Q_TILE = 128
KEY_TILE = 512
IDX_K_LANES = 128
ACC_ROWS = 32
FIRST_CHECK_AFTER = 16
CHECK_EVERY = 2
SWEEP_UNROLL = (4, 2, 1)
TM_IN = 512
TM_OUT = 256
HEADS_PER_KV = N_HEADS // N_KV_HEADS
VMEM_LIMIT = 56 * 1024 * 1024

_OFF_U = 0
_OFF_V = _OFF_U + A_WIDTH
_OFF_Q = _OFF_V + A_WIDTH
_OFF_K = _OFF_Q + N_HEADS * HEAD_DIM
_OFF_VV = _OFF_K + N_KV_HEADS * HEAD_DIM
_OFF_QI = _OFF_VV + N_KV_HEADS * HEAD_DIM
_OFF_KI = _OFF_QI + IDX_HEADS * IDX_DIM
_OFF_WI = _OFF_KI + IDX_DIM
_OFF_GA = _OFF_WI + IDX_HEADS
_OFF_GB = _OFF_GA + D_MODEL
_IN_COLS = _OFF_GB + D_MODEL

_T_Q = 0
_T_K = _T_Q + N_HEADS * HEAD_DIM
_T_V = _T_K + N_KV_HEADS * HEAD_DIM
_T_QI = _T_V + N_KV_HEADS * HEAD_DIM
_T_KI = _T_QI + IDX_HEADS * IDX_DIM
_T_WI = _T_KI + IDX_DIM
_T_ROWS = _T_WI + IDX_HEADS
_T_PAD = -(-_T_ROWS // 128) * 128

F32 = jnp.float32
BF16 = jnp.bfloat16


def _gelu_tanh(x):
    return 0.5 * x * (1.0 + jnp.tanh(np.sqrt(2.0 / np.pi).astype(np.float32) * (x + 0.044715 * (x * x * x))))


def _sigmoid(x):
    return 1.0 / (1.0 + jnp.exp(-x))


def _rms_norm(x, g):
    return x * lax.rsqrt(jnp.mean(x * x, axis=-1, keepdims=True) + EPS) * g


def _ada_kernel(c_ref, w_ref, b_ref, o_ref):
    c = c_ref[...]
    a = c * _sigmoid(c)
    w = w_ref[...]
    a_hi = a.astype(BF16)
    a_lo = (a - a_hi.astype(F32)).astype(BF16)
    w_hi = w.astype(BF16)
    w_lo = (w - w_hi.astype(F32)).astype(BF16)
    dot = functools.partial(jnp.dot, preferred_element_type=F32)
    o_ref[...] = dot(a_hi, w_hi) + (dot(a_hi, w_lo) + dot(a_lo, w_hi)) + b_ref[...]


def _ada_call(c, w_ada, b_ada):
    n_out = 6 * D_MODEL
    tn = 1024
    return pl.pallas_call(
        _ada_kernel,
        grid=(n_out // tn,),
        in_specs=[pl.BlockSpec((BATCH, D_MODEL), lambda j: (0, 0)),
                  pl.BlockSpec((D_MODEL, tn), lambda j: (0, j)),
                  pl.BlockSpec((1, tn), lambda j: (0, j))],
        out_specs=pl.BlockSpec((BATCH, tn), lambda j: (0, j)),
        out_shape=jax.ShapeDtypeStruct((BATCH, n_out), F32),
        compiler_params=pltpu.CompilerParams(dimension_semantics=("arbitrary",),
                                             vmem_limit_bytes=VMEM_LIMIT),
        name="ada_mod",
    )(c, w_ada, b_ada.reshape(1, n_out))


def _rope_rows(blk, cos, sin, half):
    x1 = blk[0:half]
    x2 = blk[half:2 * half]
    return x1 * cos - x2 * sin, x2 * cos + x1 * sin


def _inproj_kernel(x_ref, mod_ref, pos_ref, n1g_ref, w_ref, lng_ref, lnb_ref,
                   kig_ref, kib_ref, invf_ref,
                   u_ref, vln_ref, sga_ref, sgb_ref, qT_ref, k_ref, vT_ref, qiT_ref, ki_ref, wiT_ref,
                   wg_ref, wt_ref):
    tm = x_ref.shape[0]

    @pl.when(pl.program_id(0) == 0)
    def _():
        wg_ref[:, 0:D_MODEL] = w_ref[:, _OFF_GA:_OFF_GB]
        wg_ref[:, D_MODEL:] = w_ref[:, _OFF_GB:_IN_COLS]
        for c in range(_T_PAD // 128):
            cols = slice(_OFF_Q + c * 128, _OFF_Q + (c + 1) * 128)
            wt_ref[c * 128:(c + 1) * 128, :] = w_ref[:, cols].astype(F32).T.astype(BF16)

    x = x_ref[...]
    mod = mod_ref[0]
    shift1 = mod[:, 0:D_MODEL]
    scale1 = mod[:, D_MODEL:2 * D_MODEL]
    h = _rms_norm(x, n1g_ref[...]) * (1.0 + scale1) + shift1
    hb = h.astype(BF16)

    zu = jnp.dot(hb, w_ref[:, _OFF_U:_OFF_V], preferred_element_type=F32)
    u_ref[...] = _gelu_tanh(zu).astype(BF16)
    zv = _gelu_tanh(jnp.dot(hb, w_ref[:, _OFF_V:_OFF_Q], preferred_element_type=F32))
    mu = jnp.mean(zv, axis=-1, keepdims=True)
    zc = zv - mu
    var = jnp.mean(zc * zc, axis=-1, keepdims=True)
    vln_ref[...] = (zc * lax.rsqrt(var + EPS) * lng_ref[...] + lnb_ref[...]).astype(BF16)
    zga = jnp.dot(hb, wg_ref[:, 0:D_MODEL], preferred_element_type=F32)
    sga_ref[...] = _sigmoid(zga).astype(BF16)
    zgb = jnp.dot(hb, wg_ref[:, D_MODEL:], preferred_element_type=F32)
    sgb_ref[...] = _sigmoid(zgb).astype(BF16)

    nt = (((1,), (1,)), ((), ()))
    pos = pos_ref[0].astype(F32)
    ang = invf_ref[...] * pos
    cos = jnp.cos(ang)
    sin = jnp.sin(ang)
    hq = ROT_DIM // 2
    hi = IDX_ROT_DIM // 2
    cos_q, sin_q = cos[0:hq], sin[0:hq]
    cos_i, sin_i = cos[hq:hq + hi], sin[hq:hq + hi]
    n_sub = tm // Q_TILE

    zt = lax.dot_general(wt_ref[0:_T_ROWS, :], hb, nt, preferred_element_type=F32)
    zq = zt[_T_Q:_T_K]
    q_scale = HEAD_DIM ** -0.5 * float(np.log2(np.e))
    for hd in range(N_HEADS):
        blk = zq[hd * HEAD_DIM:(hd + 1) * HEAD_DIM]
        r1, r2 = _rope_rows(blk, cos_q, sin_q, hq)
        full = (jnp.concatenate([r1, r2, blk[ROT_DIM:]], axis=0) * q_scale).astype(BF16)
        for s in range(n_sub):
            qT_ref[s, :, hd * Q_TILE:(hd + 1) * Q_TILE] = full[:, s * Q_TILE:(s + 1) * Q_TILE]

    zk = zt[_T_K:_T_V]
    k_rows = []
    for g in range(N_KV_HEADS):
        blk = zk[g * HEAD_DIM:(g + 1) * HEAD_DIM]
        r1, r2 = _rope_rows(blk, cos_q, sin_q, hq)
        k_rows += [r1, r2, blk[ROT_DIM:]]
    k_ref[...] = jnp.concatenate(k_rows, axis=0).T.astype(BF16)

    vT_ref[...] = zt[_T_V:_T_QI].astype(BF16)

    zqi = zt[_T_QI:_T_KI]
    for hd in range(IDX_HEADS):
        blk = zqi[hd * IDX_DIM:(hd + 1) * IDX_DIM]
        r1, r2 = _rope_rows(blk, cos_i, sin_i, hi)
        full = jnp.concatenate([r1, r2, blk[IDX_ROT_DIM:]], axis=0).astype(BF16)
        for s in range(n_sub):
            qiT_ref[s, :, hd * Q_TILE:(hd + 1) * Q_TILE] = full[:, s * Q_TILE:(s + 1) * Q_TILE]

    zrest = zt[_T_KI:_T_ROWS]
    zki = zrest[0:IDX_DIM]
    kmu = jnp.mean(zki, axis=0, keepdims=True)
    kc = zki - kmu
    kvar = jnp.mean(kc * kc, axis=0, keepdims=True)
    kin = kc * lax.rsqrt(kvar + EPS) * kig_ref[...] + kib_ref[...]
    r1, r2 = _rope_rows(kin, cos_i, sin_i, hi)
    ki_full = jnp.concatenate([r1, r2, kin[IDX_ROT_DIM:], jnp.zeros((IDX_K_LANES - IDX_DIM, tm), F32)], axis=0)
    ki_ref[...] = ki_full.T.astype(BF16)
    wiT_ref[...] = zrest[IDX_DIM:IDX_DIM + IDX_HEADS] * ((IDX_HEADS ** -0.5) * (IDX_DIM ** -0.5))


def _inproj_call(x2, mod3, pos3, n1g, w, lng, lnb, kig, kib, invf):
    tm = TM_IN
    n_tiles = TOKENS // tm
    per_b = SEQ // tm
    n_sub = tm // Q_TILE
    const2 = lambda t: (0, 0)
    row = lambda t: (t, 0)
    in_specs = [
        pl.BlockSpec((tm, D_MODEL), row),
        pl.BlockSpec((1, 1, 6 * D_MODEL), lambda t: (t // per_b, 0, 0)),
        pl.BlockSpec((1, 1, tm), lambda t: (t // per_b, 0, t % per_b)),
        pl.BlockSpec((1, D_MODEL), const2),
        pl.BlockSpec(w.shape, const2),
        pl.BlockSpec((1, A_WIDTH), const2),
        pl.BlockSpec((1, A_WIDTH), const2),
        pl.BlockSpec((IDX_DIM, 1), const2),
        pl.BlockSpec((IDX_DIM, 1), const2),
        pl.BlockSpec(invf.shape, const2),
    ]
    out_shape = [
        jax.ShapeDtypeStruct((TOKENS, A_WIDTH), BF16),
        jax.ShapeDtypeStruct((TOKENS, A_WIDTH), BF16),
        jax.ShapeDtypeStruct((TOKENS, D_MODEL), BF16),
        jax.ShapeDtypeStruct((TOKENS, D_MODEL), BF16),
        jax.ShapeDtypeStruct((TOKENS // Q_TILE, HEAD_DIM, N_HEADS * Q_TILE), BF16),
        jax.ShapeDtypeStruct((TOKENS, N_KV_HEADS * HEAD_DIM), BF16),
        jax.ShapeDtypeStruct((N_KV_HEADS * HEAD_DIM, TOKENS), BF16),
        jax.ShapeDtypeStruct((TOKENS // Q_TILE, IDX_DIM, IDX_HEADS * Q_TILE), BF16),
        jax.ShapeDtypeStruct((TOKENS, IDX_K_LANES), BF16),
        jax.ShapeDtypeStruct((IDX_HEADS, TOKENS), F32),
    ]
    out_specs = [
        pl.BlockSpec((tm, A_WIDTH), row),
        pl.BlockSpec((tm, A_WIDTH), row),
        pl.BlockSpec((tm, D_MODEL), row),
        pl.BlockSpec((tm, D_MODEL), row),
        pl.BlockSpec((n_sub, HEAD_DIM, N_HEADS * Q_TILE), lambda t: (t, 0, 0)),
        pl.BlockSpec((tm, N_KV_HEADS * HEAD_DIM), row),
        pl.BlockSpec((N_KV_HEADS * HEAD_DIM, tm), lambda t: (0, t)),
        pl.BlockSpec((n_sub, IDX_DIM, IDX_HEADS * Q_TILE), lambda t: (t, 0, 0)),
        pl.BlockSpec((tm, IDX_K_LANES), row),
        pl.BlockSpec((IDX_HEADS, tm), lambda t: (0, t)),
    ]
    return pl.pallas_call(
        _inproj_kernel,
        grid=(n_tiles,),
        in_specs=in_specs,
        out_specs=out_specs,
        out_shape=out_shape,
        scratch_shapes=[pltpu.VMEM((D_MODEL, 2 * D_MODEL), BF16),
                        pltpu.VMEM((_T_PAD, D_MODEL), BF16)],
        compiler_params=pltpu.CompilerParams(dimension_semantics=("arbitrary",),
                                             vmem_limit_bytes=VMEM_LIMIT),
        name="in_proj",
    )(x2, mod3, pos3, n1g, w, lng, lnb, kig, kib, invf)


def _col_reduce(x, op):
    return op(x.reshape(x.shape[0] // ACC_ROWS, ACC_ROWS, x.shape[1]), axis=0)


def _key_tiles(tile):
    return (tile * Q_TILE + Q_TILE + KEY_TILE - 1) // KEY_TILE


def _attn_kernel(qT_ref, qiT_ref, wiT_ref, k_ref, ki_ref, vT_ref, y_ref,
                 sc_ref, acc_ref, l_ref, vsel_ref):
    t = pl.program_id(1)
    nq = SEQ // Q_TILE
    i = jnp.minimum(t, nq - 1)
    ia = jnp.maximum(t - 1, 0)
    sc_i = sc_ref.at[i % 2]
    sc_a = sc_ref.at[ia % 2]
    nkt = jnp.where(t < nq, _key_tiles(i), 0)
    nka = jnp.where(t >= 1, _key_tiles(ia), 0)
    kf = float(TOPK)
    gw = HEADS_PER_KV * Q_TILE
    hw = N_HEADS * Q_TILE

    @pl.when(t == 0)
    def _():
        vsel_ref[...] = jnp.zeros_like(vsel_ref)

    qiT = qiT_ref[0]
    qT = qT_ref[0]
    w = wiT_ref[...]
    wrow = jnp.concatenate([w[hd:hd + 1, :] for hd in range(IDX_HEADS)], axis=1)
    vsel_a = vsel_ref[ia % 2]

    row_i = lax.broadcasted_iota(jnp.int32, (KEY_TILE, Q_TILE), 0)
    qidx = i * Q_TILE + lax.broadcasted_iota(jnp.int32, (KEY_TILE, Q_TILE), 1)

    def score_tile(j, mx_a, mn_a):
        sl = pl.ds(pl.multiple_of(j * KEY_TILE, KEY_TILE), KEY_TILE)
        lg = jnp.dot(ki_ref[sl, 0:IDX_DIM], qiT, preferred_element_type=F32)
        r = jnp.maximum(lg, 0.0) * wrow
        s = r[:, 0:Q_TILE]
        for hd in range(1, IDX_HEADS):
            s = s + r[:, hd * Q_TILE:(hd + 1) * Q_TILE]
        causal = j * KEY_TILE + row_i <= qidx
        sc_i[sl, :] = jnp.where(causal, s, -jnp.inf)
        mx_a = jnp.maximum(mx_a, _col_reduce(jnp.where(causal, s, -jnp.inf), jnp.max))
        mn_a = jnp.minimum(mn_a, _col_reduce(jnp.where(causal, s, jnp.inf), jnp.min))
        return mx_a, mn_a

    def key_mask(sl):
        return jnp.where(sc_a[sl, :] >= vsel_a, 0.0, NEG_INF)

    def qk_tile(j):
        sl = pl.ds(pl.multiple_of(j * KEY_TILE, KEY_TILE), KEY_TILE)
        return [jnp.dot(k_ref[sl, g * HEAD_DIM:(g + 1) * HEAD_DIM], qT[:, g * gw:(g + 1) * gw],
                        preferred_element_type=F32) for g in range(N_KV_HEADS)]

    def attend_tile(j, qk, l8, m):
        sl = pl.ds(pl.multiple_of(j * KEY_TILE, KEY_TILE), KEY_TILE)
        mask = key_mask(sl)
        parts = []
        for g in range(N_KV_HEADS):
            a = qk[g]
            ps = []
            for r in range(HEADS_PER_KV):
                e = a[:, r * Q_TILE:(r + 1) * Q_TILE] + mask
                if m is not None:
                    c0 = (g * HEADS_PER_KV + r) * Q_TILE
                    e = e - m[:, c0:c0 + Q_TILE]
                ps.append(jnp.exp2(e))
            p = jnp.concatenate(ps, axis=1)
            acc_ref[:, g * gw:(g + 1) * gw] += jnp.dot(
                vT_ref[g * HEAD_DIM:(g + 1) * HEAD_DIM, sl], p.astype(BF16), preferred_element_type=F32)
            parts.append(jnp.sum(p.reshape(KEY_TILE // 8, 8, gw), axis=0))
        return l8 + jnp.concatenate(parts, axis=1)

    acc_ref[...] = jnp.zeros_like(acc_ref)
    n_both = jnp.minimum(nkt, nka)

    def both_body(j, c):
        mx_a, mn_a, l8 = c
        mx_a, mn_a = score_tile(j, mx_a, mn_a)
        return mx_a, mn_a, attend_tile(j, qk_tile(j), l8, None)

    def score_body(j, c):
        mx_a, mn_a, l8 = c
        mx_a, mn_a = score_tile(j, mx_a, mn_a)
        return mx_a, mn_a, l8

    def attend_body(j, c):
        mx_a, mn_a, l8 = c
        return mx_a, mn_a, attend_tile(j, qk_tile(j), l8, None)

    carry = (jnp.full((ACC_ROWS, Q_TILE), -jnp.inf, F32), jnp.full((ACC_ROWS, Q_TILE), jnp.inf, F32),
             jnp.zeros((8, hw), F32))
    done = 0
    for width in SWEEP_UNROLL:
        def wide_body(jw, c, width=width, done=done):
            for u in range(width):
                c = both_body(done + jw * width + u, c)
            return c
        trips = (n_both - done) // width
        carry = lax.fori_loop(0, trips, wide_body, carry)
        done = done + trips * width
    carry = lax.fori_loop(n_both, nkt, score_body, carry)
    mx_a, mn_a, l8 = lax.fori_loop(n_both, nka, attend_body, carry)
    l_ref[...] = jnp.sum(l8, axis=0, keepdims=True)

    def tiles(fn, init):
        def body(j, carry):
            s = sc_i[pl.ds(pl.multiple_of(j * KEY_TILE, KEY_TILE), KEY_TILE), :]
            return fn(s, carry)
        return lax.fori_loop(0, nkt, body, init)

    def count_ge(t):
        def fold(s, c):
            for r in range(KEY_TILE // ACC_ROWS):
                c = jnp.where(s[r * ACC_ROWS:(r + 1) * ACC_ROWS] >= t, c + 1.0, c)
            return c
        c8 = tiles(fold, jnp.zeros((ACC_ROWS, Q_TILE), F32))
        return jnp.sum(c8, axis=0, keepdims=True)

    def bisect(state, n):
        def step(_, c):
            lo, hi, n_lo, n_hi = c
            mid = 0.5 * lo + 0.5 * hi
            n_mid = count_ge(mid)
            ok = n_mid >= kf
            return (jnp.where(ok, mid, lo), jnp.where(ok, hi, mid),
                    jnp.where(ok, n_mid, n_lo), jnp.where(ok, n_hi, n_mid))
        return lax.fori_loop(0, n, step, state)

    @pl.when(t < TOPK // Q_TILE)
    def _():
        vsel_ref[i % 2] = jnp.full((1, Q_TILE), jnp.finfo(jnp.float32).min, F32)

    @pl.when((t >= TOPK // Q_TILE) & (t < nq))
    def _():
        mx = jnp.max(mx_a, axis=0, keepdims=True)
        mn = jnp.min(mn_a, axis=0, keepdims=True)
        n_valid = (qidx[0:1, :] + 1).astype(F32)
        hi0 = mx + jnp.maximum(jnp.abs(mx), 1e-30) * 1e-6
        state = bisect((mn, hi0, n_valid, jnp.zeros((1, Q_TILE), F32)), FIRST_CHECK_AFTER - CHECK_EVERY)

        def check(state):
            lo, hi, n_lo, n_hi = state

            def f(s, c):
                a8, b8 = c
                a8 = jnp.maximum(a8, _col_reduce(jnp.where(s < hi, s, -jnp.inf), jnp.max))
                b8 = jnp.minimum(b8, _col_reduce(jnp.where(s >= lo, s, jnp.inf), jnp.min))
                return a8, b8
            a8, b8 = tiles(f, (jnp.full((ACC_ROWS, Q_TILE), -jnp.inf, F32), jnp.full((ACC_ROWS, Q_TILE), jnp.inf, F32)))
            top = jnp.max(a8, axis=0, keepdims=True)
            bottom = jnp.min(b8, axis=0, keepdims=True)
            single = top == bottom
            pinned = single | (n_lo - n_hi < 2.5)
            take_top = jnp.logical_not(single) & (kf - n_hi < 1.5)
            take_bottom = jnp.logical_not(single | take_top)
            v = jnp.where(take_top, top, bottom)
            n_ge = jnp.where(take_top, n_hi + 1.0, n_lo)
            n_gt = jnp.where(take_bottom, n_hi + 1.0, n_hi)
            return (v, n_ge, n_gt), jnp.min(jnp.where(pinned, 1.0, 0.0))

        def w_cond(c):
            return c[2] < 0.5

        def w_body(c):
            st = bisect(c[0], CHECK_EVERY)
            found, done = check(st)
            return st, found, done

        _, (vk, n_ge, n_gt), _ = lax.while_loop(w_cond, w_body, (state, (mn, n_valid, n_valid), jnp.float32(0.0)))
        need = kf - n_gt
        has_tie = jnp.max(n_ge) > kf + 0.5

        @pl.when(jnp.logical_not(has_tie))
        def _():
            vsel_ref[i % 2] = vk

        @pl.when(has_tie)
        def _():
            vsel_ref[i % 2] = jnp.full((1, Q_TILE), -0.5, F32)
            r_i = lax.broadcasted_iota(jnp.int32, (KEY_TILE, KEY_TILE), 0)
            c_i = lax.broadcasted_iota(jnp.int32, (KEY_TILE, KEY_TILE), 1)
            tri = jnp.where(r_i >= c_i, 1.0, 0.0).astype(BF16)

            def body(j, seen):
                sl = pl.ds(pl.multiple_of(j * KEY_TILE, KEY_TILE), KEY_TILE)
                s = sc_i[sl, :]
                eq = s == vk
                rank = jnp.dot(tri, jnp.where(eq, 1.0, 0.0).astype(BF16), preferred_element_type=F32) + seen
                sel = (s > vk) | (eq & (rank <= need))
                sc_i[sl, :] = jnp.where(sel, 0.0, NEG_INF)
                return rank[KEY_TILE - 1:KEY_TILE, :]
            lax.fori_loop(0, nkt, body, jnp.zeros((1, Q_TILE), F32))

    @pl.when(t >= 1)
    def _():
        l_fast = l_ref[...]
        in_range = (jnp.min(l_fast) > 1e-20) & (jnp.max(l_fast) < 1e30)

        @pl.when(jnp.logical_not(in_range))
        def _():
            def max_body(j, m8):
                sl = pl.ds(pl.multiple_of(j * KEY_TILE, KEY_TILE), KEY_TILE)
                mask = key_mask(sl)
                parts = []
                for g in range(N_KV_HEADS):
                    a = jnp.dot(k_ref[sl, g * HEAD_DIM:(g + 1) * HEAD_DIM], qT[:, g * gw:(g + 1) * gw],
                                preferred_element_type=F32)
                    a = a + jnp.concatenate([mask] * HEADS_PER_KV, axis=1)
                    parts.append(jnp.max(a.reshape(KEY_TILE // 8, 8, gw), axis=0))
                return jnp.maximum(m8, jnp.concatenate(parts, axis=1))
            m8 = lax.fori_loop(0, nka, max_body, jnp.full((8, hw), -jnp.inf, F32))
            m = jnp.max(m8, axis=0, keepdims=True)
            acc_ref[...] = jnp.zeros_like(acc_ref)
            l8 = lax.fori_loop(0, nka, lambda j, c: attend_tile(j, qk_tile(j), c, m), jnp.zeros((8, hw), F32))
            l_ref[...] = jnp.sum(l8, axis=0, keepdims=True)

        oT = acc_ref[...] / l_ref[...]
        for hd in range(N_HEADS):
            y_ref[:, hd * HEAD_DIM:(hd + 1) * HEAD_DIM] = oT[:, hd * Q_TILE:(hd + 1) * Q_TILE].T.astype(BF16)


def _attn_call(qT, qiT, wiT, k, ki, vT):
    nq = SEQ // Q_TILE
    scored = lambda b, t: b * nq + jnp.minimum(t, nq - 1)
    attended = lambda b, t: b * nq + jnp.maximum(t - 1, 0)
    return pl.pallas_call(
        _attn_kernel,
        grid=(BATCH, nq + 1),
        in_specs=[
            pl.BlockSpec((1, HEAD_DIM, N_HEADS * Q_TILE), lambda b, t: (attended(b, t), 0, 0)),
            pl.BlockSpec((1, IDX_DIM, IDX_HEADS * Q_TILE), lambda b, t: (scored(b, t), 0, 0)),
            pl.BlockSpec((IDX_HEADS, Q_TILE), lambda b, t: (0, scored(b, t))),
            pl.BlockSpec((SEQ, N_KV_HEADS * HEAD_DIM), lambda b, t: (b, 0)),
            pl.BlockSpec((SEQ, IDX_K_LANES), lambda b, t: (b, 0)),
            pl.BlockSpec((N_KV_HEADS * HEAD_DIM, SEQ), lambda b, t: (0, b)),
        ],
        out_specs=pl.BlockSpec((Q_TILE, N_HEADS * HEAD_DIM), lambda b, t: (attended(b, t), 0)),
        out_shape=jax.ShapeDtypeStruct((TOKENS, N_HEADS * HEAD_DIM), BF16),
        scratch_shapes=[pltpu.VMEM((2, SEQ, Q_TILE), F32),
                        pltpu.VMEM((HEAD_DIM, N_HEADS * Q_TILE), F32),
                        pltpu.VMEM((1, N_HEADS * Q_TILE), F32),
                        pltpu.VMEM((2, 1, Q_TILE), F32)],
        compiler_params=pltpu.CompilerParams(dimension_semantics=("arbitrary", "arbitrary"),
                                             vmem_limit_bytes=VMEM_LIMIT),
        name="dsa_attention",
    )(qT, qiT, wiT, k, ki, vT)


def _post_kernel(x_ref, mod_ref, u_ref, vln_ref, sga_ref, sgb_ref, yb_ref, ws_ref, bsT_ref,
                 wpa_ref, wpb_ref, wout_ref, n2g_ref, wfi_ref, wfo_ref, fg_ref, o_ref, ya_ref):
    tm = x_ref.shape[0]
    mod = mod_ref[0]
    gate1 = mod[:, 2 * D_MODEL:3 * D_MODEL]
    shift2 = mod[:, 3 * D_MODEL:4 * D_MODEL]
    scale2 = mod[:, 4 * D_MODEL:5 * D_MODEL]
    gate2 = mod[:, 5 * D_MODEL:6 * D_MODEL]

    r_i = lax.broadcasted_iota(jnp.int32, (CHUNK, CHUNK), 0)
    c_i = lax.broadcasted_iota(jnp.int32, (CHUNK, CHUNK), 1)
    bsT = bsT_ref[...]
    gd = A_WIDTH // A_GROUPS
    for g in range(A_GROUPS):
        wm = jnp.where(r_i >= c_i, ws_ref[g], 0.0).astype(BF16)
        bias = bsT[:, g:g + 1]
        for c in range(tm // CHUNK):
            rows = slice(c * CHUNK, (c + 1) * CHUNK)
            cols = slice(g * gd, (g + 1) * gd)
            mixed = jnp.dot(wm, vln_ref[rows, cols], preferred_element_type=F32) + bias
            ya_ref[rows, cols] = (u_ref[rows, cols].astype(F32) * mixed).astype(BF16)

    pa = jnp.dot(ya_ref[...], wpa_ref[...], preferred_element_type=F32)
    pb = jnp.dot(yb_ref[...], wpb_ref[...], preferred_element_type=F32)
    merged = sga_ref[...].astype(F32) * pa + sgb_ref[...].astype(F32) * pb
    x1 = x_ref[...] + gate1 * jnp.dot(merged.astype(BF16), wout_ref[...], preferred_element_type=F32)

    h2 = (_rms_norm(x1, n2g_ref[...]) * (1.0 + scale2) + shift2).astype(BF16)
    fg = jnp.dot(h2, wfi_ref[:, 0:D_FF], preferred_element_type=F32)
    fu = jnp.dot(h2, wfi_ref[:, D_FF:], preferred_element_type=F32)
    act = (fg * _sigmoid(fg) * fu).astype(BF16)
    x2 = x1 + gate2 * jnp.dot(act, wfo_ref[...], preferred_element_type=F32)
    o_ref[...] = _rms_norm(x2, fg_ref[...])


def _post_call(x2, mod3, u, vln, sga, sgb, yb, ws, bsT, wpa, wpb, wout, n2g, wfi, wfo, fg):
    tm = TM_OUT
    per_b = SEQ // tm
    row = lambda t: (t, 0)
    const2 = lambda t: (0, 0)
    tok = lambda: pl.BlockSpec((tm, D_MODEL), row)
    in_specs = [
        tok(),
        pl.BlockSpec((1, 1, 6 * D_MODEL), lambda t: (t // per_b, 0, 0)),
        tok(), tok(), tok(), tok(), tok(),
        pl.BlockSpec(ws.shape, lambda t: (0, 0, 0)),
        pl.BlockSpec(bsT.shape, const2),
        pl.BlockSpec(wpa.shape, const2),
        pl.BlockSpec(wpb.shape, const2),
        pl.BlockSpec(wout.shape, const2),
        pl.BlockSpec((1, D_MODEL), const2),
        pl.BlockSpec(wfi.shape, const2),
        pl.BlockSpec(wfo.shape, const2),
        pl.BlockSpec((1, D_MODEL), const2),
    ]
    return pl.pallas_call(
        _post_kernel,
        grid=(TOKENS // tm,),
        in_specs=in_specs,
        out_specs=pl.BlockSpec((tm, D_MODEL), row),
        out_shape=jax.ShapeDtypeStruct((TOKENS, D_MODEL), F32),
        scratch_shapes=[pltpu.VMEM((tm, A_WIDTH), BF16)],
        compiler_params=pltpu.CompilerParams(dimension_semantics=("arbitrary",),
                                             vmem_limit_bytes=VMEM_LIMIT),
        name="merge_ffn",
    )(x2, mod3, u, vln, sga, sgb, yb, ws, bsT, wpa, wpb, wout, n2g, wfi, wfo, fg)


def _inv_freq_table():
    fq = ROPE_THETA ** (-np.arange(0, ROT_DIM, 2, dtype=np.float32) / ROT_DIM)
    fi = ROPE_THETA ** (-np.arange(0, IDX_ROT_DIM, 2, dtype=np.float32) / IDX_ROT_DIM)
    return np.concatenate([fq, fi]).astype(np.float32).reshape(-1, 1)


def kernel(x, c, positions, w_ada, b_ada, norm1_g, w_in, gmlp_ln_g, gmlp_ln_b, gmlp_w_s, gmlp_b_s,
           idx_k_ln_g, idx_k_ln_b, w_proj_a, w_proj_b, w_out, norm2_g, w_ffn_in, w_ffn_out, final_norm_g):
    assert x.shape == (BATCH, SEQ, D_MODEL) and w_in.shape == (1, D_MODEL, _IN_COLS)
    x2 = x.reshape(TOKENS, D_MODEL)
    pos3 = positions.reshape(BATCH, 1, SEQ)
    invf = jnp.asarray(_inv_freq_table())
    xcur = x2
    for l in range(w_ada.shape[0]):
        mod = _ada_call(c, w_ada[l], b_ada[l])
        mod3 = mod.reshape(BATCH, 1, 6 * D_MODEL)
        u, vln, sga, sgb, qT, k, vT, qiT, ki, wiT = _inproj_call(
            xcur, mod3, pos3, norm1_g[l].reshape(1, -1), w_in[l].astype(BF16),
            gmlp_ln_g[l].reshape(1, -1), gmlp_ln_b[l].reshape(1, -1),
            idx_k_ln_g[l].reshape(-1, 1), idx_k_ln_b[l].reshape(-1, 1), invf)
        yb = _attn_call(qT, qiT, wiT, k, ki, vT)
        last = l == w_ada.shape[0] - 1
        fg = final_norm_g.reshape(1, -1) if last else None
        assert last, "single-layer block"
        xcur = _post_call(xcur, mod3, u, vln, sga, sgb, yb, gmlp_w_s[l], gmlp_b_s[l].T,
                          w_proj_a[l].astype(BF16), w_proj_b[l].astype(BF16), w_out[l].astype(BF16),
                          norm2_g[l].reshape(1, -1), w_ffn_in[l].astype(BF16), w_ffn_out[l].astype(BF16), fg)
    return xcur.reshape(BATCH, SEQ, D_MODEL)
```

```python
import functools

import numpy as np
import jax
import jax.numpy as jnp
from jax import lax
from jax.experimental import pallas as pl
from jax.experimental.pallas import tpu as pltpu

D_MODEL = 1024
BATCH = 4
SEQ = 4096
CHUNK = 128
A_GROUPS = 8
A_WIDTH = 1024
N_HEADS = 8
N_KV_HEADS = 2
HEAD_DIM = 128
IDX_HEADS = 8
IDX_DIM = 64
TOPK = 256
ROPE_THETA = 500000.0
ROT_DIM = HEAD_DIM // 4
IDX_ROT_DIM = IDX_DIM // 4
D_FF = 2816
EPS = 1e-6
NEG_INF = -1e30

TOKENS = BATCH * SEQ
Q_TILE = 128
KEY_TILE = 512
IDX_K_LANES = 128
ACC_ROWS = 32
FIRST_CHECK_AFTER = 16
CHECK_EVERY = 2
SWEEP_UNROLL = (4, 2, 1)
TM_IN = 512
LATER_WEIGHT_ROWS = (D_MODEL, D_MODEL, D_MODEL, D_MODEL, D_FF)
TM_OUT = 512
HEADS_PER_KV = N_HEADS // N_KV_HEADS
VMEM_LIMIT = 56 * 1024 * 1024

_OFF_U = 0
_OFF_V = _OFF_U + A_WIDTH
_OFF_Q = _OFF_V + A_WIDTH
_OFF_K = _OFF_Q + N_HEADS * HEAD_DIM
_OFF_VV = _OFF_K + N_KV_HEADS * HEAD_DIM
_OFF_QI = _OFF_VV + N_KV_HEADS * HEAD_DIM
_OFF_KI = _OFF_QI + IDX_HEADS * IDX_DIM
_OFF_WI = _OFF_KI + IDX_DIM
_OFF_GA = _OFF_WI + IDX_HEADS
_OFF_GB = _OFF_GA + D_MODEL
_IN_COLS = _OFF_GB + D_MODEL

_T_Q = 0
_T_K = _T_Q + N_HEADS * HEAD_DIM
_T_V = _T_K + N_KV_HEADS * HEAD_DIM
_T_QI = _T_V + N_KV_HEADS * HEAD_DIM
_T_KI = _T_QI + IDX_HEADS * IDX_DIM
_T_WI = _T_KI + IDX_DIM
_T_ROWS = _T_WI + IDX_HEADS
_T_PAD = -(-_T_ROWS // 128) * 128

F32 = jnp.float32
BF16 = jnp.bfloat16


def _gelu_tanh(x):
    return 0.5 * x * (1.0 + jnp.tanh(np.sqrt(2.0 / np.pi).astype(np.float32) * (x + 0.044715 * (x * x * x))))


def _sigmoid(x):
    return 1.0 / (1.0 + jnp.exp(-x))


def _rms_norm(x, g):
    return x * lax.rsqrt(jnp.mean(x * x, axis=-1, keepdims=True) + EPS) * g


def _ada_kernel(c_ref, w_ref, b_ref, o_ref):
    c = c_ref[...]
    a = c * _sigmoid(c)
    w = w_ref[...]
    a_hi = a.astype(BF16)
    a_lo = (a - a_hi.astype(F32)).astype(BF16)
    w_hi = w.astype(BF16)
    w_lo = (w - w_hi.astype(F32)).astype(BF16)
    dot = functools.partial(jnp.dot, preferred_element_type=F32)
    o_ref[...] = dot(a_hi, w_hi) + (dot(a_hi, w_lo) + dot(a_lo, w_hi)) + b_ref[...]


def _ada_call(c, w_ada, b_ada):
    n_out = 6 * D_MODEL
    tn = 1024
    return pl.pallas_call(
        _ada_kernel,
        grid=(n_out // tn,),
        in_specs=[pl.BlockSpec((BATCH, D_MODEL), lambda j: (0, 0)),
                  pl.BlockSpec((D_MODEL, tn), lambda j: (0, j)),
                  pl.BlockSpec((1, tn), lambda j: (0, j))],
        out_specs=pl.BlockSpec((BATCH, tn), lambda j: (0, j)),
        out_shape=jax.ShapeDtypeStruct((BATCH, n_out), F32),
        compiler_params=pltpu.CompilerParams(dimension_semantics=("arbitrary",),
                                             vmem_limit_bytes=VMEM_LIMIT),
        name="ada_mod",
    )(c, w_ada, b_ada.reshape(1, n_out))


def _rope_rows(blk, cos, sin, half):
    x1 = blk[0:half]
    x2 = blk[half:2 * half]
    return x1 * cos - x2 * sin, x2 * cos + x1 * sin


def _inproj_kernel(x_ref, mod_ref, pos_ref, n1g_ref, w_ref, lng_ref, lnb_ref,
                   kig_ref, kib_ref, invf_ref, *rest):
    n_later = len(LATER_WEIGHT_ROWS)
    later_f32 = rest[:n_later]
    (u_ref, vln_ref, sga_ref, sgb_ref, qT_ref, k_ref, vT_ref, qiT_ref, ki_ref, wiT_ref) = rest[n_later:n_later + 10]
    later_bf16 = rest[n_later + 10:2 * n_later + 10]
    wg_ref, wt_ref = rest[2 * n_later + 10:]
    tm = x_ref.shape[0]

    for src, dst in zip(later_f32, later_bf16):
        dst[...] = src[...].astype(BF16)

    @pl.when(pl.program_id(0) == 0)
    def _():
        wg_ref[:, 0:D_MODEL] = w_ref[:, _OFF_GA:_OFF_GB]
        wg_ref[:, D_MODEL:] = w_ref[:, _OFF_GB:_IN_COLS]
        for c in range(_T_PAD // 128):
            cols = slice(_OFF_Q + c * 128, _OFF_Q + (c + 1) * 128)
            wt_ref[c * 128:(c + 1) * 128, :] = w_ref[:, cols].astype(F32).T.astype(BF16)

    x = x_ref[...]
    mod = mod_ref[0]
    shift1 = mod[:, 0:D_MODEL]
    scale1 = mod[:, D_MODEL:2 * D_MODEL]
    h = _rms_norm(x, n1g_ref[...]) * (1.0 + scale1) + shift1
    hb = h.astype(BF16)

    zu = jnp.dot(hb, w_ref[:, _OFF_U:_OFF_V], preferred_element_type=F32)
    u_ref[...] = _gelu_tanh(zu).astype(BF16)
    zv = _gelu_tanh(jnp.dot(hb, w_ref[:, _OFF_V:_OFF_Q], preferred_element_type=F32))
    mu = jnp.mean(zv, axis=-1, keepdims=True)
    zc = zv - mu
    var = jnp.mean(zc * zc, axis=-1, keepdims=True)
    vln_ref[...] = (zc * lax.rsqrt(var + EPS) * lng_ref[...] + lnb_ref[...]).astype(BF16)
    zga = jnp.dot(hb, wg_ref[:, 0:D_MODEL], preferred_element_type=F32)
    sga_ref[...] = _sigmoid(zga).astype(BF16)
    zgb = jnp.dot(hb, wg_ref[:, D_MODEL:], preferred_element_type=F32)
    sgb_ref[...] = _sigmoid(zgb).astype(BF16)

    nt = (((1,), (1,)), ((), ()))
    pos = pos_ref[0].astype(F32)
    ang = invf_ref[...] * pos
    cos = jnp.cos(ang)
    sin = jnp.sin(ang)
    hq = ROT_DIM // 2
    hi = IDX_ROT_DIM // 2
    cos_q, sin_q = cos[0:hq], sin[0:hq]
    cos_i, sin_i = cos[hq:hq + hi], sin[hq:hq + hi]
    n_sub = tm // Q_TILE

    zt = lax.dot_general(wt_ref[0:_T_ROWS, :], hb, nt, preferred_element_type=F32)
    zq = zt[_T_Q:_T_K]
    q_scale = HEAD_DIM ** -0.5 * float(np.log2(np.e))
    for hd in range(N_HEADS):
        blk = zq[hd * HEAD_DIM:(hd + 1) * HEAD_DIM]
        r1, r2 = _rope_rows(blk, cos_q, sin_q, hq)
        full = (jnp.concatenate([r1, r2, blk[ROT_DIM:]], axis=0) * q_scale).astype(BF16)
        for s in range(n_sub):
            qT_ref[s, :, hd * Q_TILE:(hd + 1) * Q_TILE] = full[:, s * Q_TILE:(s + 1) * Q_TILE]

    zk = zt[_T_K:_T_V]
    k_rows = []
    for g in range(N_KV_HEADS):
        blk = zk[g * HEAD_DIM:(g + 1) * HEAD_DIM]
        r1, r2 = _rope_rows(blk, cos_q, sin_q, hq)
        k_rows += [r1, r2, blk[ROT_DIM:]]
    k_ref[...] = jnp.concatenate(k_rows, axis=0).T.astype(BF16)

    vT_ref[...] = zt[_T_V:_T_QI].astype(BF16)

    zqi = zt[_T_QI:_T_KI]
    for hd in range(IDX_HEADS):
        blk = zqi[hd * IDX_DIM:(hd + 1) * IDX_DIM]
        r1, r2 = _rope_rows(blk, cos_i, sin_i, hi)
        full = jnp.concatenate([r1, r2, blk[IDX_ROT_DIM:]], axis=0).astype(BF16)
        for s in range(n_sub):
            qiT_ref[s, :, hd * Q_TILE:(hd + 1) * Q_TILE] = full[:, s * Q_TILE:(s + 1) * Q_TILE]

    zrest = zt[_T_KI:_T_ROWS]
    zki = zrest[0:IDX_DIM]
    kmu = jnp.mean(zki, axis=0, keepdims=True)
    kc = zki - kmu
    kvar = jnp.mean(kc * kc, axis=0, keepdims=True)
    kin = kc * lax.rsqrt(kvar + EPS) * kig_ref[...] + kib_ref[...]
    r1, r2 = _rope_rows(kin, cos_i, sin_i, hi)
    ki_full = jnp.concatenate([r1, r2, kin[IDX_ROT_DIM:], jnp.zeros((IDX_K_LANES - IDX_DIM, tm), F32)], axis=0)
    ki_ref[...] = ki_full.T.astype(BF16)
    wiT_ref[...] = zrest[IDX_DIM:IDX_DIM + IDX_HEADS] * ((IDX_HEADS ** -0.5) * (IDX_DIM ** -0.5))


def _inproj_call(x2, mod3, pos3, n1g, w, lng, lnb, kig, kib, invf, later_weights):
    tm = TM_IN
    n_tiles = TOKENS // tm
    assert tuple(a.shape[0] for a in later_weights) == LATER_WEIGHT_ROWS

    def slab_spec(a):
        rows = a.shape[0]
        slab = next(r for r in range(16, rows + 1, 16) if rows % r == 0 and rows // r <= n_tiles)
        n_slabs = rows // slab
        return pl.BlockSpec((slab, a.shape[1]), lambda t: (t * n_slabs // n_tiles, 0))
    per_b = SEQ // tm
    n_sub = tm // Q_TILE
    const2 = lambda t: (0, 0)
    row = lambda t: (t, 0)
    in_specs = [
        pl.BlockSpec((tm, D_MODEL), row),
        pl.BlockSpec((1, 1, 6 * D_MODEL), lambda t: (t // per_b, 0, 0)),
        pl.BlockSpec((1, 1, tm), lambda t: (t // per_b, 0, t % per_b)),
        pl.BlockSpec((1, D_MODEL), const2),
        pl.BlockSpec(w.shape, const2),
        pl.BlockSpec((1, A_WIDTH), const2),
        pl.BlockSpec((1, A_WIDTH), const2),
        pl.BlockSpec((IDX_DIM, 1), const2),
        pl.BlockSpec((IDX_DIM, 1), const2),
        pl.BlockSpec(invf.shape, const2),
    ] + [slab_spec(a) for a in later_weights]
    out_shape = [
        jax.ShapeDtypeStruct((TOKENS, A_WIDTH), BF16),
        jax.ShapeDtypeStruct((TOKENS, A_WIDTH), BF16),
        jax.ShapeDtypeStruct((TOKENS, D_MODEL), BF16),
        jax.ShapeDtypeStruct((TOKENS, D_MODEL), BF16),
        jax.ShapeDtypeStruct((TOKENS // Q_TILE, HEAD_DIM, N_HEADS * Q_TILE), BF16),
        jax.ShapeDtypeStruct((TOKENS, N_KV_HEADS * HEAD_DIM), BF16),
        jax.ShapeDtypeStruct((N_KV_HEADS * HEAD_DIM, TOKENS), BF16),
        jax.ShapeDtypeStruct((TOKENS // Q_TILE, IDX_DIM, IDX_HEADS * Q_TILE), BF16),
        jax.ShapeDtypeStruct((TOKENS, IDX_K_LANES), BF16),
        jax.ShapeDtypeStruct((IDX_HEADS, TOKENS), F32),
    ]
    out_specs = [
        pl.BlockSpec((tm, A_WIDTH), row),
        pl.BlockSpec((tm, A_WIDTH), row),
        pl.BlockSpec((tm, D_MODEL), row),
        pl.BlockSpec((tm, D_MODEL), row),
        pl.BlockSpec((n_sub, HEAD_DIM, N_HEADS * Q_TILE), lambda t: (t, 0, 0)),
        pl.BlockSpec((tm, N_KV_HEADS * HEAD_DIM), row),
        pl.BlockSpec((N_KV_HEADS * HEAD_DIM, tm), lambda t: (0, t)),
        pl.BlockSpec((n_sub, IDX_DIM, IDX_HEADS * Q_TILE), lambda t: (t, 0, 0)),
        pl.BlockSpec((tm, IDX_K_LANES), row),
        pl.BlockSpec((IDX_HEADS, tm), lambda t: (0, t)),
    ] + [slab_spec(a) for a in later_weights]
    out_shape = out_shape + [jax.ShapeDtypeStruct(a.shape, BF16) for a in later_weights]
    return pl.pallas_call(
        _inproj_kernel,
        grid=(n_tiles,),
        in_specs=in_specs,
        out_specs=out_specs,
        out_shape=out_shape,
        scratch_shapes=[pltpu.VMEM((D_MODEL, 2 * D_MODEL), BF16),
                        pltpu.VMEM((_T_PAD, D_MODEL), BF16)],
        compiler_params=pltpu.CompilerParams(dimension_semantics=("arbitrary",),
                                             vmem_limit_bytes=VMEM_LIMIT),
        name="in_proj",
    )(x2, mod3, pos3, n1g, w, lng, lnb, kig, kib, invf, *later_weights)


def _col_reduce(x, op):
    return op(x.reshape(x.shape[0] // ACC_ROWS, ACC_ROWS, x.shape[1]), axis=0)


def _key_tiles(tile):
    return (tile * Q_TILE + Q_TILE + KEY_TILE - 1) // KEY_TILE


def _attn_kernel(qT_ref, qiT_ref, wiT_ref, k_ref, ki_ref, vT_ref, y_ref,
                 sc_ref, acc_ref, l_ref, vsel_ref):
    t = pl.program_id(1)
    nq = SEQ // Q_TILE
    i = jnp.minimum(t, nq - 1)
    ia = jnp.maximum(t - 1, 0)
    sc_i = sc_ref.at[i % 2]
    sc_a = sc_ref.at[ia % 2]
    nkt = jnp.where(t < nq, _key_tiles(i), 0)
    nka = jnp.where(t >= 1, _key_tiles(ia), 0)
    kf = float(TOPK)
    gw = HEADS_PER_KV * Q_TILE
    hw = N_HEADS * Q_TILE

    @pl.when(t == 0)
    def _():
        vsel_ref[...] = jnp.zeros_like(vsel_ref)

    qiT = qiT_ref[0]
    qT = qT_ref[0]
    w = wiT_ref[...]
    wrow = jnp.concatenate([w[hd:hd + 1, :] for hd in range(IDX_HEADS)], axis=1)
    vsel_a = vsel_ref[ia % 2]

    row_i = lax.broadcasted_iota(jnp.int32, (KEY_TILE, Q_TILE), 0)
    qidx = i * Q_TILE + lax.broadcasted_iota(jnp.int32, (KEY_TILE, Q_TILE), 1)

    def score_tile(j, mx_a, mn_a):
        sl = pl.ds(pl.multiple_of(j * KEY_TILE, KEY_TILE), KEY_TILE)
        lg = jnp.dot(ki_ref[sl, 0:IDX_DIM], qiT, preferred_element_type=F32)
        r = jnp.maximum(lg, 0.0) * wrow
        s = r[:, 0:Q_TILE]
        for hd in range(1, IDX_HEADS):
            s = s + r[:, hd * Q_TILE:(hd + 1) * Q_TILE]
        causal = j * KEY_TILE + row_i <= qidx
        sc_i[sl, :] = jnp.where(causal, s, -jnp.inf)
        mx_a = jnp.maximum(mx_a, _col_reduce(jnp.where(causal, s, -jnp.inf), jnp.max))
        mn_a = jnp.minimum(mn_a, _col_reduce(jnp.where(causal, s, jnp.inf), jnp.min))
        return mx_a, mn_a

    def key_mask(sl):
        return jnp.where(sc_a[sl, :] >= vsel_a, 0.0, NEG_INF)

    def qk_tile(j):
        sl = pl.ds(pl.multiple_of(j * KEY_TILE, KEY_TILE), KEY_TILE)
        return [jnp.dot(k_ref[sl, g * HEAD_DIM:(g + 1) * HEAD_DIM], qT[:, g * gw:(g + 1) * gw],
                        preferred_element_type=F32) for g in range(N_KV_HEADS)]

    def attend_tile(j, qk, l8, m):
        sl = pl.ds(pl.multiple_of(j * KEY_TILE, KEY_TILE), KEY_TILE)
        mask = key_mask(sl)
        parts = []
        for g in range(N_KV_HEADS):
            a = qk[g]
            ps = []
            for r in range(HEADS_PER_KV):
                e = a[:, r * Q_TILE:(r + 1) * Q_TILE] + mask
                if m is not None:
                    c0 = (g * HEADS_PER_KV + r) * Q_TILE
                    e = e - m[:, c0:c0 + Q_TILE]
                ps.append(jnp.exp2(e))
            p = jnp.concatenate(ps, axis=1)
            acc_ref[:, g * gw:(g + 1) * gw] += jnp.dot(
                vT_ref[g * HEAD_DIM:(g + 1) * HEAD_DIM, sl], p.astype(BF16), preferred_element_type=F32)
            parts.append(jnp.sum(p.reshape(KEY_TILE // 8, 8, gw), axis=0))
        return l8 + jnp.concatenate(parts, axis=1)

    acc_ref[...] = jnp.zeros_like(acc_ref)
    n_both = jnp.minimum(nkt, nka)

    def both_body(j, c):
        mx_a, mn_a, l8 = c
        mx_a, mn_a = score_tile(j, mx_a, mn_a)
        return mx_a, mn_a, attend_tile(j, qk_tile(j), l8, None)

    def score_body(j, c):
        mx_a, mn_a, l8 = c
        mx_a, mn_a = score_tile(j, mx_a, mn_a)
        return mx_a, mn_a, l8

    def attend_body(j, c):
        mx_a, mn_a, l8 = c
        return mx_a, mn_a, attend_tile(j, qk_tile(j), l8, None)

    carry = (jnp.full((ACC_ROWS, Q_TILE), -jnp.inf, F32), jnp.full((ACC_ROWS, Q_TILE), jnp.inf, F32),
             jnp.zeros((8, hw), F32))
    done = 0
    for width in SWEEP_UNROLL:
        def wide_body(jw, c, width=width, done=done):
            for u in range(width):
                c = both_body(done + jw * width + u, c)
            return c
        trips = (n_both - done) // width
        carry = lax.fori_loop(0, trips, wide_body, carry)
        done = done + trips * width
    carry = lax.fori_loop(n_both, nkt, score_body, carry)
    mx_a, mn_a, l8 = lax.fori_loop(n_both, nka, attend_body, carry)
    l_ref[...] = jnp.sum(l8, axis=0, keepdims=True)

    def tiles(fn, init):
        def body(j, carry):
            s = sc_i[pl.ds(pl.multiple_of(j * KEY_TILE, KEY_TILE), KEY_TILE), :]
            return fn(s, carry)
        return lax.fori_loop(0, nkt, body, init)

    def count_ge(t):
        def fold(s, c):
            for r in range(KEY_TILE // ACC_ROWS):
                c = jnp.where(s[r * ACC_ROWS:(r + 1) * ACC_ROWS] >= t, c + 1.0, c)
            return c
        c8 = tiles(fold, jnp.zeros((ACC_ROWS, Q_TILE), F32))
        return jnp.sum(c8, axis=0, keepdims=True)

    def bisect(state, n):
        def step(_, c):
            lo, hi, n_lo, n_hi = c
            mid = 0.5 * lo + 0.5 * hi
            n_mid = count_ge(mid)
            ok = n_mid >= kf
            return (jnp.where(ok, mid, lo), jnp.where(ok, hi, mid),
                    jnp.where(ok, n_mid, n_lo), jnp.where(ok, n_hi, n_mid))
        return lax.fori_loop(0, n, step, state)

    @pl.when(t < TOPK // Q_TILE)
    def _():
        vsel_ref[i % 2] = jnp.full((1, Q_TILE), jnp.finfo(jnp.float32).min, F32)

    @pl.when((t >= TOPK // Q_TILE) & (t < nq))
    def _():
        mx = jnp.max(mx_a, axis=0, keepdims=True)
        mn = jnp.min(mn_a, axis=0, keepdims=True)
        n_valid = (qidx[0:1, :] + 1).astype(F32)
        hi0 = mx + jnp.maximum(jnp.abs(mx), 1e-30) * 1e-6
        state = bisect((mn, hi0, n_valid, jnp.zeros((1, Q_TILE), F32)), FIRST_CHECK_AFTER - CHECK_EVERY)

        def check(state):
            lo, hi, n_lo, n_hi = state

            def f(s, c):
                a8, b8 = c
                a8 = jnp.maximum(a8, _col_reduce(jnp.where(s < hi, s, -jnp.inf), jnp.max))
                b8 = jnp.minimum(b8, _col_reduce(jnp.where(s >= lo, s, jnp.inf), jnp.min))
                return a8, b8
            a8, b8 = tiles(f, (jnp.full((ACC_ROWS, Q_TILE), -jnp.inf, F32), jnp.full((ACC_ROWS, Q_TILE), jnp.inf, F32)))
            top = jnp.max(a8, axis=0, keepdims=True)
            bottom = jnp.min(b8, axis=0, keepdims=True)
            single = top == bottom
            pinned = single | (n_lo - n_hi < 2.5)
            take_top = jnp.logical_not(single) & (kf - n_hi < 1.5)
            take_bottom = jnp.logical_not(single | take_top)
            v = jnp.where(take_top, top, bottom)
            n_ge = jnp.where(take_top, n_hi + 1.0, n_lo)
            n_gt = jnp.where(take_bottom, n_hi + 1.0, n_hi)
            return (v, n_ge, n_gt), jnp.min(jnp.where(pinned, 1.0, 0.0))

        def w_cond(c):
            return c[2] < 0.5

        def w_body(c):
            st = bisect(c[0], CHECK_EVERY)
            found, done = check(st)
            return st, found, done

        _, (vk, n_ge, n_gt), _ = lax.while_loop(w_cond, w_body, (state, (mn, n_valid, n_valid), jnp.float32(0.0)))
        need = kf - n_gt
        has_tie = jnp.max(n_ge) > kf + 0.5

        @pl.when(jnp.logical_not(has_tie))
        def _():
            vsel_ref[i % 2] = vk

        @pl.when(has_tie)
        def _():
            vsel_ref[i % 2] = jnp.full((1, Q_TILE), -0.5, F32)
            r_i = lax.broadcasted_iota(jnp.int32, (KEY_TILE, KEY_TILE), 0)
            c_i = lax.broadcasted_iota(jnp.int32, (KEY_TILE, KEY_TILE), 1)
            tri = jnp.where(r_i >= c_i, 1.0, 0.0).astype(BF16)

            def body(j, seen):
                sl = pl.ds(pl.multiple_of(j * KEY_TILE, KEY_TILE), KEY_TILE)
                s = sc_i[sl, :]
                eq = s == vk
                rank = jnp.dot(tri, jnp.where(eq, 1.0, 0.0).astype(BF16), preferred_element_type=F32) + seen
                sel = (s > vk) | (eq & (rank <= need))
                sc_i[sl, :] = jnp.where(sel, 0.0, NEG_INF)
                return rank[KEY_TILE - 1:KEY_TILE, :]
            lax.fori_loop(0, nkt, body, jnp.zeros((1, Q_TILE), F32))

    @pl.when(t >= 1)
    def _():
        l_fast = l_ref[...]
        in_range = (jnp.min(l_fast) > 1e-20) & (jnp.max(l_fast) < 1e30)

        @pl.when(jnp.logical_not(in_range))
        def _():
            def max_body(j, m8):
                sl = pl.ds(pl.multiple_of(j * KEY_TILE, KEY_TILE), KEY_TILE)
                mask = key_mask(sl)
                parts = []
                for g in range(N_KV_HEADS):
                    a = jnp.dot(k_ref[sl, g * HEAD_DIM:(g + 1) * HEAD_DIM], qT[:, g * gw:(g + 1) * gw],
                                preferred_element_type=F32)
                    a = a + jnp.concatenate([mask] * HEADS_PER_KV, axis=1)
                    parts.append(jnp.max(a.reshape(KEY_TILE // 8, 8, gw), axis=0))
                return jnp.maximum(m8, jnp.concatenate(parts, axis=1))
            m8 = lax.fori_loop(0, nka, max_body, jnp.full((8, hw), -jnp.inf, F32))
            m = jnp.max(m8, axis=0, keepdims=True)
            acc_ref[...] = jnp.zeros_like(acc_ref)
            l8 = lax.fori_loop(0, nka, lambda j, c: attend_tile(j, qk_tile(j), c, m), jnp.zeros((8, hw), F32))
            l_ref[...] = jnp.sum(l8, axis=0, keepdims=True)

        oT = acc_ref[...] / l_ref[...]
        for hd in range(N_HEADS):
            y_ref[:, hd * HEAD_DIM:(hd + 1) * HEAD_DIM] = oT[:, hd * Q_TILE:(hd + 1) * Q_TILE].T.astype(BF16)


def _attn_call(qT, qiT, wiT, k, ki, vT):
    nq = SEQ // Q_TILE
    scored = lambda b, t: b * nq + jnp.minimum(t, nq - 1)
    attended = lambda b, t: b * nq + jnp.maximum(t - 1, 0)
    return pl.pallas_call(
        _attn_kernel,
        grid=(BATCH, nq + 1),
        in_specs=[
            pl.BlockSpec((1, HEAD_DIM, N_HEADS * Q_TILE), lambda b, t: (attended(b, t), 0, 0)),
            pl.BlockSpec((1, IDX_DIM, IDX_HEADS * Q_TILE), lambda b, t: (scored(b, t), 0, 0)),
            pl.BlockSpec((IDX_HEADS, Q_TILE), lambda b, t: (0, scored(b, t))),
            pl.BlockSpec((SEQ, N_KV_HEADS * HEAD_DIM), lambda b, t: (b, 0)),
            pl.BlockSpec((SEQ, IDX_K_LANES), lambda b, t: (b, 0)),
            pl.BlockSpec((N_KV_HEADS * HEAD_DIM, SEQ), lambda b, t: (0, b)),
        ],
        out_specs=pl.BlockSpec((Q_TILE, N_HEADS * HEAD_DIM), lambda b, t: (attended(b, t), 0)),
        out_shape=jax.ShapeDtypeStruct((TOKENS, N_HEADS * HEAD_DIM), BF16),
        scratch_shapes=[pltpu.VMEM((2, SEQ, Q_TILE), F32),
                        pltpu.VMEM((HEAD_DIM, N_HEADS * Q_TILE), F32),
                        pltpu.VMEM((1, N_HEADS * Q_TILE), F32),
                        pltpu.VMEM((2, 1, Q_TILE), F32)],
        compiler_params=pltpu.CompilerParams(dimension_semantics=("arbitrary", "arbitrary"),
                                             vmem_limit_bytes=VMEM_LIMIT),
        name="dsa_attention",
    )(qT, qiT, wiT, k, ki, vT)


def _post_kernel(x_ref, mod_ref, u_ref, vln_ref, sga_ref, sgb_ref, yb_ref, ws_ref, bsT_ref,
                 wpa_ref, wpb_ref, wout_ref, n2g_ref, wfi_ref, wfo_ref, fg_ref, o_ref, ya_ref):
    tm = x_ref.shape[0]
    mod = mod_ref[0]
    gate1 = mod[:, 2 * D_MODEL:3 * D_MODEL]
    shift2 = mod[:, 3 * D_MODEL:4 * D_MODEL]
    scale2 = mod[:, 4 * D_MODEL:5 * D_MODEL]
    gate2 = mod[:, 5 * D_MODEL:6 * D_MODEL]

    r_i = lax.broadcasted_iota(jnp.int32, (CHUNK, CHUNK), 0)
    c_i = lax.broadcasted_iota(jnp.int32, (CHUNK, CHUNK), 1)
    bsT = bsT_ref[...]
    gd = A_WIDTH // A_GROUPS
    for g in range(A_GROUPS):
        wm = jnp.where(r_i >= c_i, ws_ref[g], 0.0).astype(BF16)
        bias = bsT[:, g:g + 1]
        for c in range(tm // CHUNK):
            rows = slice(c * CHUNK, (c + 1) * CHUNK)
            cols = slice(g * gd, (g + 1) * gd)
            mixed = jnp.dot(wm, vln_ref[rows, cols], preferred_element_type=F32) + bias
            ya_ref[rows, cols] = (u_ref[rows, cols].astype(F32) * mixed).astype(BF16)

    pa = jnp.dot(ya_ref[...], wpa_ref[...], preferred_element_type=F32)
    pb = jnp.dot(yb_ref[...], wpb_ref[...], preferred_element_type=F32)
    merged = sga_ref[...].astype(F32) * pa + sgb_ref[...].astype(F32) * pb
    x1 = x_ref[...] + gate1 * jnp.dot(merged.astype(BF16), wout_ref[...], preferred_element_type=F32)

    h2 = (_rms_norm(x1, n2g_ref[...]) * (1.0 + scale2) + shift2).astype(BF16)
    fg = jnp.dot(h2, wfi_ref[:, 0:D_FF], preferred_element_type=F32)
    fu = jnp.dot(h2, wfi_ref[:, D_FF:], preferred_element_type=F32)
    act = (fg * _sigmoid(fg) * fu).astype(BF16)
    x2 = x1 + gate2 * jnp.dot(act, wfo_ref[...], preferred_element_type=F32)
    o_ref[...] = _rms_norm(x2, fg_ref[...])


def _post_call(x2, mod3, u, vln, sga, sgb, yb, ws, bsT, wpa, wpb, wout, n2g, wfi, wfo, fg):
    tm = TM_OUT
    per_b = SEQ // tm
    row = lambda t: (t, 0)
    const2 = lambda t: (0, 0)
    tok = lambda: pl.BlockSpec((tm, D_MODEL), row)
    in_specs = [
        tok(),
        pl.BlockSpec((1, 1, 6 * D_MODEL), lambda t: (t // per_b, 0, 0)),
        tok(), tok(), tok(), tok(), tok(),
        pl.BlockSpec(ws.shape, lambda t: (0, 0, 0)),
        pl.BlockSpec(bsT.shape, const2),
        pl.BlockSpec(wpa.shape, const2),
        pl.BlockSpec(wpb.shape, const2),
        pl.BlockSpec(wout.shape, const2),
        pl.BlockSpec((1, D_MODEL), const2),
        pl.BlockSpec(wfi.shape, const2),
        pl.BlockSpec(wfo.shape, const2),
        pl.BlockSpec((1, D_MODEL), const2),
    ]
    return pl.pallas_call(
        _post_kernel,
        grid=(TOKENS // tm,),
        in_specs=in_specs,
        out_specs=pl.BlockSpec((tm, D_MODEL), row),
        out_shape=jax.ShapeDtypeStruct((TOKENS, D_MODEL), F32),
        scratch_shapes=[pltpu.VMEM((tm, A_WIDTH), BF16)],
        compiler_params=pltpu.CompilerParams(dimension_semantics=("arbitrary",),
                                             vmem_limit_bytes=VMEM_LIMIT),
        name="merge_ffn",
    )(x2, mod3, u, vln, sga, sgb, yb, ws, bsT, wpa, wpb, wout, n2g, wfi, wfo, fg)


def _inv_freq_table():
    fq = ROPE_THETA ** (-np.arange(0, ROT_DIM, 2, dtype=np.float32) / ROT_DIM)
    fi = ROPE_THETA ** (-np.arange(0, IDX_ROT_DIM, 2, dtype=np.float32) / IDX_ROT_DIM)
    return np.concatenate([fq, fi]).astype(np.float32).reshape(-1, 1)


def kernel(x, c, positions, w_ada, b_ada, norm1_g, w_in, gmlp_ln_g, gmlp_ln_b, gmlp_w_s, gmlp_b_s,
           idx_k_ln_g, idx_k_ln_b, w_proj_a, w_proj_b, w_out, norm2_g, w_ffn_in, w_ffn_out, final_norm_g):
    assert x.shape == (BATCH, SEQ, D_MODEL) and w_in.shape == (1, D_MODEL, _IN_COLS)
    x2 = x.reshape(TOKENS, D_MODEL)
    pos3 = positions.reshape(BATCH, 1, SEQ)
    invf = jnp.asarray(_inv_freq_table())
    xcur = x2
    for l in range(w_ada.shape[0]):
        mod = _ada_call(c, w_ada[l], b_ada[l])
        mod3 = mod.reshape(BATCH, 1, 6 * D_MODEL)
        (u, vln, sga, sgb, qT, k, vT, qiT, ki, wiT, wpa, wpb, wout, wfi, wfo) = _inproj_call(
            xcur, mod3, pos3, norm1_g[l].reshape(1, -1), w_in[l].astype(BF16),
            gmlp_ln_g[l].reshape(1, -1), gmlp_ln_b[l].reshape(1, -1),
            idx_k_ln_g[l].reshape(-1, 1), idx_k_ln_b[l].reshape(-1, 1), invf,
            (w_proj_a[l], w_proj_b[l], w_out[l], w_ffn_in[l], w_ffn_out[l]))
        yb = _attn_call(qT, qiT, wiT, k, ki, vT)
        last = l == w_ada.shape[0] - 1
        fg = final_norm_g.reshape(1, -1) if last else None
        assert last, "single-layer block"
        xcur = _post_call(xcur, mod3, u, vln, sga, sgb, yb, gmlp_w_s[l], gmlp_b_s[l].T,
                          wpa, wpb, wout, norm2_g[l].reshape(1, -1), wfi, wfo, fg)
    return xcur.reshape(BATCH, SEQ, D_MODEL)
```

```python
import functools

import numpy as np
import jax
import jax.numpy as jnp
from jax import lax
from jax.experimental import pallas as pl
from jax.experimental.pallas import tpu as pltpu

D_MODEL = 1024
BATCH = 4
SEQ = 4096
CHUNK = 128
A_GROUPS = 8
A_WIDTH = 1024
N_HEADS = 8
N_KV_HEADS = 2
HEAD_DIM = 128
IDX_HEADS = 8
IDX_DIM = 64
TOPK = 256
ROPE_THETA = 500000.0
ROT_DIM = HEAD_DIM // 4
IDX_ROT_DIM = IDX_DIM // 4
D_FF = 2816
EPS = 1e-6
NEG_INF = -1e30

TOKENS = BATCH * SEQ
Q_TILE = 128
KEY_TILE = 512
IDX_K_LANES = 128
ACC_ROWS = 32
FIRST_CHECK_AFTER = 16
CHECK_EVERY = 2
SWEEP_UNROLL = (4, 2, 1)
TM_IN = 512
LATER_WEIGHT_ROWS = (D_MODEL, D_MODEL, D_MODEL, D_MODEL, D_FF)
TM_OUT = 512
HEADS_PER_KV = N_HEADS // N_KV_HEADS
VMEM_LIMIT = 56 * 1024 * 1024

_OFF_U = 0
_OFF_V = _OFF_U + A_WIDTH
_OFF_Q = _OFF_V + A_WIDTH
_OFF_K = _OFF_Q + N_HEADS * HEAD_DIM
_OFF_VV = _OFF_K + N_KV_HEADS * HEAD_DIM
_OFF_QI = _OFF_VV + N_KV_HEADS * HEAD_DIM
_OFF_KI = _OFF_QI + IDX_HEADS * IDX_DIM
_OFF_WI = _OFF_KI + IDX_DIM
_OFF_GA = _OFF_WI + IDX_HEADS
_OFF_GB = _OFF_GA + D_MODEL
_IN_COLS = _OFF_GB + D_MODEL

_T_Q = 0
_T_K = _T_Q + N_HEADS * HEAD_DIM
_T_V = _T_K + N_KV_HEADS * HEAD_DIM
_T_QI = _T_V + N_KV_HEADS * HEAD_DIM
_T_KI = _T_QI + IDX_HEADS * IDX_DIM
_T_WI = _T_KI + IDX_DIM
_T_ROWS = _T_WI + IDX_HEADS

F32 = jnp.float32
BF16 = jnp.bfloat16


def _gelu_tanh(x):
    return 0.5 * x * (1.0 + jnp.tanh(np.sqrt(2.0 / np.pi).astype(np.float32) * (x + 0.044715 * (x * x * x))))


def _sigmoid(x):
    return 1.0 / (1.0 + jnp.exp(-x))


def _rms_norm(x, g):
    return x * lax.rsqrt(jnp.mean(x * x, axis=-1, keepdims=True) + EPS) * g


def _ada_kernel(c_ref, w_ref, b_ref, o_ref):
    c = c_ref[...]
    a = c * _sigmoid(c)
    w = w_ref[...]
    a_hi = a.astype(BF16)
    a_lo = (a - a_hi.astype(F32)).astype(BF16)
    w_hi = w.astype(BF16)
    w_lo = (w - w_hi.astype(F32)).astype(BF16)
    dot = functools.partial(jnp.dot, preferred_element_type=F32)
    o_ref[...] = dot(a_hi, w_hi) + (dot(a_hi, w_lo) + dot(a_lo, w_hi)) + b_ref[...]


def _ada_call(c, w_ada, b_ada):
    n_out = 6 * D_MODEL
    tn = 1024
    return pl.pallas_call(
        _ada_kernel,
        grid=(n_out // tn,),
        in_specs=[pl.BlockSpec((BATCH, D_MODEL), lambda j: (0, 0)),
                  pl.BlockSpec((D_MODEL, tn), lambda j: (0, j)),
                  pl.BlockSpec((1, tn), lambda j: (0, j))],
        out_specs=pl.BlockSpec((BATCH, tn), lambda j: (0, j)),
        out_shape=jax.ShapeDtypeStruct((BATCH, n_out), F32),
        compiler_params=pltpu.CompilerParams(dimension_semantics=("arbitrary",),
                                             vmem_limit_bytes=VMEM_LIMIT),
        name="ada_mod",
    )(c, w_ada, b_ada.reshape(1, n_out))


def _rope_rows(blk, cos, sin, half):
    x1 = blk[0:half]
    x2 = blk[half:2 * half]
    return x1 * cos - x2 * sin, x2 * cos + x1 * sin


def _inproj_kernel(x_ref, mod_ref, pos_ref, n1g_ref, wm_ref, wgt_ref, lng_ref, lnb_ref,
                   kig_ref, kib_ref, invf_ref, *rest):
    n_later = len(LATER_WEIGHT_ROWS)
    later_f32 = rest[:n_later]
    (u_ref, vln_ref, sga_ref, sgb_ref, qT_ref, k_ref, vT_ref, qiT_ref, ki_ref, wiT_ref) = rest[n_later:n_later + 10]
    later_bf16 = rest[n_later + 10:2 * n_later + 10]
    (wn_ref,) = rest[2 * n_later + 10:]
    tm = x_ref.shape[0]

    for src, dst in zip(later_f32, later_bf16):
        dst[...] = src[...].astype(BF16)

    @pl.when(pl.program_id(0) == 0)
    def _():
        for c in range(2 * A_WIDTH // 128):
            wn_ref[:, c * 128:(c + 1) * 128] = wm_ref[c * 128:(c + 1) * 128, :].astype(F32).T.astype(BF16)
        for c in range(2 * D_MODEL // 128):
            wn_ref[:, 2 * A_WIDTH + c * 128:2 * A_WIDTH + (c + 1) * 128] = (
                wgt_ref[c * 128:(c + 1) * 128, :].astype(F32).T.astype(BF16))

    x = x_ref[...]
    mod = mod_ref[0]
    shift1 = mod[:, 0:D_MODEL]
    scale1 = mod[:, D_MODEL:2 * D_MODEL]
    h = _rms_norm(x, n1g_ref[...]) * (1.0 + scale1) + shift1
    hb = h.astype(BF16)

    zu = jnp.dot(hb, wn_ref[:, 0:A_WIDTH], preferred_element_type=F32)
    u_ref[...] = _gelu_tanh(zu).astype(BF16)
    zv = _gelu_tanh(jnp.dot(hb, wn_ref[:, A_WIDTH:2 * A_WIDTH], preferred_element_type=F32))
    mu = jnp.mean(zv, axis=-1, keepdims=True)
    zc = zv - mu
    var = jnp.mean(zc * zc, axis=-1, keepdims=True)
    vln_ref[...] = (zc * lax.rsqrt(var + EPS) * lng_ref[...] + lnb_ref[...]).astype(BF16)
    zga = jnp.dot(hb, wn_ref[:, 2 * A_WIDTH:2 * A_WIDTH + D_MODEL], preferred_element_type=F32)
    sga_ref[...] = _sigmoid(zga).astype(BF16)
    zgb = jnp.dot(hb, wn_ref[:, 2 * A_WIDTH + D_MODEL:], preferred_element_type=F32)
    sgb_ref[...] = _sigmoid(zgb).astype(BF16)

    nt = (((1,), (1,)), ((), ()))
    pos = pos_ref[0].astype(F32)
    ang = invf_ref[...] * pos
    cos = jnp.cos(ang)
    sin = jnp.sin(ang)
    hq = ROT_DIM // 2
    hi = IDX_ROT_DIM // 2
    cos_q, sin_q = cos[0:hq], sin[0:hq]
    cos_i, sin_i = cos[hq:hq + hi], sin[hq:hq + hi]
    n_sub = tm // Q_TILE

    zt = lax.dot_general(wm_ref[_OFF_Q:_OFF_GA, :], hb, nt, preferred_element_type=F32)
    zq = zt[_T_Q:_T_K]
    q_scale = HEAD_DIM ** -0.5 * float(np.log2(np.e))
    for hd in range(N_HEADS):
        blk = zq[hd * HEAD_DIM:(hd + 1) * HEAD_DIM]
        r1, r2 = _rope_rows(blk, cos_q, sin_q, hq)
        full = (jnp.concatenate([r1, r2, blk[ROT_DIM:]], axis=0) * q_scale).astype(BF16)
        for s in range(n_sub):
            qT_ref[s, :, hd * Q_TILE:(hd + 1) * Q_TILE] = full[:, s * Q_TILE:(s + 1) * Q_TILE]

    zk = zt[_T_K:_T_V]
    k_rows = []
    for g in range(N_KV_HEADS):
        blk = zk[g * HEAD_DIM:(g + 1) * HEAD_DIM]
        r1, r2 = _rope_rows(blk, cos_q, sin_q, hq)
        k_rows += [r1, r2, blk[ROT_DIM:]]
    k_ref[...] = jnp.concatenate(k_rows, axis=0).T.astype(BF16)

    vT_ref[...] = zt[_T_V:_T_QI].astype(BF16)

    zqi = zt[_T_QI:_T_KI]
    for hd in range(IDX_HEADS):
        blk = zqi[hd * IDX_DIM:(hd + 1) * IDX_DIM]
        r1, r2 = _rope_rows(blk, cos_i, sin_i, hi)
        full = jnp.concatenate([r1, r2, blk[IDX_ROT_DIM:]], axis=0).astype(BF16)
        for s in range(n_sub):
            qiT_ref[s, :, hd * Q_TILE:(hd + 1) * Q_TILE] = full[:, s * Q_TILE:(s + 1) * Q_TILE]

    zrest = zt[_T_KI:_T_ROWS]
    zki = zrest[0:IDX_DIM]
    kmu = jnp.mean(zki, axis=0, keepdims=True)
    kc = zki - kmu
    kvar = jnp.mean(kc * kc, axis=0, keepdims=True)
    kin = kc * lax.rsqrt(kvar + EPS) * kig_ref[...] + kib_ref[...]
    r1, r2 = _rope_rows(kin, cos_i, sin_i, hi)
    ki_full = jnp.concatenate([r1, r2, kin[IDX_ROT_DIM:], jnp.zeros((IDX_K_LANES - IDX_DIM, tm), F32)], axis=0)
    ki_ref[...] = ki_full.T.astype(BF16)
    wiT_ref[...] = zrest[IDX_DIM:IDX_DIM + IDX_HEADS] * ((IDX_HEADS ** -0.5) * (IDX_DIM ** -0.5))


def _inproj_call(x2, mod3, pos3, n1g, wm, wgt, lng, lnb, kig, kib, invf, later_weights):
    tm = TM_IN
    n_tiles = TOKENS // tm
    assert tuple(a.shape[0] for a in later_weights) == LATER_WEIGHT_ROWS

    def slab_spec(a):
        rows = a.shape[0]
        slab = next(r for r in range(16, rows + 1, 16) if rows % r == 0 and rows // r <= n_tiles)
        n_slabs = rows // slab
        return pl.BlockSpec((slab, a.shape[1]), lambda t: (t * n_slabs // n_tiles, 0))
    per_b = SEQ // tm
    n_sub = tm // Q_TILE
    const2 = lambda t: (0, 0)
    row = lambda t: (t, 0)
    in_specs = [
        pl.BlockSpec((tm, D_MODEL), row),
        pl.BlockSpec((1, 1, 6 * D_MODEL), lambda t: (t // per_b, 0, 0)),
        pl.BlockSpec((1, 1, tm), lambda t: (t // per_b, 0, t % per_b)),
        pl.BlockSpec((1, D_MODEL), const2),
        pl.BlockSpec(wm.shape, const2),
        pl.BlockSpec(wgt.shape, const2),
        pl.BlockSpec((1, A_WIDTH), const2),
        pl.BlockSpec((1, A_WIDTH), const2),
        pl.BlockSpec((IDX_DIM, 1), const2),
        pl.BlockSpec((IDX_DIM, 1), const2),
        pl.BlockSpec(invf.shape, const2),
    ] + [slab_spec(a) for a in later_weights]
    out_shape = [
        jax.ShapeDtypeStruct((TOKENS, A_WIDTH), BF16),
        jax.ShapeDtypeStruct((TOKENS, A_WIDTH), BF16),
        jax.ShapeDtypeStruct((TOKENS, D_MODEL), BF16),
        jax.ShapeDtypeStruct((TOKENS, D_MODEL), BF16),
        jax.ShapeDtypeStruct((TOKENS // Q_TILE, HEAD_DIM, N_HEADS * Q_TILE), BF16),
        jax.ShapeDtypeStruct((TOKENS, N_KV_HEADS * HEAD_DIM), BF16),
        jax.ShapeDtypeStruct((N_KV_HEADS * HEAD_DIM, TOKENS), BF16),
        jax.ShapeDtypeStruct((TOKENS // Q_TILE, IDX_DIM, IDX_HEADS * Q_TILE), BF16),
        jax.ShapeDtypeStruct((TOKENS, IDX_K_LANES), BF16),
        jax.ShapeDtypeStruct((IDX_HEADS, TOKENS), F32),
    ]
    out_specs = [
        pl.BlockSpec((tm, A_WIDTH), row),
        pl.BlockSpec((tm, A_WIDTH), row),
        pl.BlockSpec((tm, D_MODEL), row),
        pl.BlockSpec((tm, D_MODEL), row),
        pl.BlockSpec((n_sub, HEAD_DIM, N_HEADS * Q_TILE), lambda t: (t, 0, 0)),
        pl.BlockSpec((tm, N_KV_HEADS * HEAD_DIM), row),
        pl.BlockSpec((N_KV_HEADS * HEAD_DIM, tm), lambda t: (0, t)),
        pl.BlockSpec((n_sub, IDX_DIM, IDX_HEADS * Q_TILE), lambda t: (t, 0, 0)),
        pl.BlockSpec((tm, IDX_K_LANES), row),
        pl.BlockSpec((IDX_HEADS, tm), lambda t: (0, t)),
    ] + [slab_spec(a) for a in later_weights]
    out_shape = out_shape + [jax.ShapeDtypeStruct(a.shape, BF16) for a in later_weights]
    return pl.pallas_call(
        _inproj_kernel,
        grid=(n_tiles,),
        in_specs=in_specs,
        out_specs=out_specs,
        out_shape=out_shape,
        scratch_shapes=[pltpu.VMEM((D_MODEL, 2 * A_WIDTH + 2 * D_MODEL), BF16)],
        compiler_params=pltpu.CompilerParams(dimension_semantics=("arbitrary",),
                                             vmem_limit_bytes=VMEM_LIMIT),
        name="in_proj",
    )(x2, mod3, pos3, n1g, wm, wgt, lng, lnb, kig, kib, invf, *later_weights)


def _col_reduce(x, op):
    return op(x.reshape(x.shape[0] // ACC_ROWS, ACC_ROWS, x.shape[1]), axis=0)


def _key_tiles(tile):
    return (tile * Q_TILE + Q_TILE + KEY_TILE - 1) // KEY_TILE


def _attn_kernel(qT_ref, qiT_ref, wiT_ref, k_ref, ki_ref, vT_ref, y_ref,
                 sc_ref, acc_ref, l_ref, vsel_ref):
    t = pl.program_id(1)
    nq = SEQ // Q_TILE
    i = jnp.minimum(t, nq - 1)
    ia = jnp.maximum(t - 1, 0)
    sc_i = sc_ref.at[i % 2]
    sc_a = sc_ref.at[ia % 2]
    nkt = jnp.where(t < nq, _key_tiles(i), 0)
    nka = jnp.where(t >= 1, _key_tiles(ia), 0)
    kf = float(TOPK)
    gw = HEADS_PER_KV * Q_TILE
    hw = N_HEADS * Q_TILE

    @pl.when(t == 0)
    def _():
        vsel_ref[...] = jnp.zeros_like(vsel_ref)

    qiT = qiT_ref[0]
    qT = qT_ref[0]
    w = wiT_ref[...]
    wrow = jnp.concatenate([w[hd:hd + 1, :] for hd in range(IDX_HEADS)], axis=1)
    vsel_a = vsel_ref[ia % 2]

    row_i = lax.broadcasted_iota(jnp.int32, (KEY_TILE, Q_TILE), 0)
    qidx = i * Q_TILE + lax.broadcasted_iota(jnp.int32, (KEY_TILE, Q_TILE), 1)

    def score_tile(j, mx_a, mn_a):
        sl = pl.ds(pl.multiple_of(j * KEY_TILE, KEY_TILE), KEY_TILE)
        lg = jnp.dot(ki_ref[sl, 0:IDX_DIM], qiT, preferred_element_type=F32)
        r = jnp.maximum(lg, 0.0) * wrow
        s = r[:, 0:Q_TILE]
        for hd in range(1, IDX_HEADS):
            s = s + r[:, hd * Q_TILE:(hd + 1) * Q_TILE]
        causal = j * KEY_TILE + row_i <= qidx
        sc_i[sl, :] = jnp.where(causal, s, -jnp.inf)
        mx_a = jnp.maximum(mx_a, _col_reduce(jnp.where(causal, s, -jnp.inf), jnp.max))
        mn_a = jnp.minimum(mn_a, _col_reduce(jnp.where(causal, s, jnp.inf), jnp.min))
        return mx_a, mn_a

    def key_mask(sl):
        return jnp.where(sc_a[sl, :] >= vsel_a, 0.0, NEG_INF)

    def qk_tile(j):
        sl = pl.ds(pl.multiple_of(j * KEY_TILE, KEY_TILE), KEY_TILE)
        return [jnp.dot(k_ref[sl, g * HEAD_DIM:(g + 1) * HEAD_DIM], qT[:, g * gw:(g + 1) * gw],
                        preferred_element_type=F32) for g in range(N_KV_HEADS)]

    def attend_tile(j, qk, l8, m):
        sl = pl.ds(pl.multiple_of(j * KEY_TILE, KEY_TILE), KEY_TILE)
        mask = key_mask(sl)
        parts = []
        for g in range(N_KV_HEADS):
            a = qk[g]
            ps = []
            for r in range(HEADS_PER_KV):
                e = a[:, r * Q_TILE:(r + 1) * Q_TILE] + mask
                if m is not None:
                    c0 = (g * HEADS_PER_KV + r) * Q_TILE
                    e = e - m[:, c0:c0 + Q_TILE]
                ps.append(jnp.exp2(e))
            p = jnp.concatenate(ps, axis=1)
            acc_ref[:, g * gw:(g + 1) * gw] += jnp.dot(
                vT_ref[g * HEAD_DIM:(g + 1) * HEAD_DIM, sl], p.astype(BF16), preferred_element_type=F32)
            parts.append(jnp.sum(p.reshape(KEY_TILE // 8, 8, gw), axis=0))
        return l8 + jnp.concatenate(parts, axis=1)

    acc_ref[...] = jnp.zeros_like(acc_ref)
    n_both = jnp.minimum(nkt, nka)

    def both_body(j, c):
        mx_a, mn_a, l8 = c
        mx_a, mn_a = score_tile(j, mx_a, mn_a)
        return mx_a, mn_a, attend_tile(j, qk_tile(j), l8, None)

    def score_body(j, c):
        mx_a, mn_a, l8 = c
        mx_a, mn_a = score_tile(j, mx_a, mn_a)
        return mx_a, mn_a, l8

    def attend_body(j, c):
        mx_a, mn_a, l8 = c
        return mx_a, mn_a, attend_tile(j, qk_tile(j), l8, None)

    carry = (jnp.full((ACC_ROWS, Q_TILE), -jnp.inf, F32), jnp.full((ACC_ROWS, Q_TILE), jnp.inf, F32),
             jnp.zeros((8, hw), F32))
    done = 0
    for width in SWEEP_UNROLL:
        def wide_body(jw, c, width=width, done=done):
            for u in range(width):
                c = both_body(done + jw * width + u, c)
            return c
        trips = (n_both - done) // width
        carry = lax.fori_loop(0, trips, wide_body, carry)
        done = done + trips * width
    carry = lax.fori_loop(n_both, nkt, score_body, carry)
    mx_a, mn_a, l8 = lax.fori_loop(n_both, nka, attend_body, carry)
    l_ref[...] = jnp.sum(l8, axis=0, keepdims=True)

    def tiles(fn, init):
        def body(j, carry):
            s = sc_i[pl.ds(pl.multiple_of(j * KEY_TILE, KEY_TILE), KEY_TILE), :]
            return fn(s, carry)
        return lax.fori_loop(0, nkt, body, init)

    def count_ge(t):
        def fold(s, c):
            for r in range(KEY_TILE // ACC_ROWS):
                c = jnp.where(s[r * ACC_ROWS:(r + 1) * ACC_ROWS] >= t, c + 1.0, c)
            return c
        c8 = tiles(fold, jnp.zeros((ACC_ROWS, Q_TILE), F32))
        return jnp.sum(c8, axis=0, keepdims=True)

    def bisect(state, n):
        def step(_, c):
            lo, hi, n_lo, n_hi = c
            mid = 0.5 * lo + 0.5 * hi
            n_mid = count_ge(mid)
            ok = n_mid >= kf
            return (jnp.where(ok, mid, lo), jnp.where(ok, hi, mid),
                    jnp.where(ok, n_mid, n_lo), jnp.where(ok, n_hi, n_mid))
        return lax.fori_loop(0, n, step, state)

    @pl.when(t < TOPK // Q_TILE)
    def _():
        vsel_ref[i % 2] = jnp.full((1, Q_TILE), jnp.finfo(jnp.float32).min, F32)

    @pl.when((t >= TOPK // Q_TILE) & (t < nq))
    def _():
        mx = jnp.max(mx_a, axis=0, keepdims=True)
        mn = jnp.min(mn_a, axis=0, keepdims=True)
        n_valid = (qidx[0:1, :] + 1).astype(F32)
        hi0 = mx + jnp.maximum(jnp.abs(mx), 1e-30) * 1e-6
        state = bisect((mn, hi0, n_valid, jnp.zeros((1, Q_TILE), F32)), FIRST_CHECK_AFTER - CHECK_EVERY)

        def check(state):
            lo, hi, n_lo, n_hi = state

            def f(s, c):
                a8, b8 = c
                a8 = jnp.maximum(a8, _col_reduce(jnp.where(s < hi, s, -jnp.inf), jnp.max))
                b8 = jnp.minimum(b8, _col_reduce(jnp.where(s >= lo, s, jnp.inf), jnp.min))
                return a8, b8
            a8, b8 = tiles(f, (jnp.full((ACC_ROWS, Q_TILE), -jnp.inf, F32), jnp.full((ACC_ROWS, Q_TILE), jnp.inf, F32)))
            top = jnp.max(a8, axis=0, keepdims=True)
            bottom = jnp.min(b8, axis=0, keepdims=True)
            single = top == bottom
            pinned = single | (n_lo - n_hi < 2.5)
            take_top = jnp.logical_not(single) & (kf - n_hi < 1.5)
            take_bottom = jnp.logical_not(single | take_top)
            v = jnp.where(take_top, top, bottom)
            n_ge = jnp.where(take_top, n_hi + 1.0, n_lo)
            n_gt = jnp.where(take_bottom, n_hi + 1.0, n_hi)
            return (v, n_ge, n_gt), jnp.min(jnp.where(pinned, 1.0, 0.0))

        def w_cond(c):
            return c[2] < 0.5

        def w_body(c):
            st = bisect(c[0], CHECK_EVERY)
            found, done = check(st)
            return st, found, done

        _, (vk, n_ge, n_gt), _ = lax.while_loop(w_cond, w_body, (state, (mn, n_valid, n_valid), jnp.float32(0.0)))
        need = kf - n_gt
        has_tie = jnp.max(n_ge) > kf + 0.5

        @pl.when(jnp.logical_not(has_tie))
        def _():
            vsel_ref[i % 2] = vk

        @pl.when(has_tie)
        def _():
            vsel_ref[i % 2] = jnp.full((1, Q_TILE), -0.5, F32)
            r_i = lax.broadcasted_iota(jnp.int32, (KEY_TILE, KEY_TILE), 0)
            c_i = lax.broadcasted_iota(jnp.int32, (KEY_TILE, KEY_TILE), 1)
            tri = jnp.where(r_i >= c_i, 1.0, 0.0).astype(BF16)

            def body(j, seen):
                sl = pl.ds(pl.multiple_of(j * KEY_TILE, KEY_TILE), KEY_TILE)
                s = sc_i[sl, :]
                eq = s == vk
                rank = jnp.dot(tri, jnp.where(eq, 1.0, 0.0).astype(BF16), preferred_element_type=F32) + seen
                sel = (s > vk) | (eq & (rank <= need))
                sc_i[sl, :] = jnp.where(sel, 0.0, NEG_INF)
                return rank[KEY_TILE - 1:KEY_TILE, :]
            lax.fori_loop(0, nkt, body, jnp.zeros((1, Q_TILE), F32))

    @pl.when(t >= 1)
    def _():
        l_fast = l_ref[...]
        in_range = (jnp.min(l_fast) > 1e-20) & (jnp.max(l_fast) < 1e30)

        @pl.when(jnp.logical_not(in_range))
        def _():
            def max_body(j, m8):
                sl = pl.ds(pl.multiple_of(j * KEY_TILE, KEY_TILE), KEY_TILE)
                mask = key_mask(sl)
                parts = []
                for g in range(N_KV_HEADS):
                    a = jnp.dot(k_ref[sl, g * HEAD_DIM:(g + 1) * HEAD_DIM], qT[:, g * gw:(g + 1) * gw],
                                preferred_element_type=F32)
                    a = a + jnp.concatenate([mask] * HEADS_PER_KV, axis=1)
                    parts.append(jnp.max(a.reshape(KEY_TILE // 8, 8, gw), axis=0))
                return jnp.maximum(m8, jnp.concatenate(parts, axis=1))
            m8 = lax.fori_loop(0, nka, max_body, jnp.full((8, hw), -jnp.inf, F32))
            m = jnp.max(m8, axis=0, keepdims=True)
            acc_ref[...] = jnp.zeros_like(acc_ref)
            l8 = lax.fori_loop(0, nka, lambda j, c: attend_tile(j, qk_tile(j), c, m), jnp.zeros((8, hw), F32))
            l_ref[...] = jnp.sum(l8, axis=0, keepdims=True)

        oT = acc_ref[...] / l_ref[...]
        for hd in range(N_HEADS):
            y_ref[:, hd * HEAD_DIM:(hd + 1) * HEAD_DIM] = oT[:, hd * Q_TILE:(hd + 1) * Q_TILE].T.astype(BF16)


def _attn_call(qT, qiT, wiT, k, ki, vT):
    nq = SEQ // Q_TILE
    scored = lambda b, t: b * nq + jnp.minimum(t, nq - 1)
    attended = lambda b, t: b * nq + jnp.maximum(t - 1, 0)
    return pl.pallas_call(
        _attn_kernel,
        grid=(BATCH, nq + 1),
        in_specs=[
            pl.BlockSpec((1, HEAD_DIM, N_HEADS * Q_TILE), lambda b, t: (attended(b, t), 0, 0)),
            pl.BlockSpec((1, IDX_DIM, IDX_HEADS * Q_TILE), lambda b, t: (scored(b, t), 0, 0)),
            pl.BlockSpec((IDX_HEADS, Q_TILE), lambda b, t: (0, scored(b, t))),
            pl.BlockSpec((SEQ, N_KV_HEADS * HEAD_DIM), lambda b, t: (b, 0)),
            pl.BlockSpec((SEQ, IDX_K_LANES), lambda b, t: (b, 0)),
            pl.BlockSpec((N_KV_HEADS * HEAD_DIM, SEQ), lambda b, t: (0, b)),
        ],
        out_specs=pl.BlockSpec((Q_TILE, N_HEADS * HEAD_DIM), lambda b, t: (attended(b, t), 0)),
        out_shape=jax.ShapeDtypeStruct((TOKENS, N_HEADS * HEAD_DIM), BF16),
        scratch_shapes=[pltpu.VMEM((2, SEQ, Q_TILE), F32),
                        pltpu.VMEM((HEAD_DIM, N_HEADS * Q_TILE), F32),
                        pltpu.VMEM((1, N_HEADS * Q_TILE), F32),
                        pltpu.VMEM((2, 1, Q_TILE), F32)],
        compiler_params=pltpu.CompilerParams(dimension_semantics=("arbitrary", "arbitrary"),
                                             vmem_limit_bytes=VMEM_LIMIT),
        name="dsa_attention",
    )(qT, qiT, wiT, k, ki, vT)


def _post_kernel(x_ref, mod_ref, u_ref, vln_ref, sga_ref, sgb_ref, yb_ref, ws_ref, bsT_ref,
                 wpa_ref, wpb_ref, wout_ref, n2g_ref, wfi_ref, wfo_ref, fg_ref, o_ref, ya_ref):
    tm = x_ref.shape[0]
    mod = mod_ref[0]
    gate1 = mod[:, 2 * D_MODEL:3 * D_MODEL]
    shift2 = mod[:, 3 * D_MODEL:4 * D_MODEL]
    scale2 = mod[:, 4 * D_MODEL:5 * D_MODEL]
    gate2 = mod[:, 5 * D_MODEL:6 * D_MODEL]

    r_i = lax.broadcasted_iota(jnp.int32, (CHUNK, CHUNK), 0)
    c_i = lax.broadcasted_iota(jnp.int32, (CHUNK, CHUNK), 1)
    bsT = bsT_ref[...]
    gd = A_WIDTH // A_GROUPS
    for g in range(A_GROUPS):
        wm = jnp.where(r_i >= c_i, ws_ref[g], 0.0).astype(BF16)
        bias = bsT[:, g:g + 1]
        for c in range(tm // CHUNK):
            rows = slice(c * CHUNK, (c + 1) * CHUNK)
            cols = slice(g * gd, (g + 1) * gd)
            mixed = jnp.dot(wm, vln_ref[rows, cols], preferred_element_type=F32) + bias
            ya_ref[rows, cols] = (u_ref[rows, cols].astype(F32) * mixed).astype(BF16)

    pa = jnp.dot(ya_ref[...], wpa_ref[...], preferred_element_type=F32)
    pb = jnp.dot(yb_ref[...], wpb_ref[...], preferred_element_type=F32)
    merged = sga_ref[...].astype(F32) * pa + sgb_ref[...].astype(F32) * pb
    x1 = x_ref[...] + gate1 * jnp.dot(merged.astype(BF16), wout_ref[...], preferred_element_type=F32)

    h2 = (_rms_norm(x1, n2g_ref[...]) * (1.0 + scale2) + shift2).astype(BF16)
    fg = jnp.dot(h2, wfi_ref[:, 0:D_FF], preferred_element_type=F32)
    fu = jnp.dot(h2, wfi_ref[:, D_FF:], preferred_element_type=F32)
    act = (fg * _sigmoid(fg) * fu).astype(BF16)
    x2 = x1 + gate2 * jnp.dot(act, wfo_ref[...], preferred_element_type=F32)
    o_ref[...] = _rms_norm(x2, fg_ref[...])


def _post_call(x2, mod3, u, vln, sga, sgb, yb, ws, bsT, wpa, wpb, wout, n2g, wfi, wfo, fg):
    tm = TM_OUT
    per_b = SEQ // tm
    row = lambda t: (t, 0)
    const2 = lambda t: (0, 0)
    tok = lambda: pl.BlockSpec((tm, D_MODEL), row)
    in_specs = [
        tok(),
        pl.BlockSpec((1, 1, 6 * D_MODEL), lambda t: (t // per_b, 0, 0)),
        tok(), tok(), tok(), tok(), tok(),
        pl.BlockSpec(ws.shape, lambda t: (0, 0, 0)),
        pl.BlockSpec(bsT.shape, const2),
        pl.BlockSpec(wpa.shape, const2),
        pl.BlockSpec(wpb.shape, const2),
        pl.BlockSpec(wout.shape, const2),
        pl.BlockSpec((1, D_MODEL), const2),
        pl.BlockSpec(wfi.shape, const2),
        pl.BlockSpec(wfo.shape, const2),
        pl.BlockSpec((1, D_MODEL), const2),
    ]
    return pl.pallas_call(
        _post_kernel,
        grid=(TOKENS // tm,),
        in_specs=in_specs,
        out_specs=pl.BlockSpec((tm, D_MODEL), row),
        out_shape=jax.ShapeDtypeStruct((TOKENS, D_MODEL), F32),
        scratch_shapes=[pltpu.VMEM((tm, A_WIDTH), BF16)],
        compiler_params=pltpu.CompilerParams(dimension_semantics=("arbitrary",),
                                             vmem_limit_bytes=VMEM_LIMIT),
        name="merge_ffn",
    )(x2, mod3, u, vln, sga, sgb, yb, ws, bsT, wpa, wpb, wout, n2g, wfi, wfo, fg)


def _inv_freq_table():
    fq = ROPE_THETA ** (-np.arange(0, ROT_DIM, 2, dtype=np.float32) / ROT_DIM)
    fi = ROPE_THETA ** (-np.arange(0, IDX_ROT_DIM, 2, dtype=np.float32) / IDX_ROT_DIM)
    return np.concatenate([fq, fi]).astype(np.float32).reshape(-1, 1)


def kernel(x, c, positions, w_ada, b_ada, norm1_g, w_in, gmlp_ln_g, gmlp_ln_b, gmlp_w_s, gmlp_b_s,
           idx_k_ln_g, idx_k_ln_b, w_proj_a, w_proj_b, w_out, norm2_g, w_ffn_in, w_ffn_out, final_norm_g):
    assert x.shape == (BATCH, SEQ, D_MODEL) and w_in.shape == (1, D_MODEL, _IN_COLS)
    x2 = x.reshape(TOKENS, D_MODEL)
    pos3 = positions.reshape(BATCH, 1, SEQ)
    invf = jnp.asarray(_inv_freq_table())
    xcur = x2
    for l in range(w_ada.shape[0]):
        mod = _ada_call(c, w_ada[l], b_ada[l])
        mod3 = mod.reshape(BATCH, 1, 6 * D_MODEL)
        w_t = w_in[l].T
        (u, vln, sga, sgb, qT, k, vT, qiT, ki, wiT, wpa, wpb, wout, wfi, wfo) = _inproj_call(
            xcur, mod3, pos3, norm1_g[l].reshape(1, -1),
            w_t[:_OFF_GA].astype(BF16), w_t[_OFF_GA:].astype(BF16),
            gmlp_ln_g[l].reshape(1, -1), gmlp_ln_b[l].reshape(1, -1),
            idx_k_ln_g[l].reshape(-1, 1), idx_k_ln_b[l].reshape(-1, 1), invf,
            (w_proj_a[l], w_proj_b[l], w_out[l], w_ffn_in[l], w_ffn_out[l]))
        yb = _attn_call(qT, qiT, wiT, k, ki, vT)
        last = l == w_ada.shape[0] - 1
        fg = final_norm_g.reshape(1, -1) if last else None
        assert last, "single-layer block"
        xcur = _post_call(xcur, mod3, u, vln, sga, sgb, yb, gmlp_w_s[l], gmlp_b_s[l].T,
                          wpa, wpb, wout, norm2_g[l].reshape(1, -1), wfi, wfo, fg)
    return xcur.reshape(BATCH, SEQ, D_MODEL)
```

```python
import functools

import numpy as np
import jax
import jax.numpy as jnp
from jax import lax
from jax.experimental import pallas as pl
from jax.experimental.pallas import tpu as pltpu

D_MODEL = 1024
BATCH = 4
SEQ = 4096
CHUNK = 128
A_GROUPS = 8
A_WIDTH = 1024
N_HEADS = 8
N_KV_HEADS = 2
HEAD_DIM = 128
IDX_HEADS = 8
IDX_DIM = 64
TOPK = 256
ROPE_THETA = 500000.0
ROT_DIM = HEAD_DIM // 4
IDX_ROT_DIM = IDX_DIM // 4
D_FF = 2816
EPS = 1e-6
NEG_INF = -1e30

TOKENS = BATCH * SEQ
Q_TILE = 128
KEY_TILE = 512
IDX_K_LANES = 128
ACC_ROWS = 32
FIRST_CHECK_AFTER = 16
CHECK_EVERY = 2
SWEEP_UNROLL = (4, 2, 1)
TM_IN = 512
TM_OUT = 512
HEADS_PER_KV = N_HEADS // N_KV_HEADS
VMEM_LIMIT = 56 * 1024 * 1024

_OFF_U = 0
_OFF_V = _OFF_U + A_WIDTH
_OFF_Q = _OFF_V + A_WIDTH
_OFF_K = _OFF_Q + N_HEADS * HEAD_DIM
_OFF_VV = _OFF_K + N_KV_HEADS * HEAD_DIM
_OFF_QI = _OFF_VV + N_KV_HEADS * HEAD_DIM
_OFF_KI = _OFF_QI + IDX_HEADS * IDX_DIM
_OFF_WI = _OFF_KI + IDX_DIM
_OFF_GA = _OFF_WI + IDX_HEADS
_OFF_GB = _OFF_GA + D_MODEL
_IN_COLS = _OFF_GB + D_MODEL

_T_Q = 0
_T_K = _T_Q + N_HEADS * HEAD_DIM
_T_V = _T_K + N_KV_HEADS * HEAD_DIM
_T_QI = _T_V + N_KV_HEADS * HEAD_DIM
_T_KI = _T_QI + IDX_HEADS * IDX_DIM
_T_WI = _T_KI + IDX_DIM
_T_ROWS = _T_WI + IDX_HEADS
_T_PAD = -(-_T_ROWS // 128) * 128

F32 = jnp.float32
BF16 = jnp.bfloat16


def _gelu_tanh(x):
    return 0.5 * x * (1.0 + jnp.tanh(np.sqrt(2.0 / np.pi).astype(np.float32) * (x + 0.044715 * (x * x * x))))


def _sigmoid(x):
    return 1.0 / (1.0 + jnp.exp(-x))


def _rms_norm(x, g):
    return x * lax.rsqrt(jnp.mean(x * x, axis=-1, keepdims=True) + EPS) * g


def _ada_kernel(c_ref, w_ref, b_ref, o_ref):
    c = c_ref[...]
    a = c * _sigmoid(c)
    w = w_ref[...]
    a_hi = a.astype(BF16)
    a_lo = (a - a_hi.astype(F32)).astype(BF16)
    w_hi = w.astype(BF16)
    w_lo = (w - w_hi.astype(F32)).astype(BF16)
    dot = functools.partial(jnp.dot, preferred_element_type=F32)
    o_ref[...] = dot(a_hi, w_hi) + (dot(a_hi, w_lo) + dot(a_lo, w_hi)) + b_ref[...]


def _ada_call(c, w_ada, b_ada):
    n_out = 6 * D_MODEL
    tn = 1024
    return pl.pallas_call(
        _ada_kernel,
        grid=(n_out // tn,),
        in_specs=[pl.BlockSpec((BATCH, D_MODEL), lambda j: (0, 0)),
                  pl.BlockSpec((D_MODEL, tn), lambda j: (0, j)),
                  pl.BlockSpec((1, tn), lambda j: (0, j))],
        out_specs=pl.BlockSpec((BATCH, tn), lambda j: (0, j)),
        out_shape=jax.ShapeDtypeStruct((BATCH, n_out), F32),
        compiler_params=pltpu.CompilerParams(dimension_semantics=("arbitrary",),
                                             vmem_limit_bytes=VMEM_LIMIT),
        name="ada_mod",
    )(c, w_ada, b_ada.reshape(1, n_out))


def _rope_rows(blk, cos, sin, half):
    x1 = blk[0:half]
    x2 = blk[half:2 * half]
    return x1 * cos - x2 * sin, x2 * cos + x1 * sin


def _inproj_kernel(x_ref, mod_ref, pos_ref, n1g_ref, w_ref, lng_ref, lnb_ref,
                   kig_ref, kib_ref, invf_ref, *rest):
    n_later = (len(rest) - 12) // 2
    later_f32 = rest[:n_later]
    (u_ref, vln_ref, sga_ref, sgb_ref, qT_ref, k_ref, vT_ref, qiT_ref, ki_ref, wiT_ref) = rest[n_later:n_later + 10]
    later_bf16 = rest[n_later + 10:2 * n_later + 10]
    wg_ref, wt_ref = rest[2 * n_later + 10:]
    tm = x_ref.shape[0]

    for src, dst in zip(later_f32, later_bf16):
        dst[...] = src[...].astype(BF16)

    @pl.when(pl.program_id(0) == 0)
    def _():
        wg_ref[:, 0:D_MODEL] = w_ref[:, _OFF_GA:_OFF_GB]
        wg_ref[:, D_MODEL:] = w_ref[:, _OFF_GB:_IN_COLS]
        for c in range(_T_PAD // 128):
            cols = slice(_OFF_Q + c * 128, _OFF_Q + (c + 1) * 128)
            wt_ref[c * 128:(c + 1) * 128, :] = w_ref[:, cols].astype(F32).T.astype(BF16)

    x = x_ref[...]
    mod = mod_ref[0]
    shift1 = mod[:, 0:D_MODEL]
    scale1 = mod[:, D_MODEL:2 * D_MODEL]
    h = _rms_norm(x, n1g_ref[...]) * (1.0 + scale1) + shift1
    hb = h.astype(BF16)

    zu = jnp.dot(hb, w_ref[:, _OFF_U:_OFF_V], preferred_element_type=F32)
    u_ref[...] = _gelu_tanh(zu).astype(BF16)
    zv = _gelu_tanh(jnp.dot(hb, w_ref[:, _OFF_V:_OFF_Q], preferred_element_type=F32))
    mu = jnp.mean(zv, axis=-1, keepdims=True)
    zc = zv - mu
    var = jnp.mean(zc * zc, axis=-1, keepdims=True)
    vln_ref[...] = (zc * lax.rsqrt(var + EPS) * lng_ref[...] + lnb_ref[...]).astype(BF16)
    zga = jnp.dot(hb, wg_ref[:, 0:D_MODEL], preferred_element_type=F32)
    sga_ref[...] = _sigmoid(zga).astype(BF16)
    zgb = jnp.dot(hb, wg_ref[:, D_MODEL:], preferred_element_type=F32)
    sgb_ref[...] = _sigmoid(zgb).astype(BF16)

    nt = (((1,), (1,)), ((), ()))
    pos = pos_ref[0].astype(F32)
    ang = invf_ref[...] * pos
    cos = jnp.cos(ang)
    sin = jnp.sin(ang)
    hq = ROT_DIM // 2
    hi = IDX_ROT_DIM // 2
    cos_q, sin_q = cos[0:hq], sin[0:hq]
    cos_i, sin_i = cos[hq:hq + hi], sin[hq:hq + hi]
    n_sub = tm // Q_TILE

    zt = lax.dot_general(wt_ref[0:_T_ROWS, :], hb, nt, preferred_element_type=F32)
    zq = zt[_T_Q:_T_K]
    q_scale = HEAD_DIM ** -0.5 * float(np.log2(np.e))
    for hd in range(N_HEADS):
        blk = zq[hd * HEAD_DIM:(hd + 1) * HEAD_DIM]
        r1, r2 = _rope_rows(blk, cos_q, sin_q, hq)
        full = (jnp.concatenate([r1, r2, blk[ROT_DIM:]], axis=0) * q_scale).astype(BF16)
        for s in range(n_sub):
            qT_ref[s, :, hd * Q_TILE:(hd + 1) * Q_TILE] = full[:, s * Q_TILE:(s + 1) * Q_TILE]

    zk = zt[_T_K:_T_V]
    k_rows = []
    for g in range(N_KV_HEADS):
        blk = zk[g * HEAD_DIM:(g + 1) * HEAD_DIM]
        r1, r2 = _rope_rows(blk, cos_q, sin_q, hq)
        k_rows += [r1, r2, blk[ROT_DIM:]]
    k_ref[...] = jnp.concatenate(k_rows, axis=0).T.astype(BF16)

    vT_ref[...] = zt[_T_V:_T_QI].astype(BF16)

    zqi = zt[_T_QI:_T_KI]
    for hd in range(IDX_HEADS):
        blk = zqi[hd * IDX_DIM:(hd + 1) * IDX_DIM]
        r1, r2 = _rope_rows(blk, cos_i, sin_i, hi)
        full = jnp.concatenate([r1, r2, blk[IDX_ROT_DIM:]], axis=0).astype(BF16)
        for s in range(n_sub):
            qiT_ref[s, :, hd * Q_TILE:(hd + 1) * Q_TILE] = full[:, s * Q_TILE:(s + 1) * Q_TILE]

    zrest = zt[_T_KI:_T_ROWS]
    zki = zrest[0:IDX_DIM]
    kmu = jnp.mean(zki, axis=0, keepdims=True)
    kc = zki - kmu
    kvar = jnp.mean(kc * kc, axis=0, keepdims=True)
    kin = kc * lax.rsqrt(kvar + EPS) * kig_ref[...] + kib_ref[...]
    r1, r2 = _rope_rows(kin, cos_i, sin_i, hi)
    ki_full = jnp.concatenate([r1, r2, kin[IDX_ROT_DIM:], jnp.zeros((IDX_K_LANES - IDX_DIM, tm), F32)], axis=0)
    ki_ref[...] = ki_full.T.astype(BF16)
    wiT_ref[...] = zrest[IDX_DIM:IDX_DIM + IDX_HEADS] * ((IDX_HEADS ** -0.5) * (IDX_DIM ** -0.5))


def _inproj_call(x2, mod3, pos3, n1g, w, lng, lnb, kig, kib, invf, later_weights):
    tm = TM_IN
    n_tiles = TOKENS // tm

    def slab_spec(a):
        rows = a.shape[0]
        slab = next(r for r in range(16, rows + 1, 16) if rows % r == 0 and rows // r <= n_tiles)
        n_slabs = rows // slab
        return pl.BlockSpec((slab, a.shape[1]), lambda t: (t * n_slabs // n_tiles, 0))
    per_b = SEQ // tm
    n_sub = tm // Q_TILE
    const2 = lambda t: (0, 0)
    row = lambda t: (t, 0)
    in_specs = [
        pl.BlockSpec((tm, D_MODEL), row),
        pl.BlockSpec((1, 1, 6 * D_MODEL), lambda t: (t // per_b, 0, 0)),
        pl.BlockSpec((1, 1, tm), lambda t: (t // per_b, 0, t % per_b)),
        pl.BlockSpec((1, D_MODEL), const2),
        pl.BlockSpec(w.shape, const2),
        pl.BlockSpec((1, A_WIDTH), const2),
        pl.BlockSpec((1, A_WIDTH), const2),
        pl.BlockSpec((IDX_DIM, 1), const2),
        pl.BlockSpec((IDX_DIM, 1), const2),
        pl.BlockSpec(invf.shape, const2),
    ] + [slab_spec(a) for a in later_weights]
    out_shape = [
        jax.ShapeDtypeStruct((TOKENS, A_WIDTH), BF16),
        jax.ShapeDtypeStruct((TOKENS, A_WIDTH), BF16),
        jax.ShapeDtypeStruct((TOKENS, D_MODEL), BF16),
        jax.ShapeDtypeStruct((TOKENS, D_MODEL), BF16),
        jax.ShapeDtypeStruct((TOKENS // Q_TILE, HEAD_DIM, N_HEADS * Q_TILE), BF16),
        jax.ShapeDtypeStruct((TOKENS, N_KV_HEADS * HEAD_DIM), BF16),
        jax.ShapeDtypeStruct((N_KV_HEADS * HEAD_DIM, TOKENS), BF16),
        jax.ShapeDtypeStruct((TOKENS // Q_TILE, IDX_DIM, IDX_HEADS * Q_TILE), BF16),
        jax.ShapeDtypeStruct((TOKENS, IDX_K_LANES), BF16),
        jax.ShapeDtypeStruct((IDX_HEADS, TOKENS), F32),
    ]
    out_specs = [
        pl.BlockSpec((tm, A_WIDTH), row),
        pl.BlockSpec((tm, A_WIDTH), row),
        pl.BlockSpec((tm, D_MODEL), row),
        pl.BlockSpec((tm, D_MODEL), row),
        pl.BlockSpec((n_sub, HEAD_DIM, N_HEADS * Q_TILE), lambda t: (t, 0, 0)),
        pl.BlockSpec((tm, N_KV_HEADS * HEAD_DIM), row),
        pl.BlockSpec((N_KV_HEADS * HEAD_DIM, tm), lambda t: (0, t)),
        pl.BlockSpec((n_sub, IDX_DIM, IDX_HEADS * Q_TILE), lambda t: (t, 0, 0)),
        pl.BlockSpec((tm, IDX_K_LANES), row),
        pl.BlockSpec((IDX_HEADS, tm), lambda t: (0, t)),
    ] + [slab_spec(a) for a in later_weights]
    out_shape = out_shape + [jax.ShapeDtypeStruct(a.shape, BF16) for a in later_weights]
    return pl.pallas_call(
        _inproj_kernel,
        grid=(n_tiles,),
        in_specs=in_specs,
        out_specs=out_specs,
        out_shape=out_shape,
        scratch_shapes=[pltpu.VMEM((D_MODEL, 2 * D_MODEL), BF16),
                        pltpu.VMEM((_T_PAD, D_MODEL), BF16)],
        compiler_params=pltpu.CompilerParams(dimension_semantics=("arbitrary",),
                                             vmem_limit_bytes=VMEM_LIMIT),
        name="in_proj",
    )(x2, mod3, pos3, n1g, w, lng, lnb, kig, kib, invf, *later_weights)


def _col_reduce(x, op):
    return op(x.reshape(x.shape[0] // ACC_ROWS, ACC_ROWS, x.shape[1]), axis=0)


def _key_tiles(tile):
    return (tile * Q_TILE + Q_TILE + KEY_TILE - 1) // KEY_TILE


def _attn_kernel(qT_ref, qiT_ref, wiT_ref, k_ref, ki_ref, vT_ref, *rest):
    n_later = (len(rest) - 5) // 2
    later_f32 = rest[:n_later]
    y_ref = rest[n_later]
    later_bf16 = rest[n_later + 1:2 * n_later + 1]
    sc_ref, acc_ref, l_ref, vsel_ref = rest[2 * n_later + 1:]
    for src, dst in zip(later_f32, later_bf16):
        dst[...] = src[...].astype(BF16)
    t = pl.program_id(1)
    nq = SEQ // Q_TILE
    i = jnp.minimum(t, nq - 1)
    ia = jnp.maximum(t - 1, 0)
    sc_i = sc_ref.at[i % 2]
    sc_a = sc_ref.at[ia % 2]
    nkt = jnp.where(t < nq, _key_tiles(i), 0)
    nka = jnp.where(t >= 1, _key_tiles(ia), 0)
    kf = float(TOPK)
    gw = HEADS_PER_KV * Q_TILE
    hw = N_HEADS * Q_TILE

    @pl.when(t == 0)
    def _():
        vsel_ref[...] = jnp.zeros_like(vsel_ref)

    qiT = qiT_ref[0]
    qT = qT_ref[0]
    w = wiT_ref[...]
    wrow = jnp.concatenate([w[hd:hd + 1, :] for hd in range(IDX_HEADS)], axis=1)
    vsel_a = vsel_ref[ia % 2]

    row_i = lax.broadcasted_iota(jnp.int32, (KEY_TILE, Q_TILE), 0)
    qidx = i * Q_TILE + lax.broadcasted_iota(jnp.int32, (KEY_TILE, Q_TILE), 1)

    def score_tile(j, mx_a, mn_a):
        sl = pl.ds(pl.multiple_of(j * KEY_TILE, KEY_TILE), KEY_TILE)
        lg = jnp.dot(ki_ref[sl, 0:IDX_DIM], qiT, preferred_element_type=F32)
        r = jnp.maximum(lg, 0.0) * wrow
        s = r[:, 0:Q_TILE]
        for hd in range(1, IDX_HEADS):
            s = s + r[:, hd * Q_TILE:(hd + 1) * Q_TILE]
        causal = j * KEY_TILE + row_i <= qidx
        sc_i[sl, :] = jnp.where(causal, s, -jnp.inf)
        mx_a = jnp.maximum(mx_a, _col_reduce(jnp.where(causal, s, -jnp.inf), jnp.max))
        mn_a = jnp.minimum(mn_a, _col_reduce(jnp.where(causal, s, jnp.inf), jnp.min))
        return mx_a, mn_a

    def key_mask(sl):
        return jnp.where(sc_a[sl, :] >= vsel_a, 0.0, NEG_INF)

    def qk_tile(j):
        sl = pl.ds(pl.multiple_of(j * KEY_TILE, KEY_TILE), KEY_TILE)
        return [jnp.dot(k_ref[sl, g * HEAD_DIM:(g + 1) * HEAD_DIM], qT[:, g * gw:(g + 1) * gw],
                        preferred_element_type=F32) for g in range(N_KV_HEADS)]

    def attend_tile(j, qk, l8, m):
        sl = pl.ds(pl.multiple_of(j * KEY_TILE, KEY_TILE), KEY_TILE)
        mask = key_mask(sl)
        parts = []
        for g in range(N_KV_HEADS):
            a = qk[g]
            ps = []
            for r in range(HEADS_PER_KV):
                e = a[:, r * Q_TILE:(r + 1) * Q_TILE] + mask
                if m is not None:
                    c0 = (g * HEADS_PER_KV + r) * Q_TILE
                    e = e - m[:, c0:c0 + Q_TILE]
                ps.append(jnp.exp2(e))
            p = jnp.concatenate(ps, axis=1)
            acc_ref[:, g * gw:(g + 1) * gw] += jnp.dot(
                vT_ref[g * HEAD_DIM:(g + 1) * HEAD_DIM, sl], p.astype(BF16), preferred_element_type=F32)
            parts.append(jnp.sum(p.reshape(KEY_TILE // 8, 8, gw), axis=0))
        return l8 + jnp.concatenate(parts, axis=1)

    acc_ref[...] = jnp.zeros_like(acc_ref)
    n_both = jnp.minimum(nkt, nka)

    def both_body(j, c):
        mx_a, mn_a, l8 = c
        mx_a, mn_a = score_tile(j, mx_a, mn_a)
        return mx_a, mn_a, attend_tile(j, qk_tile(j), l8, None)

    def score_body(j, c):
        mx_a, mn_a, l8 = c
        mx_a, mn_a = score_tile(j, mx_a, mn_a)
        return mx_a, mn_a, l8

    def attend_body(j, c):
        mx_a, mn_a, l8 = c
        return mx_a, mn_a, attend_tile(j, qk_tile(j), l8, None)

    carry = (jnp.full((ACC_ROWS, Q_TILE), -jnp.inf, F32), jnp.full((ACC_ROWS, Q_TILE), jnp.inf, F32),
             jnp.zeros((8, hw), F32))
    done = 0
    for width in SWEEP_UNROLL:
        def wide_body(jw, c, width=width, done=done):
            for u in range(width):
                c = both_body(done + jw * width + u, c)
            return c
        trips = (n_both - done) // width
        carry = lax.fori_loop(0, trips, wide_body, carry)
        done = done + trips * width
    carry = lax.fori_loop(n_both, nkt, score_body, carry)
    mx_a, mn_a, l8 = lax.fori_loop(n_both, nka, attend_body, carry)
    l_ref[...] = jnp.sum(l8, axis=0, keepdims=True)

    def tiles(fn, init):
        def body(j, carry):
            s = sc_i[pl.ds(pl.multiple_of(j * KEY_TILE, KEY_TILE), KEY_TILE), :]
            return fn(s, carry)
        return lax.fori_loop(0, nkt, body, init)

    def count_ge(t):
        def fold(s, c):
            for r in range(KEY_TILE // ACC_ROWS):
                c = jnp.where(s[r * ACC_ROWS:(r + 1) * ACC_ROWS] >= t, c + 1.0, c)
            return c
        c8 = tiles(fold, jnp.zeros((ACC_ROWS, Q_TILE), F32))
        return jnp.sum(c8, axis=0, keepdims=True)

    def bisect(state, n):
        def step(_, c):
            lo, hi, n_lo, n_hi = c
            mid = 0.5 * lo + 0.5 * hi
            n_mid = count_ge(mid)
            ok = n_mid >= kf
            return (jnp.where(ok, mid, lo), jnp.where(ok, hi, mid),
                    jnp.where(ok, n_mid, n_lo), jnp.where(ok, n_hi, n_mid))
        return lax.fori_loop(0, n, step, state)

    @pl.when(t < TOPK // Q_TILE)
    def _():
        vsel_ref[i % 2] = jnp.full((1, Q_TILE), jnp.finfo(jnp.float32).min, F32)

    @pl.when((t >= TOPK // Q_TILE) & (t < nq))
    def _():
        mx = jnp.max(mx_a, axis=0, keepdims=True)
        mn = jnp.min(mn_a, axis=0, keepdims=True)
        n_valid = (qidx[0:1, :] + 1).astype(F32)
        hi0 = mx + jnp.maximum(jnp.abs(mx), 1e-30) * 1e-6
        state = bisect((mn, hi0, n_valid, jnp.zeros((1, Q_TILE), F32)), FIRST_CHECK_AFTER - CHECK_EVERY)

        def check(state):
            lo, hi, n_lo, n_hi = state

            def f(s, c):
                a8, b8 = c
                a8 = jnp.maximum(a8, _col_reduce(jnp.where(s < hi, s, -jnp.inf), jnp.max))
                b8 = jnp.minimum(b8, _col_reduce(jnp.where(s >= lo, s, jnp.inf), jnp.min))
                return a8, b8
            a8, b8 = tiles(f, (jnp.full((ACC_ROWS, Q_TILE), -jnp.inf, F32), jnp.full((ACC_ROWS, Q_TILE), jnp.inf, F32)))
            top = jnp.max(a8, axis=0, keepdims=True)
            bottom = jnp.min(b8, axis=0, keepdims=True)
            single = top == bottom
            pinned = single | (n_lo - n_hi < 2.5)
            take_top = jnp.logical_not(single) & (kf - n_hi < 1.5)
            take_bottom = jnp.logical_not(single | take_top)
            v = jnp.where(take_top, top, bottom)
            n_ge = jnp.where(take_top, n_hi + 1.0, n_lo)
            n_gt = jnp.where(take_bottom, n_hi + 1.0, n_hi)
            return (v, n_ge, n_gt), jnp.min(jnp.where(pinned, 1.0, 0.0))

        def w_cond(c):
            return c[2] < 0.5

        def w_body(c):
            st = bisect(c[0], CHECK_EVERY)
            found, done = check(st)
            return st, found, done

        _, (vk, n_ge, n_gt), _ = lax.while_loop(w_cond, w_body, (state, (mn, n_valid, n_valid), jnp.float32(0.0)))
        need = kf - n_gt
        has_tie = jnp.max(n_ge) > kf + 0.5

        @pl.when(jnp.logical_not(has_tie))
        def _():
            vsel_ref[i % 2] = vk

        @pl.when(has_tie)
        def _():
            vsel_ref[i % 2] = jnp.full((1, Q_TILE), -0.5, F32)
            r_i = lax.broadcasted_iota(jnp.int32, (KEY_TILE, KEY_TILE), 0)
            c_i = lax.broadcasted_iota(jnp.int32, (KEY_TILE, KEY_TILE), 1)
            tri = jnp.where(r_i >= c_i, 1.0, 0.0).astype(BF16)

            def body(j, seen):
                sl = pl.ds(pl.multiple_of(j * KEY_TILE, KEY_TILE), KEY_TILE)
                s = sc_i[sl, :]
                eq = s == vk
                rank = jnp.dot(tri, jnp.where(eq, 1.0, 0.0).astype(BF16), preferred_element_type=F32) + seen
                sel = (s > vk) | (eq & (rank <= need))
                sc_i[sl, :] = jnp.where(sel, 0.0, NEG_INF)
                return rank[KEY_TILE - 1:KEY_TILE, :]
            lax.fori_loop(0, nkt, body, jnp.zeros((1, Q_TILE), F32))

    @pl.when(t >= 1)
    def _():
        l_fast = l_ref[...]
        in_range = (jnp.min(l_fast) > 1e-20) & (jnp.max(l_fast) < 1e30)

        @pl.when(jnp.logical_not(in_range))
        def _():
            def max_body(j, m8):
                sl = pl.ds(pl.multiple_of(j * KEY_TILE, KEY_TILE), KEY_TILE)
                mask = key_mask(sl)
                parts = []
                for g in range(N_KV_HEADS):
                    a = jnp.dot(k_ref[sl, g * HEAD_DIM:(g + 1) * HEAD_DIM], qT[:, g * gw:(g + 1) * gw],
                                preferred_element_type=F32)
                    a = a + jnp.concatenate([mask] * HEADS_PER_KV, axis=1)
                    parts.append(jnp.max(a.reshape(KEY_TILE // 8, 8, gw), axis=0))
                return jnp.maximum(m8, jnp.concatenate(parts, axis=1))
            m8 = lax.fori_loop(0, nka, max_body, jnp.full((8, hw), -jnp.inf, F32))
            m = jnp.max(m8, axis=0, keepdims=True)
            acc_ref[...] = jnp.zeros_like(acc_ref)
            l8 = lax.fori_loop(0, nka, lambda j, c: attend_tile(j, qk_tile(j), c, m), jnp.zeros((8, hw), F32))
            l_ref[...] = jnp.sum(l8, axis=0, keepdims=True)

        oT = acc_ref[...] / l_ref[...]
        for hd in range(N_HEADS):
            y_ref[:, hd * HEAD_DIM:(hd + 1) * HEAD_DIM] = oT[:, hd * Q_TILE:(hd + 1) * Q_TILE].T.astype(BF16)


def _attn_call(qT, qiT, wiT, k, ki, vT, later_weights):
    nq = SEQ // Q_TILE
    scored = lambda b, t: b * nq + jnp.minimum(t, nq - 1)
    attended = lambda b, t: b * nq + jnp.maximum(t - 1, 0)
    n_steps = BATCH * (nq + 1)

    def slab_spec(a):
        rows = a.shape[0]
        slab = next(r for r in range(16, rows + 1, 16) if rows % r == 0 and rows // r <= n_steps)
        n_slabs = rows // slab
        return pl.BlockSpec((slab, a.shape[1]), lambda b, t: ((b * (nq + 1) + t) * n_slabs // n_steps, 0))
    slabs = [slab_spec(a) for a in later_weights]
    return pl.pallas_call(
        _attn_kernel,
        grid=(BATCH, nq + 1),
        in_specs=[
            pl.BlockSpec((1, HEAD_DIM, N_HEADS * Q_TILE), lambda b, t: (attended(b, t), 0, 0)),
            pl.BlockSpec((1, IDX_DIM, IDX_HEADS * Q_TILE), lambda b, t: (scored(b, t), 0, 0)),
            pl.BlockSpec((IDX_HEADS, Q_TILE), lambda b, t: (0, scored(b, t))),
            pl.BlockSpec((SEQ, N_KV_HEADS * HEAD_DIM), lambda b, t: (b, 0)),
            pl.BlockSpec((SEQ, IDX_K_LANES), lambda b, t: (b, 0)),
            pl.BlockSpec((N_KV_HEADS * HEAD_DIM, SEQ), lambda b, t: (0, b)),
        ] + slabs,
        out_specs=[pl.BlockSpec((Q_TILE, N_HEADS * HEAD_DIM), lambda b, t: (attended(b, t), 0))] + slabs,
        out_shape=[jax.ShapeDtypeStruct((TOKENS, N_HEADS * HEAD_DIM), BF16)]
        + [jax.ShapeDtypeStruct(a.shape, BF16) for a in later_weights],
        scratch_shapes=[pltpu.VMEM((2, SEQ, Q_TILE), F32),
                        pltpu.VMEM((HEAD_DIM, N_HEADS * Q_TILE), F32),
                        pltpu.VMEM((1, N_HEADS * Q_TILE), F32),
                        pltpu.VMEM((2, 1, Q_TILE), F32)],
        compiler_params=pltpu.CompilerParams(dimension_semantics=("arbitrary", "arbitrary"),
                                             vmem_limit_bytes=VMEM_LIMIT),
        name="dsa_attention",
    )(qT, qiT, wiT, k, ki, vT, *later_weights)


def _post_kernel(x_ref, mod_ref, u_ref, vln_ref, sga_ref, sgb_ref, yb_ref, ws_ref, bsT_ref,
                 wpa_ref, wpb_ref, wout_ref, n2g_ref, wfi_ref, wfo_ref, fg_ref, o_ref, ya_ref):
    tm = x_ref.shape[0]
    mod = mod_ref[0]
    gate1 = mod[:, 2 * D_MODEL:3 * D_MODEL]
    shift2 = mod[:, 3 * D_MODEL:4 * D_MODEL]
    scale2 = mod[:, 4 * D_MODEL:5 * D_MODEL]
    gate2 = mod[:, 5 * D_MODEL:6 * D_MODEL]

    r_i = lax.broadcasted_iota(jnp.int32, (CHUNK, CHUNK), 0)
    c_i = lax.broadcasted_iota(jnp.int32, (CHUNK, CHUNK), 1)
    bsT = bsT_ref[...]
    gd = A_WIDTH // A_GROUPS
    for g in range(A_GROUPS):
        wm = jnp.where(r_i >= c_i, ws_ref[g], 0.0).astype(BF16)
        bias = bsT[:, g:g + 1]
        for c in range(tm // CHUNK):
            rows = slice(c * CHUNK, (c + 1) * CHUNK)
            cols = slice(g * gd, (g + 1) * gd)
            mixed = jnp.dot(wm, vln_ref[rows, cols], preferred_element_type=F32) + bias
            ya_ref[rows, cols] = (u_ref[rows, cols].astype(F32) * mixed).astype(BF16)

    pa = jnp.dot(ya_ref[...], wpa_ref[...], preferred_element_type=F32)
    pb = jnp.dot(yb_ref[...], wpb_ref[...], preferred_element_type=F32)
    merged = sga_ref[...].astype(F32) * pa + sgb_ref[...].astype(F32) * pb
    x1 = x_ref[...] + gate1 * jnp.dot(merged.astype(BF16), wout_ref[...], preferred_element_type=F32)

    h2 = (_rms_norm(x1, n2g_ref[...]) * (1.0 + scale2) + shift2).astype(BF16)
    fg = jnp.dot(h2, wfi_ref[:, 0:D_FF], preferred_element_type=F32)
    fu = jnp.dot(h2, wfi_ref[:, D_FF:], preferred_element_type=F32)
    act = (fg * _sigmoid(fg) * fu).astype(BF16)
    x2 = x1 + gate2 * jnp.dot(act, wfo_ref[...], preferred_element_type=F32)
    o_ref[...] = _rms_norm(x2, fg_ref[...])


def _post_call(x2, mod3, u, vln, sga, sgb, yb, ws, bsT, wpa, wpb, wout, n2g, wfi, wfo, fg):
    tm = TM_OUT
    per_b = SEQ // tm
    row = lambda t: (t, 0)
    const2 = lambda t: (0, 0)
    tok = lambda: pl.BlockSpec((tm, D_MODEL), row)
    in_specs = [
        tok(),
        pl.BlockSpec((1, 1, 6 * D_MODEL), lambda t: (t // per_b, 0, 0)),
        tok(), tok(), tok(), tok(), tok(),
        pl.BlockSpec(ws.shape, lambda t: (0, 0, 0)),
        pl.BlockSpec(bsT.shape, const2),
        pl.BlockSpec(wpa.shape, const2),
        pl.BlockSpec(wpb.shape, const2),
        pl.BlockSpec(wout.shape, const2),
        pl.BlockSpec((1, D_MODEL), const2),
        pl.BlockSpec(wfi.shape, const2),
        pl.BlockSpec(wfo.shape, const2),
        pl.BlockSpec((1, D_MODEL), const2),
    ]
    return pl.pallas_call(
        _post_kernel,
        grid=(TOKENS // tm,),
        in_specs=in_specs,
        out_specs=pl.BlockSpec((tm, D_MODEL), row),
        out_shape=jax.ShapeDtypeStruct((TOKENS, D_MODEL), F32),
        scratch_shapes=[pltpu.VMEM((tm, A_WIDTH), BF16)],
        compiler_params=pltpu.CompilerParams(dimension_semantics=("arbitrary",),
                                             vmem_limit_bytes=VMEM_LIMIT),
        name="merge_ffn",
    )(x2, mod3, u, vln, sga, sgb, yb, ws, bsT, wpa, wpb, wout, n2g, wfi, wfo, fg)


def _inv_freq_table():
    fq = ROPE_THETA ** (-np.arange(0, ROT_DIM, 2, dtype=np.float32) / ROT_DIM)
    fi = ROPE_THETA ** (-np.arange(0, IDX_ROT_DIM, 2, dtype=np.float32) / IDX_ROT_DIM)
    return np.concatenate([fq, fi]).astype(np.float32).reshape(-1, 1)


def kernel(x, c, positions, w_ada, b_ada, norm1_g, w_in, gmlp_ln_g, gmlp_ln_b, gmlp_w_s, gmlp_b_s,
           idx_k_ln_g, idx_k_ln_b, w_proj_a, w_proj_b, w_out, norm2_g, w_ffn_in, w_ffn_out, final_norm_g):
    assert x.shape == (BATCH, SEQ, D_MODEL) and w_in.shape == (1, D_MODEL, _IN_COLS)
    x2 = x.reshape(TOKENS, D_MODEL)
    pos3 = positions.reshape(BATCH, 1, SEQ)
    invf = jnp.asarray(_inv_freq_table())
    xcur = x2
    for l in range(w_ada.shape[0]):
        mod = _ada_call(c, w_ada[l], b_ada[l])
        mod3 = mod.reshape(BATCH, 1, 6 * D_MODEL)
        u, vln, sga, sgb, qT, k, vT, qiT, ki, wiT = _inproj_call(
            xcur, mod3, pos3, norm1_g[l].reshape(1, -1), w_in[l].astype(BF16),
            gmlp_ln_g[l].reshape(1, -1), gmlp_ln_b[l].reshape(1, -1),
            idx_k_ln_g[l].reshape(-1, 1), idx_k_ln_b[l].reshape(-1, 1), invf, ())
        yb, wpa, wpb, wout, wfi, wfo = _attn_call(
            qT, qiT, wiT, k, ki, vT, (w_proj_a[l], w_proj_b[l], w_out[l], w_ffn_in[l], w_ffn_out[l]))
        last = l == w_ada.shape[0] - 1
        fg = final_norm_g.reshape(1, -1) if last else None
        assert last, "single-layer block"
        xcur = _post_call(xcur, mod3, u, vln, sga, sgb, yb, gmlp_w_s[l], gmlp_b_s[l].T,
                          wpa, wpb, wout, norm2_g[l].reshape(1, -1), wfi, wfo, fg)
    return xcur.reshape(BATCH, SEQ, D_MODEL)
```
